```python
import math
import jax, jax.numpy as jnp
from jax import lax
import numpy as np

D_MODEL = 1024
BATCH = 4
SEQ = 8192
DEPTH = 2

N_A_LAYERS = max(1, DEPTH // 2)
N_B_LAYERS = DEPTH - N_A_LAYERS

GMLP_DIM = 2 * D_MODEL
GMLP_CHUNK = 128
GMLP_GROUPS = 8
GMLP_GDIM = GMLP_DIM // GMLP_GROUPS

N_HEADS = 16
HEAD_DIM = 64
N_KV_GROUPS = 4
HEADS_PER_GROUP = N_HEADS // N_KV_GROUPS
N_BRANCH = 3
CMP_BLOCK = 32
CMP_STRIDE = 16
CMP_HIDDEN = 256
SEL_BLOCK = 64
SEL_TOP = 16
WINDOW = 512
Q_BLOCK = 64
N_KV_SLOTS = 6

REL_BUCKETS = 32
REL_MAX_DIST = 2048

N_EXPERTS = 32
TOP_K = 4
EXPERT_DIM = D_MODEL
SWIGLU_LIMIT = 7.0
SWIGLU_ALPHA = 1.702
MOE_BLOCK = 512

PLE_DIM = 256

NORM_EPS = 1e-6
NEG_INF = -1e30
FORCE = 1e30

kernel_name = "yoco_gmlp_nsa_moe_trunk"


def rms_norm(x, g):
    xf = x.astype(jnp.float32)
    y = xf * lax.rsqrt(jnp.mean(xf * xf, axis=-1, keepdims=True) + NORM_EPS)
    return (y * g.astype(jnp.float32)).astype(x.dtype)


def layer_norm(x, g, b):
    xf = x.astype(jnp.float32)
    mu = jnp.mean(xf, axis=-1, keepdims=True)
    xc = xf - mu
    y = xc * lax.rsqrt(jnp.mean(xc * xc, axis=-1, keepdims=True) + NORM_EPS)
    return (y * g.astype(jnp.float32) + b.astype(jnp.float32)).astype(x.dtype)


def t5_bucket(dist):
    n = jnp.maximum(dist, 0)
    max_exact = REL_BUCKETS // 2
    nf = jnp.maximum(n, 1).astype(jnp.float32)
    large = max_exact + (jnp.log(nf / max_exact) / math.log(REL_MAX_DIST / max_exact)
                         * (REL_BUCKETS - max_exact)).astype(jnp.int32)
    return jnp.where(n < max_exact, n, jnp.minimum(large, REL_BUCKETS - 1))


def gmlp_mixer(xn, w_in, ln_g, ln_b, w_s, b_s, w_out):
    B_, S_, _ = xn.shape
    z = jax.nn.gelu(xn @ w_in)
    u, v = jnp.split(z, 2, axis=-1)
    v = layer_norm(v, ln_g, ln_b)
    v = v.reshape(B_, S_ // GMLP_CHUNK, GMLP_CHUNK, GMLP_GROUPS, GMLP_GDIM)
    causal = jnp.tril(jnp.ones((GMLP_CHUNK, GMLP_CHUNK), dtype=w_s.dtype))
    mixed = jnp.einsum('gts,bcsgd->bctgd', w_s * causal, v) + jnp.transpose(b_s)[:, :, None]
    return (u * mixed.reshape(B_, S_, GMLP_DIM)) @ w_out


def shared_kv(h, kv_norm, kv_w, cmp_pos, cmp_w1, cmp_b1, cmp_w2, k_norm):
    B_, S_, _ = h.shape
    kv = (rms_norm(h, kv_norm) @ kv_w).reshape(B_, S_, N_KV_SLOTS, N_KV_GROUPS, HEAD_DIM)
    k_c, v_c, k_s, v_s, k_w, v_w = [kv[:, :, j] for j in range(N_KV_SLOTS)]
    n_cmp = (S_ - CMP_BLOCK) // CMP_STRIDE + 1
    blk_idx = np.arange(n_cmp)[:, None] * CMP_STRIDE + np.arange(CMP_BLOCK)[None, :]

    def compress(t, j):
        blocks = t[:, blk_idx] + cmp_pos[j][None, None, :, None, :]
        w1 = cmp_w1[j].reshape(CMP_BLOCK, HEAD_DIM, CMP_HIDDEN)
        hid = jax.nn.gelu(jnp.einsum('bnlgd,ldf->bngf', blocks, w1) + cmp_b1[j])
        return jnp.einsum('bngf,fd->bgnd', hid, cmp_w2[j])

    k_cmp = rms_norm(compress(k_c, 0), k_norm[0])
    v_cmp = compress(v_c, 1)
    n_sel = S_ // SEL_BLOCK

    def to_blocks(t):
        return jnp.transpose(t, (0, 2, 1, 3)).reshape(B_, N_KV_GROUPS, n_sel, SEL_BLOCK, HEAD_DIM)

    def pad_front(t):
        return jnp.pad(jnp.transpose(t, (0, 2, 1, 3)), ((0, 0), (0, 0), (WINDOW, 0), (0, 0)))

    return (k_cmp, v_cmp,
            to_blocks(rms_norm(k_s, k_norm[1])), to_blocks(v_s),
            pad_front(rms_norm(k_w, k_norm[2])), pad_front(v_w))


def gather_blocks(blocks, idx):
    return jax.vmap(jax.vmap(lambda bl, ix: bl[ix]))(blocks, idx)


def nsa_mixer(xn, w_in, b_gate, q_norm, w_out, rel_bias,
              k_cmp, v_cmp, k_sel, v_sel, k_win, v_win):
    B_, S_, _ = xn.shape
    HD = N_HEADS * HEAD_DIM
    proj = xn @ w_in
    q = proj[..., :HD].reshape(B_, S_, N_HEADS, HEAD_DIM)
    q = rms_norm(q, q_norm) * (HEAD_DIM ** -0.5)
    gates = jax.nn.sigmoid((proj[..., HD:] + b_gate).astype(jnp.float32))
    n_qb = S_ // Q_BLOCK
    n_cmp = k_cmp.shape[2]
    n_sel = S_ // SEL_BLOCK
    n_top = min(SEL_TOP, n_sel)
    q_blocks = q.reshape(B_, n_qb, Q_BLOCK, N_KV_GROUPS, HEADS_PER_GROUP, HEAD_DIM).transpose(1, 0, 2, 3, 4, 5)
    g_blocks = gates.reshape(B_, n_qb, Q_BLOCK, N_BRANCH, N_KV_GROUPS, HEADS_PER_GROUP).transpose(1, 0, 2, 3, 4, 5)

    cmp_start = np.arange(n_cmp) * CMP_STRIDE
    sel_start = np.arange(n_sel) * SEL_BLOCK
    overlap = np.clip(np.minimum(cmp_start[:, None] + CMP_BLOCK, sel_start[None, :] + SEL_BLOCK)
                      - np.maximum(cmp_start[:, None], sel_start[None, :]), 0, None)
    overlap = jnp.asarray((overlap / CMP_BLOCK).astype(np.float32))
    cmp_end = jnp.asarray(cmp_start + CMP_BLOCK - 1, dtype=jnp.int32)
    bias_tab = rel_bias.astype(jnp.float32).reshape(REL_BUCKETS, N_KV_GROUPS, HEADS_PER_GROUP)
    tab_gr = jnp.transpose(bias_tab, (1, 2, 0))
    sel_pos = jnp.arange(SEL_BLOCK, dtype=jnp.int32)
    blk_ids = jnp.arange(n_sel, dtype=jnp.int32)

    def head_bias(dist):
        return jnp.moveaxis(bias_tab[t5_bucket(dist)], (-2, -1), (0, 1))

    def block_fn(args):
        qb, gb, i = args
        t = i * Q_BLOCK + jnp.arange(Q_BLOCK, dtype=jnp.int32)
        s = jnp.einsum('bqgrd,bgnd->bgrqn', qb, k_cmp).astype(jnp.float32)
        dist = t[:, None] - cmp_end[None, :]
        s = jnp.where(dist >= 0, s + head_bias(dist), NEG_INF)
        p_c = jax.nn.softmax(s, axis=-1) * (t >= CMP_BLOCK - 1)[:, None]
        o_c = jnp.einsum('bgrqn,bgnd->bgrqd', p_c.astype(v_cmp.dtype), v_cmp)
        imp = jnp.einsum('bgrqn,ns->bgqs', p_c, overlap)
        cur = t // SEL_BLOCK
        forced = (blk_ids[None, :] == 0) | (blk_ids[None, :] == cur[:, None]) | (blk_ids[None, :] == cur[:, None] - 1)
        future = blk_ids[None, :] > cur[:, None]
        imp = jnp.where(forced, FORCE, jnp.where(future, NEG_INF, imp))
        _, idx = lax.top_k(imp, n_top)
        kg = gather_blocks(k_sel, idx)
        vg = gather_blocks(v_sel, idx)
        s = jnp.einsum('bqgrd,bgqkjd->bgrqkj', qb, kg).astype(jnp.float32)
        pos = idx[..., None] * SEL_BLOCK + sel_pos
        dist = t[None, None, :, None, None] - pos
        sel_bias = jax.vmap(lambda tg, bk: tg[:, bk], in_axes=(0, 1), out_axes=1)(tab_gr, t5_bucket(dist))
        sel_bias = jnp.transpose(sel_bias, (2, 1, 0, 3, 4, 5))
        s = jnp.where((dist >= 0)[:, :, None], s + sel_bias, NEG_INF)
        p_s = jax.nn.softmax(s.reshape(s.shape[:4] + (n_top * SEL_BLOCK,)), axis=-1).reshape(s.shape)
        o_s = jnp.einsum('bgrqkj,bgqkjd->bgrqd', p_s.astype(vg.dtype), vg)
        kw = lax.dynamic_slice_in_dim(k_win, i * Q_BLOCK, Q_BLOCK + WINDOW, axis=2)
        vw = lax.dynamic_slice_in_dim(v_win, i * Q_BLOCK, Q_BLOCK + WINDOW, axis=2)
        wpos = i * Q_BLOCK - WINDOW + jnp.arange(Q_BLOCK + WINDOW, dtype=jnp.int32)
        dist = t[:, None] - wpos[None, :]
        valid = (dist >= 0) & (dist < WINDOW) & (wpos >= 0)[None, :]
        s = jnp.einsum('bqgrd,bgld->bgrql', qb, kw).astype(jnp.float32)
        s = jnp.where(valid, s + head_bias(dist), NEG_INF)
        p_w = jax.nn.softmax(s, axis=-1)
        o_w = jnp.einsum('bgrql,bgld->bgrqd', p_w.astype(vw.dtype), vw)
        g = jnp.transpose(gb, (2, 0, 3, 4, 1))[..., None]
        o = (g[0] * o_c + g[1] * o_s + g[2] * o_w).astype(qb.dtype)
        return jnp.transpose(o, (0, 3, 1, 2, 4)).reshape(B_, Q_BLOCK, HD)

    out = lax.map(block_fn, (q_blocks, g_blocks, jnp.arange(n_qb, dtype=jnp.int32)))
    out = jnp.transpose(out, (1, 0, 2, 3)).reshape(B_, S_, HD)
    return out @ w_out


def moe(xn, router_w, router_b, w_gu, b_gu, w_d, b_d):
    B_, S_, D_ = xn.shape
    xt = xn.reshape(-1, D_)
    n_tok = xt.shape[0]
    logits = (xt @ router_w + router_b).astype(jnp.float32)
    top_val, top_idx = lax.top_k(logits, TOP_K)
    top_w = jax.nn.softmax(top_val, axis=-1)
    n_asg = n_tok * TOP_K
    e_flat = top_idx.reshape(-1).astype(jnp.int32)
    tok_flat = jnp.arange(n_asg, dtype=jnp.int32) // TOP_K
    w_flat = top_w.reshape(-1)
    e_sorted, order = lax.sort((e_flat, jnp.arange(n_asg, dtype=jnp.int32)), num_keys=1, is_stable=True)
    tok_sorted = tok_flat[order]
    w_sorted = w_flat[order]
    counts = jnp.bincount(e_flat, length=N_EXPERTS)
    starts = jnp.cumsum(counts) - counts
    pad_counts = (counts + MOE_BLOCK - 1) // MOE_BLOCK * MOE_BLOCK
    pad_ends = jnp.cumsum(pad_counts)
    pad_starts = pad_ends - pad_counts
    dest = pad_starts[e_sorted] + jnp.arange(n_asg, dtype=jnp.int32) - starts[e_sorted]
    n_blk = -(-n_asg // MOE_BLOCK) + N_EXPERTS
    n_rows = n_blk * MOE_BLOCK
    row_tok = jnp.zeros((n_rows,), jnp.int32).at[dest].set(tok_sorted)
    row_w = jnp.zeros((n_rows,), jnp.float32).at[dest].set(w_sorted)
    blk_expert = jnp.minimum(jnp.searchsorted(pad_ends, jnp.arange(n_blk, dtype=jnp.int32) * MOE_BLOCK, side='right'),
                             N_EXPERTS - 1)

    def expert_block(args):
        tok, w, e = args
        gu = xt[tok] @ w_gu[e] + b_gu[e]
        gate, up = jnp.split(gu, 2, axis=-1)
        gate = jnp.minimum(gate, SWIGLU_LIMIT)
        up = jnp.clip(up, -SWIGLU_LIMIT, SWIGLU_LIMIT)
        glu = gate * jax.nn.sigmoid(gate * SWIGLU_ALPHA)
        y = ((up + 1.0) * glu) @ w_d[e] + b_d[e]
        return y * w[:, None].astype(y.dtype)

    ys = lax.map(expert_block, (row_tok.reshape(n_blk, MOE_BLOCK), row_w.reshape(n_blk, MOE_BLOCK), blk_expert))
    out = jax.ops.segment_sum(ys.reshape(n_rows, D_), row_tok, num_segments=n_tok)
    return out.reshape(B_, S_, D_)


def setup_inputs(seed: int = 0) -> dict:
    key = jax.random.key(seed)
    ks = iter(jax.random.split(key, 40))
    f32 = jnp.float32

    def nrm(shape, scale):
        return jax.random.normal(next(ks), shape, f32) * scale

    def gain(shape):
        return 1.0 + nrm(shape, 0.1)

    HD = N_HEADS * HEAD_DIM
    return {
        "x": nrm((BATCH, SEQ, D_MODEL), 1.0),
        "p": nrm((DEPTH, BATCH, SEQ, PLE_DIM), 1.0),
        "rel_bias": nrm((REL_BUCKETS, N_HEADS), 0.5),
        "norm_mix": gain((DEPTH, D_MODEL)),
        "norm_ffn": gain((DEPTH, D_MODEL)),
        "a_w_in": nrm((N_A_LAYERS, D_MODEL, 2 * GMLP_DIM), D_MODEL ** -0.5),
        "a_ln_g": gain((N_A_LAYERS, GMLP_DIM)),
        "a_ln_b": nrm((N_A_LAYERS, GMLP_DIM), 0.1),
        "a_w_s": nrm((N_A_LAYERS, GMLP_GROUPS, GMLP_CHUNK, GMLP_CHUNK), GMLP_CHUNK ** -0.5),
        "a_b_s": gain((N_A_LAYERS, GMLP_GROUPS, GMLP_CHUNK)),
        "a_w_out": nrm((N_A_LAYERS, GMLP_DIM, D_MODEL), GMLP_DIM ** -0.5),
        "kv_norm": gain((D_MODEL,)),
        "kv_w": nrm((D_MODEL, N_KV_SLOTS * N_KV_GROUPS * HEAD_DIM), D_MODEL ** -0.5),
        "cmp_pos": nrm((2, CMP_BLOCK, HEAD_DIM), 0.1),
        "cmp_w1": nrm((2, CMP_BLOCK * HEAD_DIM, CMP_HIDDEN), (CMP_BLOCK * HEAD_DIM) ** -0.5),
        "cmp_b1": nrm((2, CMP_HIDDEN), 0.01),
        "cmp_w2": nrm((2, CMP_HIDDEN, HEAD_DIM), CMP_HIDDEN ** -0.5),
        "k_norm": gain((N_BRANCH, HEAD_DIM)),
        "b_w_in": nrm((N_B_LAYERS, D_MODEL, HD + N_BRANCH * N_HEADS), D_MODEL ** -0.5),
        "b_b_gate": nrm((N_B_LAYERS, N_BRANCH * N_HEADS), 0.1),
        "q_norm": gain((N_B_LAYERS, HEAD_DIM)),
        "b_w_out": nrm((N_B_LAYERS, HD, D_MODEL), HD ** -0.5),
        "router_w": nrm((DEPTH, D_MODEL, N_EXPERTS), D_MODEL ** -0.5),
        "router_b": nrm((DEPTH, N_EXPERTS), 0.01),
        "e_w_gu": nrm((DEPTH, N_EXPERTS, D_MODEL, 2 * EXPERT_DIM), D_MODEL ** -0.5),
        "e_b_gu": nrm((DEPTH, N_EXPERTS, 2 * EXPERT_DIM), 0.01),
        "e_w_d": nrm((DEPTH, N_EXPERTS, EXPERT_DIM, D_MODEL), EXPERT_DIM ** -0.5),
        "e_b_d": nrm((DEPTH, N_EXPERTS, D_MODEL), 0.01),
        "ple_w": nrm((DEPTH, PLE_DIM, D_MODEL), PLE_DIM ** -0.5),
        "ple_gate_w": nrm((DEPTH, D_MODEL, D_MODEL), D_MODEL ** -0.5),
        "ple_norm": gain((DEPTH, D_MODEL)),
    }


def reference(x, p, rel_bias, norm_mix, norm_ffn, a_w_in, a_ln_g, a_ln_b, a_w_s, a_b_s, a_w_out,
              kv_norm, kv_w, cmp_pos, cmp_w1, cmp_b1, cmp_w2, k_norm,
              b_w_in, b_b_gate, q_norm, b_w_out,
              router_w, router_b, e_w_gu, e_b_gu, e_w_d, e_b_d,
              ple_w, ple_gate_w, ple_norm):
    h = x
    kv = None
    for i in range(DEPTH):
        xn = rms_norm(h, norm_mix[i])
        if i < N_A_LAYERS:
            h = h + gmlp_mixer(xn, a_w_in[i], a_ln_g[i], a_ln_b[i], a_w_s[i], a_b_s[i], a_w_out[i])
        else:
            j = i - N_A_LAYERS
            h = h + nsa_mixer(xn, b_w_in[j], b_b_gate[j], q_norm[j], b_w_out[j], rel_bias, *kv)
        h = h + moe(rms_norm(h, norm_ffn[i]), router_w[i], router_b[i], e_w_gu[i], e_b_gu[i], e_w_d[i], e_b_d[i])
        h = h + (p[i] @ ple_w[i]) * jax.nn.sigmoid(rms_norm(h, ple_norm[i]) @ ple_gate_w[i])
        if i == N_A_LAYERS - 1:
            kv = shared_kv(h, kv_norm, kv_w, cmp_pos, cmp_w1, cmp_b1, cmp_w2, k_norm)
    return h
```

```python
import functools
import math

import numpy as np
import jax
import jax.numpy as jnp
from jax import lax
from jax.experimental import pallas as pl
from jax.experimental.pallas import tpu as pltpu

D_MODEL = 1024
GMLP_CHUNK = 128
GMLP_GROUPS = 8
N_HEADS = 16
HEAD_DIM = 64
N_KV_GROUPS = 4
N_BRANCH = 3
CMP_BLOCK = 32
CMP_STRIDE = 16
CMP_HIDDEN = 256
SEL_BLOCK = 64
SEL_TOP = 16
WINDOW = 512
Q_BLOCK = 64
N_KV_SLOTS = 6
REL_BUCKETS = 32
REL_MAX_DIST = 2048
N_EXPERTS = 32
TOP_K = 4
SWIGLU_LIMIT = 7.0
SWIGLU_ALPHA = 1.702
NORM_EPS = 1e-6
NEG_INF = -1e30
FORCE = 1e30

LANES = 128
MOE_ROWS = 512
KEY_CHUNK = 512
VMEM_LIMIT = 56 * 1024 * 1024

F32 = jnp.float32
BF16 = jnp.bfloat16


def _cparams(sem):
    return pltpu.CompilerParams(dimension_semantics=sem, vmem_limit_bytes=VMEM_LIMIT)


def _rms(x, g):
    return x * lax.rsqrt(jnp.mean(x * x, axis=-1, keepdims=True) + NORM_EPS) * g


def _dot(a, b):
    return jnp.dot(a, b, preferred_element_type=F32)


def _dot_nt(a, b):
    return lax.dot_general(a, b, (((1,), (1,)), ((), ())), preferred_element_type=F32)


def _dot_exact(a, b):
    return jnp.dot(a, b, preferred_element_type=F32, precision=lax.Precision.HIGHEST)


def _softmax_rows(s):
    m = jnp.max(s, axis=-1, keepdims=True)
    p = jnp.exp(s - m)
    return p / jnp.sum(p, axis=-1, keepdims=True)


def _head_lanes(x, idx):
    base = idx * HEAD_DIM // LANES * LANES
    y = x[:, base:base + LANES]
    return y if idx * HEAD_DIM == base else pltpu.roll(y, LANES - HEAD_DIM, 1)


def _low_lanes(x, fill):
    lane = lax.broadcasted_iota(jnp.int32, x.shape, 1)
    return jnp.where(lane < HEAD_DIM, x, fill)


def _argmax_first(x, ids, n):
    mx = jnp.max(x, axis=-1, keepdims=True)
    return mx, jnp.min(jnp.where(x == mx, ids, float(n)), axis=-1, keepdims=True)


def _gmlp_body(x_ref, nm_ref, win_ref, lng_ref, lnb_ref, ws_ref, bs_ref, wout_ref, o_ref):
    tm = x_ref.shape[0]
    gd = win_ref.shape[1] // 2
    gdim = gd // GMLP_GROUPS
    x = x_ref[...]
    xn = _rms(x, nm_ref[...]).astype(BF16)
    z = jax.nn.gelu(_dot(xn, win_ref[...]))
    u = z[:, :gd]
    v = z[:, gd:]
    mu = jnp.mean(v, axis=-1, keepdims=True)
    vc = v - mu
    vln = vc * lax.rsqrt(jnp.mean(vc * vc, axis=-1, keepdims=True) + NORM_EPS) * lng_ref[...] + lnb_ref[...]
    vb = vln.astype(BF16)
    row = lax.broadcasted_iota(jnp.int32, (GMLP_CHUNK, GMLP_CHUNK), 0)
    col = lax.broadcasted_iota(jnp.int32, (GMLP_CHUNK, GMLP_CHUNK), 1)
    causal = row >= col
    chunks = []
    for c in range(tm // GMLP_CHUNK):
        cols = []
        for g in range(GMLP_GROUPS):
            wsg = jnp.where(causal, ws_ref[g], 0.0).astype(BF16)
            vg = vb[c * GMLP_CHUNK:(c + 1) * GMLP_CHUNK, g * gdim:(g + 1) * gdim]
            cols.append(_dot(wsg, vg) + bs_ref[g])
        chunks.append(jnp.concatenate(cols, axis=1))
    mixed = jnp.concatenate(chunks, axis=0)
    gated = (u * mixed).astype(BF16)
    o_ref[...] = x + _dot(gated, wout_ref[...])


def _gmlp_layer(h, norm_g, w_in, ln_g, ln_b, w_s, b_s, w_out):
    n, d = h.shape
    gd2 = w_in.shape[1]
    gd = gd2 // 2
    tm = min(512, n)
    full = lambda *shape: pl.BlockSpec(shape, lambda i: (0,) * len(shape))
    return pl.pallas_call(
        _gmlp_body,
        grid=(n // tm,),
        in_specs=[pl.BlockSpec((tm, d), lambda i: (i, 0)),
                  full(1, d), full(d, gd2), full(1, gd), full(1, gd),
                  full(GMLP_GROUPS, GMLP_CHUNK, GMLP_CHUNK), full(GMLP_GROUPS, GMLP_CHUNK, 1),
                  full(gd, d)],
        out_specs=pl.BlockSpec((tm, d), lambda i: (i, 0)),
        out_shape=jax.ShapeDtypeStruct((n, d), F32),
        compiler_params=_cparams(("arbitrary",)),
        name="gmlp_layer",
    )(h, norm_g.reshape(1, d), w_in.astype(BF16), ln_g.reshape(1, gd), ln_b.reshape(1, gd),
      w_s, b_s.reshape(GMLP_GROUPS, GMLP_CHUNK, 1), w_out.astype(BF16))


def _route_body(h_ref, ng_ref, rw_ref, rb_ref, o_ref, cnt_ref, run_ref):
    i = pl.program_id(0)
    tm = h_ref.shape[0]

    @pl.when(i == 0)
    def _():
        run_ref[...] = jnp.zeros_like(run_ref)

    xn = _rms(h_ref[...], ng_ref[...])
    logits = _dot_exact(xn, rw_ref[...]) + rb_ref[...]
    eid = lax.broadcasted_iota(jnp.int32, logits.shape, 1).astype(F32)
    lane = lax.broadcasted_iota(jnp.int32, (tm, LANES), 1)
    work = logits
    vals, idxs = [], []
    for _ in range(TOP_K):
        mx, ix = _argmax_first(work, eid, N_EXPERTS)
        vals.append(mx)
        idxs.append(ix)
        work = jnp.where(eid == ix, -jnp.inf, work)
    exps = [jnp.exp(v - vals[0]) for v in vals]
    den = exps[0]
    for e in exps[1:]:
        den = den + e
    onehot = jnp.zeros(logits.shape, F32)
    for ix in idxs:
        onehot = onehot + (eid == ix).astype(F32)
    r = lax.broadcasted_iota(jnp.int32, (tm, tm), 0)
    c = lax.broadcasted_iota(jnp.int32, (tm, tm), 1)
    before = (r > c).astype(BF16)
    prefix = _dot(before, onehot.astype(BF16)) + run_ref[...]
    out = jnp.zeros((tm, LANES), F32)
    for k in range(TOP_K):
        rank = jnp.sum(jnp.where(eid == idxs[k], prefix, 0.0), axis=-1, keepdims=True)
        out = jnp.where(lane == k, idxs[k], out)
        out = jnp.where(lane == TOP_K + k, exps[k] / den, out)
        out = jnp.where(lane == 2 * TOP_K + k, rank, out)
    o_ref[...] = out
    run_ref[...] = run_ref[...] + jnp.sum(onehot, axis=0, keepdims=True)
    cnt_ref[...] = run_ref[...]


def _moe_route(h, norm_g, router_w, router_b):
    n, d = h.shape
    tm = min(512, n)
    full = lambda *shape: pl.BlockSpec(shape, lambda i: (0,) * len(shape))
    return pl.pallas_call(
        _route_body,
        grid=(n // tm,),
        in_specs=[pl.BlockSpec((tm, d), lambda i: (i, 0)), full(1, d), full(d, N_EXPERTS), full(1, N_EXPERTS)],
        out_specs=[pl.BlockSpec((tm, LANES), lambda i: (i, 0)), full(1, N_EXPERTS)],
        out_shape=[jax.ShapeDtypeStruct((n, LANES), F32), jax.ShapeDtypeStruct((1, N_EXPERTS), F32)],
        scratch_shapes=[pltpu.VMEM((1, N_EXPERTS), F32)],
        compiler_params=_cparams(("arbitrary",)),
        name="moe_route",
    )(h, norm_g.reshape(1, d), router_w, router_b.reshape(1, N_EXPERTS))


def _dispatch_body(dest_ref, h_ref, ng_ref, xs_in, xs_out, buf, sem):
    del xs_in
    tm = h_ref.shape[0]
    buf[...] = _rms(h_ref[...], ng_ref[...])

    def issue(j, carry):
        for k in range(TOP_K):
            r = dest_ref[0, 0, j * TOP_K + k]
            pltpu.make_async_copy(buf.at[pl.ds(j, 1), :], xs_out.at[pl.ds(r, 1), :], sem).start()
        return carry

    lax.fori_loop(0, tm, issue, 0)
    for _ in range(TOP_K):
        pltpu.make_async_copy(buf, xs_out.at[pl.ds(0, tm), :], sem).wait()


def _moe_dispatch(h, norm_g, dest, n_rows):
    n, d = h.shape
    tm = min(256, n)
    xs0 = jnp.zeros((n_rows, d), F32)
    return pl.pallas_call(
        _dispatch_body,
        grid=(n // tm,),
        in_specs=[pl.BlockSpec((1, 1, tm * TOP_K), lambda i: (i, 0, 0), memory_space=pltpu.SMEM),
                  pl.BlockSpec((tm, d), lambda i: (i, 0)),
                  pl.BlockSpec((1, d), lambda i: (0, 0)),
                  pl.BlockSpec(memory_space=pl.ANY)],
        out_specs=pl.BlockSpec(memory_space=pl.ANY),
        out_shape=jax.ShapeDtypeStruct((n_rows, d), F32),
        scratch_shapes=[pltpu.VMEM((tm, d), F32), pltpu.SemaphoreType.DMA(())],
        input_output_aliases={3: 0},
        compiler_params=_cparams(("arbitrary",)),
        name="moe_dispatch",
    )(dest.reshape(n // tm, 1, tm * TOP_K), h, norm_g.reshape(1, d), xs0)


def _expert_body(be_ref, nu_ref, xs_ref, wgu_ref, bgu_ref, wd_ref, bd_ref, ys_ref):
    i = pl.program_id(0)
    ed = wd_ref.shape[1]

    @pl.when(i < nu_ref[0])
    def _():
        x = xs_ref[...].astype(BF16)
        gu = _dot(x, wgu_ref[0]) + bgu_ref[0]
        gate = jnp.minimum(gu[:, :ed], SWIGLU_LIMIT)
        up = jnp.clip(gu[:, ed:], -SWIGLU_LIMIT, SWIGLU_LIMIT)
        glu = gate * jax.nn.sigmoid(gate * SWIGLU_ALPHA)
        ys_ref[...] = _dot(((up + 1.0) * glu).astype(BF16), wd_ref[0]) + bd_ref[0]

    @pl.when(i >= nu_ref[0])
    def _():
        ys_ref[...] = jnp.zeros_like(ys_ref)


def _moe_experts(xs, blk_expert, n_used, w_gu, b_gu, w_d, b_d):
    n_rows, d = xs.shape
    ed = w_d.shape[1]
    n_blk = n_rows // MOE_ROWS
    return pl.pallas_call(
        _expert_body,
        grid_spec=pltpu.PrefetchScalarGridSpec(
            num_scalar_prefetch=2, grid=(n_blk,),
            in_specs=[pl.BlockSpec((MOE_ROWS, d), lambda i, be, nu: (jnp.minimum(i, nu[0] - 1), 0)),
                      pl.BlockSpec((1, d, 2 * ed), lambda i, be, nu: (be[i], 0, 0)),
                      pl.BlockSpec((1, 1, 2 * ed), lambda i, be, nu: (be[i], 0, 0)),
                      pl.BlockSpec((1, ed, d), lambda i, be, nu: (be[i], 0, 0)),
                      pl.BlockSpec((1, 1, d), lambda i, be, nu: (be[i], 0, 0))],
            out_specs=pl.BlockSpec((MOE_ROWS, d), lambda i, be, nu: (i, 0))),
        out_shape=jax.ShapeDtypeStruct((n_rows, d), F32),
        compiler_params=_cparams(("arbitrary",)),
        name="moe_experts",
    )(blk_expert, n_used, xs, w_gu.astype(BF16), b_gu.reshape(N_EXPERTS, 1, 2 * ed),
      w_d.astype(BF16), b_d.reshape(N_EXPERTS, 1, d))


def _combine_body(dest_ref, rt_ref, h_ref, p_ref, pw_ref, pg_ref, pn_ref, ys_hbm, o_ref, buf, sem):
    tm = h_ref.shape[0]

    def issue(j, carry):
        for k in range(TOP_K):
            r = dest_ref[0, 0, j * TOP_K + k]
            pltpu.make_async_copy(ys_hbm.at[pl.ds(r, 1), :], buf.at[k, pl.ds(j, 1), :], sem).start()
        return carry

    lax.fori_loop(0, tm, issue, 0)
    for k in range(TOP_K):
        pltpu.make_async_copy(ys_hbm.at[pl.ds(0, tm), :], buf.at[k], sem).wait()
    rt = rt_ref[...]
    h = h_ref[...]
    for k in range(TOP_K):
        h = h + rt[:, TOP_K + k:TOP_K + k + 1] * buf[k]
    emb = _dot(p_ref[...].astype(BF16), pw_ref[...])
    gate = jax.nn.sigmoid(_dot(_rms(h, pn_ref[...]).astype(BF16), pg_ref[...]))
    o_ref[...] = h + emb * gate


def _moe_combine_ple(h, route, dest, ys, p, ple_w, ple_gate_w, ple_norm):
    n, d = h.shape
    pd = p.shape[1]
    tm = min(256, n)
    full = lambda *shape: pl.BlockSpec(shape, lambda i: (0,) * len(shape))
    return pl.pallas_call(
        _combine_body,
        grid=(n // tm,),
        in_specs=[pl.BlockSpec((1, 1, tm * TOP_K), lambda i: (i, 0, 0), memory_space=pltpu.SMEM),
                  pl.BlockSpec((tm, LANES), lambda i: (i, 0)),
                  pl.BlockSpec((tm, d), lambda i: (i, 0)),
                  pl.BlockSpec((tm, pd), lambda i: (i, 0)),
                  full(pd, d), full(d, d), full(1, d),
                  pl.BlockSpec(memory_space=pl.ANY)],
        out_specs=pl.BlockSpec((tm, d), lambda i: (i, 0)),
        out_shape=jax.ShapeDtypeStruct((n, d), F32),
        scratch_shapes=[pltpu.VMEM((TOP_K, tm, d), F32), pltpu.SemaphoreType.DMA(())],
        compiler_params=_cparams(("arbitrary",)),
        name="moe_combine_ple",
    )(dest.reshape(n // tm, 1, tm * TOP_K), route, h, p, ple_w.astype(BF16), ple_gate_w.astype(BF16),
      ple_norm.reshape(1, d), ys)


def _moe_ple_layer(h, norm_g, router_w, router_b, w_gu, b_gu, w_d, b_d, p, ple_w, ple_gate_w, ple_norm):
    n, _ = h.shape
    route, counts = _moe_route(h, norm_g, router_w, router_b)
    counts = counts[0].astype(jnp.int32)
    pad_counts = (counts + MOE_ROWS - 1) // MOE_ROWS * MOE_ROWS
    pad_ends = jnp.cumsum(pad_counts)
    pad_starts = pad_ends - pad_counts
    top_idx = route[:, :TOP_K].astype(jnp.int32)
    rank = route[:, 2 * TOP_K:3 * TOP_K].astype(jnp.int32)
    dest = (pad_starts[top_idx] + rank).reshape(-1)
    n_blk = -(-(n * TOP_K) // MOE_ROWS) + N_EXPERTS
    blk_expert = jnp.minimum(
        jnp.searchsorted(pad_ends, jnp.arange(n_blk, dtype=jnp.int32) * MOE_ROWS, side='right'),
        N_EXPERTS - 1).astype(jnp.int32)
    n_used = (pad_ends[-1:] // MOE_ROWS).astype(jnp.int32)
    xs = _moe_dispatch(h, norm_g, dest, n_blk * MOE_ROWS)
    ys = _moe_experts(xs, blk_expert, n_used, w_gu, b_gu, w_d, b_d)
    return _moe_combine_ple(h, route, dest, ys, p, ple_w, ple_gate_w, ple_norm)


def _kv_body(h_ref, ng_ref, w_ref, seg_ref, kn_ref, kc_ref, vc_ref, ks_ref, vs_ref, kw_ref, vw_ref):
    ts = h_ref.shape[1]
    gw = N_KV_GROUPS * HEAD_DIM
    st = pl.program_id(1)
    hn = _rms(h_ref[0], ng_ref[...]).astype(BF16)
    kv = _dot(hn, w_ref[...])

    def knorm(x, j):
        ms = _dot_exact(x * x, seg_ref[...])
        return x * lax.rsqrt(ms + NORM_EPS) * kn_ref[j]

    k_c, v_c = kv[:, 0:gw], kv[:, gw:2 * gw]
    k_s, v_s = knorm(kv[:, 2 * gw:3 * gw], 1), kv[:, 3 * gw:4 * gw]
    k_w, v_w = knorm(kv[:, 4 * gw:5 * gw], 2), kv[:, 5 * gw:6 * gw]
    tok = st * ts + lax.broadcasted_iota(jnp.int32, (ts, LANES), 0)
    blk = lax.broadcasted_iota(jnp.int32, (ts, LANES), 1)
    onehot = (tok // SEL_BLOCK == blk).astype(BF16)
    for g in range(N_KV_GROUPS):
        kc_ref[0, g] = _head_lanes(k_c, g)[:, :HEAD_DIM]
        vc_ref[0, g] = _head_lanes(v_c, g)[:, :HEAD_DIM]
        ks_ref[0, g] = jnp.concatenate([_low_lanes(_head_lanes(k_s, g), 0.0).astype(BF16), onehot], axis=1)
        vs_ref[0, g] = _low_lanes(_head_lanes(v_s, g), 0.0).astype(BF16)
        kw_ref[0, g] = _low_lanes(_head_lanes(k_w, g), 0.0).astype(BF16)
        vw_ref[0, g] = _low_lanes(_head_lanes(v_w, g), 0.0).astype(BF16)


def _kv_project(h3, kv_norm, kv_w, k_norm):
    b, s, d = h3.shape
    gw = N_KV_GROUPS * HEAD_DIM
    ts = min(512, s)
    seg = jnp.asarray(np.kron(np.eye(N_KV_GROUPS), np.full((HEAD_DIM, HEAD_DIM), 1.0 / HEAD_DIM)), F32)
    kn = jnp.tile(k_norm, (1, N_KV_GROUPS)).reshape(N_BRANCH, 1, gw)
    full = lambda *shape: pl.BlockSpec(shape, lambda bi, si: (0,) * len(shape))
    hd = lambda w: pl.BlockSpec((1, N_KV_GROUPS, ts, w), lambda bi, si: (bi, 0, si, 0))
    sds = lambda w, dt: jax.ShapeDtypeStruct((b, N_KV_GROUPS, s, w), dt)
    return pl.pallas_call(
        _kv_body,
        grid=(b, s // ts),
        in_specs=[pl.BlockSpec((1, ts, d), lambda bi, si: (bi, si, 0)), full(1, d), full(d, N_KV_SLOTS * gw),
                  full(gw, gw), full(N_BRANCH, 1, gw)],
        out_specs=[hd(HEAD_DIM), hd(HEAD_DIM), hd(2 * LANES), hd(LANES), hd(LANES), hd(LANES)],
        out_shape=[sds(HEAD_DIM, F32), sds(HEAD_DIM, F32), sds(2 * LANES, BF16), sds(LANES, BF16),
                   sds(LANES, BF16), sds(LANES, BF16)],
        compiler_params=_cparams(("arbitrary", "arbitrary")),
        name="kv_project",
    )(h3, kv_norm.reshape(1, d), kv_w.astype(BF16), seg, kn)


def _compress_body(kc_ref, vc_ref, pos_ref, w1_ref, b1_ref, w2_ref, kn_ref, ko_ref, vo_ref):
    nc = kc_ref.shape[2]
    half = kc_ref.shape[3]
    row = lax.broadcasted_iota(jnp.int32, (nc, 1), 0)
    valid = row < nc - 1

    def compress(c, j):
        a = _dot((c + pos_ref[j, 0]).astype(BF16), w1_ref[j, 0])
        bm = _dot((c + pos_ref[j, 1]).astype(BF16), w1_ref[j, 1])
        nxt = pltpu.roll(bm, nc - 1, 0)
        hid = jax.nn.gelu(a + nxt + b1_ref[j])
        return _dot(hid.astype(BF16), w2_ref[j])

    kraw = compress(kc_ref[0, 0], 0)
    ms = jnp.sum(kraw * kraw, axis=-1, keepdims=True) * (1.0 / HEAD_DIM)
    kcmp = kraw * lax.rsqrt(ms + NORM_EPS) * kn_ref[...]
    vcmp = jnp.where(valid, compress(vc_ref[0, 0], 1), 0.0)
    lane = lax.broadcasted_iota(jnp.int32, (nc, LANES), 1)
    flag = (lane == HEAD_DIM).astype(F32)
    ko_ref[0, 0, 0:nc, :] = flag
    ko_ref[0, 0, nc:2 * nc, :] = jnp.where(valid, kcmp, flag)
    vo_ref[0, 0, 0:nc, :] = jnp.zeros((nc, LANES), F32)
    vo_ref[0, 0, nc:2 * nc, :] = vcmp


def _compress(kc, vc, cmp_pos, cmp_w1, cmp_b1, cmp_w2, k_norm0):
    b, g, s, dh = kc.shape
    nc = s // CMP_STRIDE
    half = CMP_STRIDE * dh
    kc_r = kc.reshape(b, g, nc, half)
    vc_r = vc.reshape(b, g, nc, half)
    pos = cmp_pos.reshape(2, 2, 1, half)
    w1 = cmp_w1.reshape(2, 2, half, CMP_HIDDEN).astype(BF16)
    w2 = jnp.pad(cmp_w2, ((0, 0), (0, 0), (0, LANES - dh))).astype(BF16)
    kn = jnp.pad(k_norm0, (0, LANES - dh)).reshape(1, LANES)
    full = lambda *shape: pl.BlockSpec(shape, lambda bi, gi: (0,) * len(shape))
    blk = pl.BlockSpec((1, 1, nc, half), lambda bi, gi: (bi, gi, 0, 0))
    out = pl.BlockSpec((1, 1, 2 * nc, LANES), lambda bi, gi: (bi, gi, 0, 0))
    return pl.pallas_call(
        _compress_body,
        grid=(b, g),
        in_specs=[blk, blk, full(2, 2, 1, half), full(2, 2, half, CMP_HIDDEN), full(2, 1, CMP_HIDDEN),
                  full(2, CMP_HIDDEN, LANES), full(1, LANES)],
        out_specs=[out, out],
        out_shape=[jax.ShapeDtypeStruct((b, g, 2 * nc, LANES), F32)] * 2,
        compiler_params=_cparams(("arbitrary", "arbitrary")),
        name="kv_compress",
    )(kc_r, vc_r, pos, w1, cmp_b1.reshape(2, 1, CMP_HIDDEN), w2, kn)


def _qproj_body(h_ref, ng_ref, w_ref, bg_ref, ind_ref, indt_ref, qn_ref, q_ref, gate_ref):
    hd = N_HEADS * HEAD_DIM
    xn = _rms(h_ref[0], ng_ref[...]).astype(BF16)
    proj = _dot(xn, w_ref[...])
    q = proj[:, :hd]
    ms = _dot_exact(q * q, ind_ref[...]) * (1.0 / HEAD_DIM)
    scale = _dot_exact(lax.rsqrt(ms + NORM_EPS), indt_ref[...])
    qn = q * scale * qn_ref[...] * (HEAD_DIM ** -0.5)
    lane = lax.broadcasted_iota(jnp.int32, (q.shape[0], LANES), 1)
    fill = jnp.where(lane == HEAD_DIM, NEG_INF, 0.0)
    for h in range(N_HEADS):
        q_ref[0, h] = _low_lanes(_head_lanes(qn, h), fill).astype(BF16)
    gate_ref[0] = jax.nn.sigmoid(proj[:, hd:] + bg_ref[...])


def _q_project(h3, norm_g, w_in, b_gate, q_norm):
    b, s, d = h3.shape
    hd = N_HEADS * HEAD_DIM
    ng = N_BRANCH * N_HEADS
    ts = min(512, s)
    w = jnp.pad(w_in, ((0, 0), (0, LANES - ng))).astype(BF16)
    bg = jnp.pad(b_gate, (0, LANES - ng)).reshape(1, LANES)
    ind = np.zeros((hd, LANES), np.float32)
    ind[np.arange(hd), np.arange(hd) // HEAD_DIM] = 1.0
    full = lambda *shape: pl.BlockSpec(shape, lambda bi, si: (0,) * len(shape))
    return pl.pallas_call(
        _qproj_body,
        grid=(b, s // ts),
        in_specs=[pl.BlockSpec((1, ts, d), lambda bi, si: (bi, si, 0)), full(1, d), full(d, hd + LANES),
                  full(1, LANES), full(hd, LANES), full(LANES, hd), full(1, hd)],
        out_specs=[pl.BlockSpec((1, N_HEADS, ts, LANES), lambda bi, si: (bi, 0, si, 0)),
                   pl.BlockSpec((1, ts, LANES), lambda bi, si: (bi, si, 0))],
        out_shape=[jax.ShapeDtypeStruct((b, N_HEADS, s, LANES), BF16),
                   jax.ShapeDtypeStruct((b, s, LANES), F32)],
        compiler_params=_cparams(("arbitrary", "arbitrary")),
        name="nsa_qproj",
    )(h3, norm_g.reshape(1, d), w, bg, jnp.asarray(ind), jnp.asarray(ind.T),
      jnp.tile(q_norm, N_HEADS).reshape(1, hd))


def _t5_bucket_np(dist):
    n = np.maximum(dist, 0)
    max_exact = REL_BUCKETS // 2
    nf = np.maximum(n, 1).astype(np.float64)
    large = max_exact + (np.log(nf / max_exact) / math.log(REL_MAX_DIST / max_exact)
                         * (REL_BUCKETS - max_exact)).astype(np.int64)
    return np.where(n < max_exact, n, np.minimum(large, REL_BUCKETS - 1))


def _bias_table(rel_bias, dist, valid):
    r = N_HEADS // N_KV_GROUPS
    tab = rel_bias.astype(F32).T.reshape(N_KV_GROUPS, r, REL_BUCKETS)
    onehot = jnp.asarray(_t5_bucket_np(dist)[..., None] == np.arange(REL_BUCKETS), F32)
    bias = jnp.einsum('...qln,grn->g...rql', onehot, tab, precision=lax.Precision.HIGHEST)
    bias = jnp.where(jnp.asarray(valid)[..., None, :, :], bias, NEG_INF)
    return bias.reshape(bias.shape[:-3] + (r * dist.shape[-2], dist.shape[-1]))


def _n_delta(seq):
    d = np.arange(seq + SEL_BLOCK)
    bk = _t5_bucket_np(d)
    change = np.nonzero(bk[1:] != bk[:-1])[0]
    d_const = int(change[-1]) + 1 if change.size else 0
    return -(-(d_const + SEL_BLOCK - 1) // SEL_BLOCK) + 1


def _attn_tables(rel_bias, seq):
    qi = np.arange(Q_BLOCK)[:, None]
    nc = seq // CMP_STRIDE
    j = np.arange(nc)[None, :]
    dist_c = np.stack([qi - (CMP_BLOCK - 1) - Q_BLOCK - Q_BLOCK * e + CMP_STRIDE * (nc - j) for e in (0, 1)])
    rc = _bias_table(rel_bias, dist_c, dist_c >= 0)
    nd = _n_delta(seq)
    delta = np.arange(-1, nd + 1)[:, None, None]
    kj = np.arange(2 * SEL_BLOCK)[None, None, :]
    dist_s = SEL_BLOCK * (delta - kj // SEL_BLOCK) + qi[None] - kj % SEL_BLOCK
    bt = _bias_table(rel_bias, dist_s, dist_s >= 0)
    jw = np.arange(WINDOW + 2 * Q_BLOCK)[None, :]
    dist_w = qi - jw + WINDOW
    wb = _bias_table(rel_bias, dist_w, (dist_w >= 0) & (dist_w < WINDOW))
    n_sel = seq // SEL_BLOCK
    cs = np.arange(nc) * CMP_STRIDE
    ss = np.arange(n_sel) * SEL_BLOCK
    ov = np.clip(np.minimum(cs[:, None] + CMP_BLOCK, ss[None, :] + SEL_BLOCK)
                 - np.maximum(cs[:, None], ss[None, :]), 0, None) / CMP_BLOCK
    return rc, bt, wb, jnp.asarray(ov.astype(np.float32))


def _attn_body(q_ref, gate_ref, kc_ref, vc_ref, ks_ref, vs_ref, kw_ref, vw_ref, rc_ref, bt_ref, wb_ref, ov_ref,
               o_ref):
    g = pl.program_id(1)
    i = pl.program_id(2)
    r = q_ref.shape[1]
    rq = r * Q_BLOCK
    nc = rc_ref.shape[3]
    n_sel = ov_ref.shape[1]
    nd = bt_ref.shape[1] - 2
    wl = wb_ref.shape[2]
    per = SEL_BLOCK // CMP_STRIDE

    q_pad = q_ref[0].reshape(rq, LANES)

    e = 1 - i % 2
    end = pl.multiple_of(per * i + per + per * e, 2 * per)
    kcw = kc_ref[0, 0, pl.ds(end, nc), :].astype(BF16)
    vcw = vc_ref[0, 0, pl.ds(end, nc), :].astype(BF16)
    s = _dot_nt(q_pad, kcw) + rc_ref[0, e]
    qpos = i * Q_BLOCK + lax.broadcasted_iota(jnp.int32, (rq, 1), 0) % Q_BLOCK
    p_c = _softmax_rows(s) * (qpos >= CMP_BLOCK - 1).astype(F32)
    o_c = _dot(p_c.astype(BF16), vcw)
    p_sum = p_c[0:Q_BLOCK]
    for h in range(1, r):
        p_sum = p_sum + p_c[h * Q_BLOCK:(h + 1) * Q_BLOCK]
    imp = _dot_exact(p_sum, ov_ref[...])
    imp = pltpu.roll(imp, (i + 1 + e) % n_sel, 1)
    blk = lax.broadcasted_iota(jnp.int32, (Q_BLOCK, n_sel), 1)
    forced = (blk == 0) | (blk == i) | (blk == i - 1)
    imp = jnp.where(forced, FORCE, jnp.where(blk > i, NEG_INF, imp))
    blk_f = blk.astype(F32)
    sel = jnp.zeros((Q_BLOCK, n_sel), F32)
    for _ in range(min(SEL_TOP, n_sel)):
        _, ix = _argmax_first(imp, blk_f, n_sel)
        hit = blk_f == ix
        sel = jnp.where(hit, 1.0, sel)
        imp = jnp.where(hit, -jnp.inf, imp)

    unsel = ((sel - 1.0) * FORCE).astype(BF16)
    if n_sel < LANES:
        unsel = jnp.concatenate([unsel, jnp.zeros((Q_BLOCK, LANES - n_sel), BF16)], axis=1)
    q_aug = jnp.concatenate([q_pad, jnp.concatenate([unsel] * r, axis=0)], axis=1)
    pairs = KEY_CHUNK // (2 * SEL_BLOCK)

    def chunk(c, carry):
        m, l, acc = carry
        start = pl.multiple_of(c * KEY_CHUNK, KEY_CHUNK)
        sc = _dot_nt(q_aug, ks_ref[0, 0, pl.ds(start, KEY_CHUNK), :])
        d0 = i - c * (KEY_CHUNK // SEL_BLOCK)
        sc = sc + jnp.concatenate(
            [bt_ref[0, jnp.clip(d0 - 2 * pm, -1, nd) + 1] for pm in range(pairs)], axis=1)
        m_new = jnp.maximum(m, jnp.max(sc, axis=-1, keepdims=True))
        alpha = jnp.exp(m - m_new)
        p = jnp.exp(sc - m_new)
        l = alpha * l + jnp.sum(p, axis=-1, keepdims=True)
        acc = alpha * acc + _dot(p.astype(BF16), vs_ref[0, 0, pl.ds(start, KEY_CHUNK), :])
        return m_new, l, acc

    init = (jnp.full((rq, 1), NEG_INF, F32), jnp.zeros((rq, 1), F32), jnp.zeros((rq, LANES), F32))
    _, l_s, acc_s = lax.fori_loop(0, i // (KEY_CHUNK // SEL_BLOCK) + 1, chunk, init)
    o_s = acc_s / l_s

    ws = pl.multiple_of(i * Q_BLOCK, Q_BLOCK)
    sw = _dot_nt(q_pad, kw_ref[0, 0, pl.ds(ws, wl), :]) + wb_ref[0]
    o_w = _dot(_softmax_rows(sw).astype(BF16), vw_ref[0, 0, pl.ds(ws, wl), :])

    gates = gate_ref[0]
    glane = lax.broadcasted_iota(jnp.int32, gates.shape, 1)
    outs = []
    for h in range(r):
        rows = slice(h * Q_BLOCK, (h + 1) * Q_BLOCK)
        head = g * r + h
        gs = [jnp.sum(jnp.where(glane == br * N_HEADS + head, gates, 0.0), axis=-1, keepdims=True)
              for br in range(N_BRANCH)]
        outs.append(gs[0] * o_c[rows] + gs[1] * o_s[rows] + gs[2] * o_w[rows])
    o_ref[0] = jnp.concatenate(
        [outs[h] + pltpu.roll(outs[h + 1], HEAD_DIM, 1) for h in range(0, r, 2)], axis=1)


def _nsa_attention(q, gates, kc_pad, vc_pad, ks_aug, vs, kw_pad, vw_pad, tables):
    b, _, s, _ = q.shape
    r = N_HEADS // N_KV_GROUPS
    rc, bt, wb, ov = tables
    n_qb = s // Q_BLOCK
    per_bg = lambda a: pl.BlockSpec((1, 1) + a.shape[2:], lambda bi, gi, qi: (bi, gi, 0, 0))
    per_g = lambda a: pl.BlockSpec((1,) + a.shape[1:], lambda bi, gi, qi: (gi,) + (0,) * (a.ndim - 1))
    return pl.pallas_call(
        _attn_body,
        grid=(b, N_KV_GROUPS, n_qb),
        in_specs=[pl.BlockSpec((1, r, Q_BLOCK, LANES), lambda bi, gi, qi: (bi, gi, qi, 0)),
                  pl.BlockSpec((1, Q_BLOCK, LANES), lambda bi, gi, qi: (bi, qi, 0)),
                  per_bg(kc_pad), per_bg(vc_pad), per_bg(ks_aug), per_bg(vs), per_bg(kw_pad), per_bg(vw_pad),
                  per_g(rc), per_g(bt), per_g(wb),
                  pl.BlockSpec(ov.shape, lambda bi, gi, qi: (0, 0))],
        out_specs=pl.BlockSpec((1, Q_BLOCK, r * HEAD_DIM), lambda bi, gi, qi: (bi, qi, gi)),
        out_shape=jax.ShapeDtypeStruct((b, s, N_HEADS * HEAD_DIM), F32),
        compiler_params=_cparams(("arbitrary", "arbitrary", "arbitrary")),
        name="nsa_attention",
    )(q, gates, kc_pad, vc_pad, ks_aug, vs, kw_pad, vw_pad, rc, bt, wb, ov)


def _outproj_body(a_ref, h_ref, w_ref, o_ref):
    o_ref[...] = h_ref[...] + _dot(a_ref[...].astype(BF16), w_ref[...])


def _out_project(attn, h, w_out):
    n, d = h.shape
    hd = attn.shape[1]
    tm = min(512, n)
    return pl.pallas_call(
        _outproj_body,
        grid=(n // tm,),
        in_specs=[pl.BlockSpec((tm, hd), lambda i: (i, 0)), pl.BlockSpec((tm, d), lambda i: (i, 0)),
                  pl.BlockSpec((hd, d), lambda i: (0, 0))],
        out_specs=pl.BlockSpec((tm, d), lambda i: (i, 0)),
        out_shape=jax.ShapeDtypeStruct((n, d), F32),
        compiler_params=_cparams(("arbitrary",)),
        name="nsa_outproj",
    )(attn, h, w_out.astype(BF16))


def kernel(x, p, rel_bias, norm_mix, norm_ffn, a_w_in, a_ln_g, a_ln_b, a_w_s, a_b_s, a_w_out, kv_norm, kv_w, cmp_pos, cmp_w1, cmp_b1, cmp_w2, k_norm, b_w_in, b_b_gate, q_norm, b_w_out, router_w, router_b, e_w_gu, e_b_gu, e_w_d, e_b_d, ple_w, ple_gate_w, ple_norm):
    b, s, d = x.shape
    n = b * s
    pf = p.reshape(p.shape[0], n, p.shape[-1])

    def moe_ple(h, i):
        return _moe_ple_layer(h, norm_ffn[i], router_w[i], router_b[i], e_w_gu[i], e_b_gu[i], e_w_d[i], e_b_d[i],
                              pf[i], ple_w[i], ple_gate_w[i], ple_norm[i])

    h = _gmlp_layer(x.reshape(n, d), norm_mix[0], a_w_in[0], a_ln_g[0], a_ln_b[0], a_w_s[0], a_b_s[0], a_w_out[0])
    h = moe_ple(h, 0)

    h3 = h.reshape(b, s, d)
    kc, vc, ks_aug, vs, kw, vw = _kv_project(h3, kv_norm, kv_w, k_norm)
    kc_pad, vc_pad = _compress(kc, vc, cmp_pos, cmp_w1, cmp_b1, cmp_w2, k_norm[0])
    front = jnp.zeros((b, N_KV_GROUPS, WINDOW, LANES), BF16).at[..., HEAD_DIM].set(1.0)
    kw_pad = jnp.concatenate([front, kw, front[:, :, :2 * Q_BLOCK]], axis=2)
    vw_pad = jnp.pad(vw, ((0, 0), (0, 0), (WINDOW, 2 * Q_BLOCK), (0, 0)))

    q, gates = _q_project(h3, norm_mix[1], b_w_in[0], b_b_gate[0], q_norm[0])
    attn = _nsa_attention(q, gates, kc_pad, vc_pad, ks_aug, vs, kw_pad, vw_pad, _attn_tables(rel_bias, s))
    h = _out_project(attn.reshape(n, -1), h, b_w_out[0])
    h = moe_ple(h, 1)
    return h.reshape(b, s, d)
```

```python
import functools
import math

import numpy as np
import jax
import jax.numpy as jnp
from jax import lax
from jax.experimental import pallas as pl
from jax.experimental.pallas import tpu as pltpu

D_MODEL = 1024
GMLP_CHUNK = 128
GMLP_GROUPS = 8
N_HEADS = 16
HEAD_DIM = 64
N_KV_GROUPS = 4
N_BRANCH = 3
CMP_BLOCK = 32
CMP_STRIDE = 16
CMP_HIDDEN = 256
SEL_BLOCK = 64
SEL_TOP = 16
WINDOW = 512
Q_BLOCK = 64
N_KV_SLOTS = 6
REL_BUCKETS = 32
REL_MAX_DIST = 2048
N_EXPERTS = 32
TOP_K = 4
SWIGLU_LIMIT = 7.0
SWIGLU_ALPHA = 1.702
NORM_EPS = 1e-6
NEG_INF = -1e30
FORCE = 1e30
LOG2E = math.log2(math.e)

LANES = 128
MOE_ROWS = 512
KEY_CHUNK = 1024
VMEM_LIMIT = 56 * 1024 * 1024

F32 = jnp.float32
BF16 = jnp.bfloat16


def _cparams(sem):
    return pltpu.CompilerParams(dimension_semantics=sem, vmem_limit_bytes=VMEM_LIMIT)


def _rms(x, g):
    return x * lax.rsqrt(jnp.mean(x * x, axis=-1, keepdims=True) + NORM_EPS) * g


def _dot(a, b):
    return jnp.dot(a, b, preferred_element_type=F32)


def _dot_nt(a, b):
    return lax.dot_general(a, b, (((1,), (1,)), ((), ())), preferred_element_type=F32)


def _dot_exact(a, b):
    return jnp.dot(a, b, preferred_element_type=F32, precision=lax.Precision.HIGHEST)


def _softmax2_rows(s):
    m = jnp.max(s, axis=-1, keepdims=True)
    p = jnp.exp2(s - m)
    return p / jnp.sum(p, axis=-1, keepdims=True)


def _head_lanes(x, idx):
    base = idx * HEAD_DIM // LANES * LANES
    y = x[:, base:base + LANES]
    return y if idx * HEAD_DIM == base else pltpu.roll(y, LANES - HEAD_DIM, 1)


def _low_lanes(x, fill):
    lane = lax.broadcasted_iota(jnp.int32, x.shape, 1)
    return jnp.where(lane < HEAD_DIM, x, fill)


def _argmax_first(x, ids, n):
    mx = jnp.max(x, axis=-1, keepdims=True)
    return mx, jnp.min(jnp.where(x == mx, ids, float(n)), axis=-1, keepdims=True)


def _gmlp_body(x_ref, nm_ref, win_ref, lng_ref, lnb_ref, ws_ref, bs_ref, wout_ref, o_ref):
    tm = x_ref.shape[0]
    gd = win_ref.shape[1] // 2
    gdim = gd // GMLP_GROUPS
    x = x_ref[...]
    xn = _rms(x, nm_ref[...]).astype(BF16)
    z = jax.nn.gelu(_dot(xn, win_ref[...]))
    u = z[:, :gd]
    v = z[:, gd:]
    mu = jnp.mean(v, axis=-1, keepdims=True)
    vc = v - mu
    vln = vc * lax.rsqrt(jnp.mean(vc * vc, axis=-1, keepdims=True) + NORM_EPS) * lng_ref[...] + lnb_ref[...]
    vb = vln.astype(BF16)
    row = lax.broadcasted_iota(jnp.int32, (GMLP_CHUNK, GMLP_CHUNK), 0)
    col = lax.broadcasted_iota(jnp.int32, (GMLP_CHUNK, GMLP_CHUNK), 1)
    causal = row >= col
    chunks = []
    for c in range(tm // GMLP_CHUNK):
        cols = []
        for g in range(GMLP_GROUPS):
            wsg = jnp.where(causal, ws_ref[g], 0.0).astype(BF16)
            vg = vb[c * GMLP_CHUNK:(c + 1) * GMLP_CHUNK, g * gdim:(g + 1) * gdim]
            cols.append(_dot(wsg, vg) + bs_ref[g])
        chunks.append(jnp.concatenate(cols, axis=1))
    mixed = jnp.concatenate(chunks, axis=0)
    gated = (u * mixed).astype(BF16)
    o_ref[...] = x + _dot(gated, wout_ref[...])


def _gmlp_layer(h, norm_g, w_in, ln_g, ln_b, w_s, b_s, w_out):
    n, d = h.shape
    gd2 = w_in.shape[1]
    gd = gd2 // 2
    tm = min(512, n)
    full = lambda *shape: pl.BlockSpec(shape, lambda i: (0,) * len(shape))
    return pl.pallas_call(
        _gmlp_body,
        grid=(n // tm,),
        in_specs=[pl.BlockSpec((tm, d), lambda i: (i, 0)),
                  full(1, d), full(d, gd2), full(1, gd), full(1, gd),
                  full(GMLP_GROUPS, GMLP_CHUNK, GMLP_CHUNK), full(GMLP_GROUPS, GMLP_CHUNK, 1),
                  full(gd, d)],
        out_specs=pl.BlockSpec((tm, d), lambda i: (i, 0)),
        out_shape=jax.ShapeDtypeStruct((n, d), F32),
        compiler_params=_cparams(("arbitrary",)),
        name="gmlp_layer",
    )(h, norm_g.reshape(1, d), w_in.astype(BF16), ln_g.reshape(1, gd), ln_b.reshape(1, gd),
      w_s, b_s.reshape(GMLP_GROUPS, GMLP_CHUNK, 1), w_out.astype(BF16))


def _route_body(h_ref, ng_ref, rw_ref, rb_ref, o_ref, cnt_ref, run_ref):
    i = pl.program_id(0)
    tm = h_ref.shape[0]

    @pl.when(i == 0)
    def _():
        run_ref[...] = jnp.zeros_like(run_ref)

    xn = _rms(h_ref[...], ng_ref[...])
    logits = _dot_exact(xn, rw_ref[...]) + rb_ref[...]
    eid = lax.broadcasted_iota(jnp.int32, logits.shape, 1).astype(F32)
    lane = lax.broadcasted_iota(jnp.int32, (tm, LANES), 1)
    work = logits
    vals, idxs = [], []
    for _ in range(TOP_K):
        mx, ix = _argmax_first(work, eid, N_EXPERTS)
        vals.append(mx)
        idxs.append(ix)
        work = jnp.where(eid == ix, -jnp.inf, work)
    exps = [jnp.exp(v - vals[0]) for v in vals]
    den = exps[0]
    for e in exps[1:]:
        den = den + e
    onehot = jnp.zeros(logits.shape, F32)
    for ix in idxs:
        onehot = onehot + (eid == ix).astype(F32)
    r = lax.broadcasted_iota(jnp.int32, (tm, tm), 0)
    c = lax.broadcasted_iota(jnp.int32, (tm, tm), 1)
    before = (r > c).astype(BF16)
    prefix = _dot(before, onehot.astype(BF16)) + run_ref[...]
    out = jnp.zeros((tm, LANES), F32)
    for k in range(TOP_K):
        rank = jnp.sum(jnp.where(eid == idxs[k], prefix, 0.0), axis=-1, keepdims=True)
        out = jnp.where(lane == k, idxs[k], out)
        out = jnp.where(lane == TOP_K + k, exps[k] / den, out)
        out = jnp.where(lane == 2 * TOP_K + k, rank, out)
    o_ref[...] = out
    run_ref[...] = run_ref[...] + jnp.sum(onehot, axis=0, keepdims=True)
    cnt_ref[...] = run_ref[...]


def _moe_route(h, norm_g, router_w, router_b):
    n, d = h.shape
    tm = min(512, n)
    full = lambda *shape: pl.BlockSpec(shape, lambda i: (0,) * len(shape))
    return pl.pallas_call(
        _route_body,
        grid=(n // tm,),
        in_specs=[pl.BlockSpec((tm, d), lambda i: (i, 0)), full(1, d), full(d, N_EXPERTS), full(1, N_EXPERTS)],
        out_specs=[pl.BlockSpec((tm, LANES), lambda i: (i, 0)), full(1, N_EXPERTS)],
        out_shape=[jax.ShapeDtypeStruct((n, LANES), F32), jax.ShapeDtypeStruct((1, N_EXPERTS), F32)],
        scratch_shapes=[pltpu.VMEM((1, N_EXPERTS), F32)],
        compiler_params=_cparams(("arbitrary",)),
        name="moe_route",
    )(h, norm_g.reshape(1, d), router_w, router_b.reshape(1, N_EXPERTS))


def _dispatch_body(dest_ref, h_ref, ng_ref, xs_in, xs_out, buf, sem):
    del xs_in
    tm = h_ref.shape[0]
    buf[...] = _rms(h_ref[...], ng_ref[...])

    def issue(j, carry):
        for k in range(TOP_K):
            r = dest_ref[0, 0, j * TOP_K + k]
            pltpu.make_async_copy(buf.at[pl.ds(j, 1), :], xs_out.at[pl.ds(r, 1), :], sem).start()
        return carry

    lax.fori_loop(0, tm, issue, 0)
    for _ in range(TOP_K):
        pltpu.make_async_copy(buf, xs_out.at[pl.ds(0, tm), :], sem).wait()


def _moe_dispatch(h, norm_g, dest, n_rows):
    n, d = h.shape
    tm = min(256, n)
    xs0 = jnp.zeros((n_rows, d), F32)
    return pl.pallas_call(
        _dispatch_body,
        grid=(n // tm,),
        in_specs=[pl.BlockSpec((1, 1, tm * TOP_K), lambda i: (i, 0, 0), memory_space=pltpu.SMEM),
                  pl.BlockSpec((tm, d), lambda i: (i, 0)),
                  pl.BlockSpec((1, d), lambda i: (0, 0)),
                  pl.BlockSpec(memory_space=pl.ANY)],
        out_specs=pl.BlockSpec(memory_space=pl.ANY),
        out_shape=jax.ShapeDtypeStruct((n_rows, d), F32),
        scratch_shapes=[pltpu.VMEM((tm, d), F32), pltpu.SemaphoreType.DMA(())],
        input_output_aliases={3: 0},
        compiler_params=_cparams(("arbitrary",)),
        name="moe_dispatch",
    )(dest.reshape(n // tm, 1, tm * TOP_K), h, norm_g.reshape(1, d), xs0)


def _expert_body(be_ref, nu_ref, xs_ref, wgu_ref, bgu_ref, wd_ref, bd_ref, ys_ref):
    i = pl.program_id(0)
    ed = wd_ref.shape[1]

    @pl.when(i < nu_ref[0])
    def _():
        x = xs_ref[...].astype(BF16)
        gu = _dot(x, wgu_ref[0]) + bgu_ref[0]
        gate = jnp.minimum(gu[:, :ed], SWIGLU_LIMIT)
        up = jnp.clip(gu[:, ed:], -SWIGLU_LIMIT, SWIGLU_LIMIT)
        glu = gate * jax.nn.sigmoid(gate * SWIGLU_ALPHA)
        ys_ref[...] = _dot(((up + 1.0) * glu).astype(BF16), wd_ref[0]) + bd_ref[0]

    @pl.when(i >= nu_ref[0])
    def _():
        ys_ref[...] = jnp.zeros_like(ys_ref)


def _moe_experts(xs, blk_expert, n_used, w_gu, b_gu, w_d, b_d):
    n_rows, d = xs.shape
    ed = w_d.shape[1]
    n_blk = n_rows // MOE_ROWS
    return pl.pallas_call(
        _expert_body,
        grid_spec=pltpu.PrefetchScalarGridSpec(
            num_scalar_prefetch=2, grid=(n_blk,),
            in_specs=[pl.BlockSpec((MOE_ROWS, d), lambda i, be, nu: (jnp.minimum(i, nu[0] - 1), 0)),
                      pl.BlockSpec((1, d, 2 * ed), lambda i, be, nu: (be[i], 0, 0)),
                      pl.BlockSpec((1, 1, 2 * ed), lambda i, be, nu: (be[i], 0, 0)),
                      pl.BlockSpec((1, ed, d), lambda i, be, nu: (be[i], 0, 0)),
                      pl.BlockSpec((1, 1, d), lambda i, be, nu: (be[i], 0, 0))],
            out_specs=pl.BlockSpec((MOE_ROWS, d), lambda i, be, nu: (i, 0))),
        out_shape=jax.ShapeDtypeStruct((n_rows, d), F32),
        compiler_params=_cparams(("arbitrary",)),
        name="moe_experts",
    )(blk_expert, n_used, xs, w_gu.astype(BF16), b_gu.reshape(N_EXPERTS, 1, 2 * ed),
      w_d.astype(BF16), b_d.reshape(N_EXPERTS, 1, d))


def _combine_body(dest_ref, rt_ref, h_ref, p_ref, pw_ref, pg_ref, pn_ref, ys_hbm, o_ref, buf, sem):
    tm = h_ref.shape[0]

    def issue(j, carry):
        for k in range(TOP_K):
            r = dest_ref[0, 0, j * TOP_K + k]
            pltpu.make_async_copy(ys_hbm.at[pl.ds(r, 1), :], buf.at[k, pl.ds(j, 1), :], sem).start()
        return carry

    lax.fori_loop(0, tm, issue, 0)
    for k in range(TOP_K):
        pltpu.make_async_copy(ys_hbm.at[pl.ds(0, tm), :], buf.at[k], sem).wait()
    rt = rt_ref[...]
    h = h_ref[...]
    for k in range(TOP_K):
        h = h + rt[:, TOP_K + k:TOP_K + k + 1] * buf[k]
    emb = _dot(p_ref[...].astype(BF16), pw_ref[...])
    gate = jax.nn.sigmoid(_dot(_rms(h, pn_ref[...]).astype(BF16), pg_ref[...]))
    o_ref[...] = h + emb * gate


def _moe_combine_ple(h, route, dest, ys, p, ple_w, ple_gate_w, ple_norm):
    n, d = h.shape
    pd = p.shape[1]
    tm = min(256, n)
    full = lambda *shape: pl.BlockSpec(shape, lambda i: (0,) * len(shape))
    return pl.pallas_call(
        _combine_body,
        grid=(n // tm,),
        in_specs=[pl.BlockSpec((1, 1, tm * TOP_K), lambda i: (i, 0, 0), memory_space=pltpu.SMEM),
                  pl.BlockSpec((tm, LANES), lambda i: (i, 0)),
                  pl.BlockSpec((tm, d), lambda i: (i, 0)),
                  pl.BlockSpec((tm, pd), lambda i: (i, 0)),
                  full(pd, d), full(d, d), full(1, d),
                  pl.BlockSpec(memory_space=pl.ANY)],
        out_specs=pl.BlockSpec((tm, d), lambda i: (i, 0)),
        out_shape=jax.ShapeDtypeStruct((n, d), F32),
        scratch_shapes=[pltpu.VMEM((TOP_K, tm, d), F32), pltpu.SemaphoreType.DMA(())],
        compiler_params=_cparams(("arbitrary",)),
        name="moe_combine_ple",
    )(dest.reshape(n // tm, 1, tm * TOP_K), route, h, p, ple_w.astype(BF16), ple_gate_w.astype(BF16),
      ple_norm.reshape(1, d), ys)


def _moe_ple_layer(h, norm_g, router_w, router_b, w_gu, b_gu, w_d, b_d, p, ple_w, ple_gate_w, ple_norm):
    n, _ = h.shape
    route, counts = _moe_route(h, norm_g, router_w, router_b)
    counts = counts[0].astype(jnp.int32)
    pad_counts = (counts + MOE_ROWS - 1) // MOE_ROWS * MOE_ROWS
    pad_ends = jnp.cumsum(pad_counts)
    pad_starts = pad_ends - pad_counts
    top_idx = route[:, :TOP_K].astype(jnp.int32)
    rank = route[:, 2 * TOP_K:3 * TOP_K].astype(jnp.int32)
    dest = (pad_starts[top_idx] + rank).reshape(-1)
    n_blk = -(-(n * TOP_K) // MOE_ROWS) + N_EXPERTS
    blk_start = jnp.arange(n_blk, dtype=jnp.int32) * MOE_ROWS
    blk_expert = jnp.minimum(jnp.sum((pad_ends[None, :] <= blk_start[:, None]).astype(jnp.int32), axis=1),
                             N_EXPERTS - 1)
    n_used = (pad_ends[-1:] // MOE_ROWS).astype(jnp.int32)
    xs = _moe_dispatch(h, norm_g, dest, n_blk * MOE_ROWS)
    ys = _moe_experts(xs, blk_expert, n_used, w_gu, b_gu, w_d, b_d)
    return _moe_combine_ple(h, route, dest, ys, p, ple_w, ple_gate_w, ple_norm)


def _kv_body(h_ref, ng_ref, w_ref, seg_ref, kn_ref, kc_ref, vc_ref, ks_ref, vs_ref, kw_ref, vw_ref):
    ts = h_ref.shape[1]
    gw = N_KV_GROUPS * HEAD_DIM
    st = pl.program_id(1)
    hn = _rms(h_ref[0], ng_ref[...]).astype(BF16)
    kv = _dot(hn, w_ref[...])

    def knorm(x, j):
        ms = _dot_exact(x * x, seg_ref[...])
        return x * lax.rsqrt(ms + NORM_EPS) * kn_ref[j]

    k_c, v_c = kv[:, 0:gw], kv[:, gw:2 * gw]
    k_s, v_s = knorm(kv[:, 2 * gw:3 * gw], 1), kv[:, 3 * gw:4 * gw]
    k_w, v_w = knorm(kv[:, 4 * gw:5 * gw], 2), kv[:, 5 * gw:6 * gw]
    tok = st * ts + lax.broadcasted_iota(jnp.int32, (ts, LANES), 0)
    blk = lax.broadcasted_iota(jnp.int32, (ts, LANES), 1)
    onehot = (tok // SEL_BLOCK == blk).astype(BF16)
    for g in range(N_KV_GROUPS):
        kc_ref[0, g] = _head_lanes(k_c, g)[:, :HEAD_DIM]
        vc_ref[0, g] = _head_lanes(v_c, g)[:, :HEAD_DIM]
        ks_ref[0, g] = jnp.concatenate([_low_lanes(_head_lanes(k_s, g), 0.0).astype(BF16), onehot], axis=1)
        vs_ref[0, g] = _low_lanes(_head_lanes(v_s, g), 0.0).astype(BF16)
        kw_ref[0, g] = _low_lanes(_head_lanes(k_w, g), 0.0).astype(BF16)
        vw_ref[0, g] = _low_lanes(_head_lanes(v_w, g), 0.0).astype(BF16)


def _kv_project(h3, kv_norm, kv_w, k_norm):
    b, s, d = h3.shape
    gw = N_KV_GROUPS * HEAD_DIM
    ts = min(512, s)
    seg = jnp.asarray(np.kron(np.eye(N_KV_GROUPS), np.full((HEAD_DIM, HEAD_DIM), 1.0 / HEAD_DIM)), F32)
    kn = jnp.tile(k_norm, (1, N_KV_GROUPS)).reshape(N_BRANCH, 1, gw)
    full = lambda *shape: pl.BlockSpec(shape, lambda bi, si: (0,) * len(shape))
    hd = lambda w: pl.BlockSpec((1, N_KV_GROUPS, ts, w), lambda bi, si: (bi, 0, si, 0))
    sds = lambda w, dt: jax.ShapeDtypeStruct((b, N_KV_GROUPS, s, w), dt)
    return pl.pallas_call(
        _kv_body,
        grid=(b, s // ts),
        in_specs=[pl.BlockSpec((1, ts, d), lambda bi, si: (bi, si, 0)), full(1, d), full(d, N_KV_SLOTS * gw),
                  full(gw, gw), full(N_BRANCH, 1, gw)],
        out_specs=[hd(HEAD_DIM), hd(HEAD_DIM), hd(2 * LANES), hd(LANES), hd(LANES), hd(LANES)],
        out_shape=[sds(HEAD_DIM, F32), sds(HEAD_DIM, F32), sds(2 * LANES, BF16), sds(LANES, BF16),
                   sds(LANES, BF16), sds(LANES, BF16)],
        compiler_params=_cparams(("arbitrary", "arbitrary")),
        name="kv_project",
    )(h3, kv_norm.reshape(1, d), kv_w.astype(BF16), seg, kn)


def _compress_body(kc_ref, vc_ref, pos_ref, w1_ref, b1_ref, w2_ref, kn_ref, ko_ref, vo_ref):
    nc = kc_ref.shape[2]
    half = kc_ref.shape[3]
    row = lax.broadcasted_iota(jnp.int32, (nc, 1), 0)
    valid = row < nc - 1

    def compress(c, j):
        a = _dot((c + pos_ref[j, 0]).astype(BF16), w1_ref[j, 0])
        bm = _dot((c + pos_ref[j, 1]).astype(BF16), w1_ref[j, 1])
        nxt = pltpu.roll(bm, nc - 1, 0)
        hid = jax.nn.gelu(a + nxt + b1_ref[j])
        return _dot(hid.astype(BF16), w2_ref[j])

    kraw = compress(kc_ref[0, 0], 0)
    ms = jnp.sum(kraw * kraw, axis=-1, keepdims=True) * (1.0 / HEAD_DIM)
    kcmp = kraw * lax.rsqrt(ms + NORM_EPS) * kn_ref[...]
    vcmp = jnp.where(valid, compress(vc_ref[0, 0], 1), 0.0)
    lane = lax.broadcasted_iota(jnp.int32, (nc, LANES), 1)
    flag = (lane == HEAD_DIM).astype(F32)
    ko_ref[0, 0, 0:nc, :] = flag
    ko_ref[0, 0, nc:2 * nc, :] = jnp.where(valid, kcmp, flag)
    vo_ref[0, 0, 0:nc, :] = jnp.zeros((nc, LANES), F32)
    vo_ref[0, 0, nc:2 * nc, :] = vcmp


def _compress(kc, vc, cmp_pos, cmp_w1, cmp_b1, cmp_w2, k_norm0):
    b, g, s, dh = kc.shape
    nc = s // CMP_STRIDE
    half = CMP_STRIDE * dh
    kc_r = kc.reshape(b, g, nc, half)
    vc_r = vc.reshape(b, g, nc, half)
    pos = cmp_pos.reshape(2, 2, 1, half)
    w1 = cmp_w1.reshape(2, 2, half, CMP_HIDDEN).astype(BF16)
    w2 = jnp.pad(cmp_w2, ((0, 0), (0, 0), (0, LANES - dh))).astype(BF16)
    kn = jnp.pad(k_norm0, (0, LANES - dh)).reshape(1, LANES)
    full = lambda *shape: pl.BlockSpec(shape, lambda bi, gi: (0,) * len(shape))
    blk = pl.BlockSpec((1, 1, nc, half), lambda bi, gi: (bi, gi, 0, 0))
    out = pl.BlockSpec((1, 1, 2 * nc, LANES), lambda bi, gi: (bi, gi, 0, 0))
    return pl.pallas_call(
        _compress_body,
        grid=(b, g),
        in_specs=[blk, blk, full(2, 2, 1, half), full(2, 2, half, CMP_HIDDEN), full(2, 1, CMP_HIDDEN),
                  full(2, CMP_HIDDEN, LANES), full(1, LANES)],
        out_specs=[out, out],
        out_shape=[jax.ShapeDtypeStruct((b, g, 2 * nc, LANES), F32)] * 2,
        compiler_params=_cparams(("arbitrary", "arbitrary")),
        name="kv_compress",
    )(kc_r, vc_r, pos, w1, cmp_b1.reshape(2, 1, CMP_HIDDEN), w2, kn)


def _qproj_body(h_ref, ng_ref, w_ref, bg_ref, ind_ref, indt_ref, qn_ref, q_ref, gate_ref):
    hd = N_HEADS * HEAD_DIM
    xn = _rms(h_ref[0], ng_ref[...]).astype(BF16)
    proj = _dot(xn, w_ref[...])
    q = proj[:, :hd]
    ms = _dot_exact(q * q, ind_ref[...]) * (1.0 / HEAD_DIM)
    scale = _dot_exact(lax.rsqrt(ms + NORM_EPS), indt_ref[...])
    qn = q * scale * qn_ref[...] * (HEAD_DIM ** -0.5 * LOG2E)
    lane = lax.broadcasted_iota(jnp.int32, (q.shape[0], LANES), 1)
    fill = jnp.where(lane == HEAD_DIM, NEG_INF, 0.0)
    for h in range(N_HEADS):
        q_ref[0, h] = _low_lanes(_head_lanes(qn, h), fill).astype(BF16)
    gate_ref[0] = jax.nn.sigmoid(proj[:, hd:] + bg_ref[...])


def _q_project(h3, norm_g, w_in, b_gate, q_norm):
    b, s, d = h3.shape
    hd = N_HEADS * HEAD_DIM
    ng = N_BRANCH * N_HEADS
    ts = min(512, s)
    w = jnp.pad(w_in, ((0, 0), (0, LANES - ng))).astype(BF16)
    bg = jnp.pad(b_gate, (0, LANES - ng)).reshape(1, LANES)
    ind = np.zeros((hd, LANES), np.float32)
    ind[np.arange(hd), np.arange(hd) // HEAD_DIM] = 1.0
    full = lambda *shape: pl.BlockSpec(shape, lambda bi, si: (0,) * len(shape))
    return pl.pallas_call(
        _qproj_body,
        grid=(b, s // ts),
        in_specs=[pl.BlockSpec((1, ts, d), lambda bi, si: (bi, si, 0)), full(1, d), full(d, hd + LANES),
                  full(1, LANES), full(hd, LANES), full(LANES, hd), full(1, hd)],
        out_specs=[pl.BlockSpec((1, N_HEADS, ts, LANES), lambda bi, si: (bi, 0, si, 0)),
                   pl.BlockSpec((1, ts, LANES), lambda bi, si: (bi, si, 0))],
        out_shape=[jax.ShapeDtypeStruct((b, N_HEADS, s, LANES), BF16),
                   jax.ShapeDtypeStruct((b, s, LANES), F32)],
        compiler_params=_cparams(("arbitrary", "arbitrary")),
        name="nsa_qproj",
    )(h3, norm_g.reshape(1, d), w, bg, jnp.asarray(ind), jnp.asarray(ind.T),
      jnp.tile(q_norm, N_HEADS).reshape(1, hd))


def _t5_bucket_np(dist):
    n = np.maximum(dist, 0)
    max_exact = REL_BUCKETS // 2
    nf = np.maximum(n, 1).astype(np.float64)
    large = max_exact + (np.log(nf / max_exact) / math.log(REL_MAX_DIST / max_exact)
                         * (REL_BUCKETS - max_exact)).astype(np.int64)
    return np.where(n < max_exact, n, np.minimum(large, REL_BUCKETS - 1))


def _bias_table(rel_bias, dist, valid):
    r = N_HEADS // N_KV_GROUPS
    tab = rel_bias.astype(F32).T.reshape(N_KV_GROUPS, r, REL_BUCKETS)
    onehot = jnp.asarray(_t5_bucket_np(dist)[..., None] == np.arange(REL_BUCKETS), F32)
    bias = jnp.einsum('...qln,grn->g...rql', onehot, tab, precision=lax.Precision.HIGHEST) * LOG2E
    bias = jnp.where(jnp.asarray(valid)[..., None, :, :], bias, NEG_INF)
    return bias.reshape(bias.shape[:-3] + (r * dist.shape[-2], dist.shape[-1]))


def _n_delta(seq):
    d = np.arange(seq + SEL_BLOCK)
    bk = _t5_bucket_np(d)
    change = np.nonzero(bk[1:] != bk[:-1])[0]
    d_const = int(change[-1]) + 1 if change.size else 0
    return -(-(d_const + SEL_BLOCK - 1) // SEL_BLOCK) + 1


def _attn_tables(rel_bias, seq):
    qi = np.arange(Q_BLOCK)[:, None]
    nc = seq // CMP_STRIDE
    j = np.arange(nc)[None, :]
    dist_c = np.stack([qi - (CMP_BLOCK - 1) - Q_BLOCK - Q_BLOCK * e + CMP_STRIDE * (nc - j) for e in (0, 1)])
    rc = _bias_table(rel_bias, dist_c, dist_c >= 0)
    nd = _n_delta(seq)
    delta = np.arange(-1, nd + 1)[:, None, None]
    kj = np.arange(2 * SEL_BLOCK)[None, None, :]
    dist_s = SEL_BLOCK * (delta - kj // SEL_BLOCK) + qi[None] - kj % SEL_BLOCK
    bt = _bias_table(rel_bias, dist_s, dist_s >= 0)
    jw = np.arange(WINDOW + 2 * Q_BLOCK)[None, :]
    dist_w = qi - jw + WINDOW
    wb = _bias_table(rel_bias, dist_w, (dist_w >= 0) & (dist_w < WINDOW))
    n_sel = seq // SEL_BLOCK
    cs = np.arange(nc) * CMP_STRIDE
    ss = np.arange(n_sel) * SEL_BLOCK
    ov = np.clip(np.minimum(cs[:, None] + CMP_BLOCK, ss[None, :] + SEL_BLOCK)
                 - np.maximum(cs[:, None], ss[None, :]), 0, None) / CMP_BLOCK
    return rc, bt, wb, jnp.asarray(ov.T.astype(np.float32)).astype(BF16)


def _attn_body(q_ref, gate_ref, kc_ref, vc_ref, ks_ref, vs_ref, kw_ref, vw_ref, rc_ref, bt_ref, wb_ref, ovt_ref,
               o_ref):
    g = pl.program_id(1)
    i = pl.program_id(2)
    r = q_ref.shape[1]
    rq = r * Q_BLOCK
    nc = rc_ref.shape[3]
    n_sel = ovt_ref.shape[0]
    nd = bt_ref.shape[1] - 2
    wl = wb_ref.shape[2]
    per = SEL_BLOCK // CMP_STRIDE

    heads = [slice(h * Q_BLOCK, (h + 1) * Q_BLOCK) for h in range(r)]
    q_pad = q_ref[0].reshape(rq, LANES)

    ws = pl.multiple_of(i * Q_BLOCK, Q_BLOCK)
    sw = _dot_nt(q_pad, kw_ref[0, 0, pl.ds(ws, wl), :]) + wb_ref[0]
    o_w = _dot(_softmax2_rows(sw).astype(BF16), vw_ref[0, 0, pl.ds(ws, wl), :])

    e = 1 - i % 2
    end = pl.multiple_of(per * i + per + per * e, 2 * per)
    kcw = kc_ref[0, 0, pl.ds(end, nc), :].astype(BF16)
    vcw = vc_ref[0, 0, pl.ds(end, nc), :].astype(BF16)
    qpos = i * Q_BLOCK + lax.broadcasted_iota(jnp.int32, (rq, 1), 0) % Q_BLOCK
    p_c = _softmax2_rows(_dot_nt(q_pad, kcw) + rc_ref[0, e]) * (qpos >= CMP_BLOCK - 1).astype(F32)
    o_c = _dot(p_c.astype(BF16), vcw)
    p_sum = p_c[heads[0]]
    for h in range(1, r):
        p_sum = p_sum + p_c[heads[h]]

    p_hi = p_sum.astype(BF16)
    p_lo = (p_sum - p_hi.astype(F32)).astype(BF16)
    imp2 = _dot_nt(ovt_ref[...], jnp.concatenate([p_hi, p_lo], axis=0))
    imp = (imp2 + pltpu.roll(imp2, Q_BLOCK, 1))[:, :Q_BLOCK]
    shift = i + 1 + e
    blk_rel = lax.broadcasted_iota(jnp.int32, (n_sel, Q_BLOCK), 0)
    blk = blk_rel + shift - n_sel
    forced = (blk == 0) | (blk == i) | (blk == i - 1)
    imp = jnp.where(forced, FORCE, jnp.where(blk > i, NEG_INF, imp))
    imp = jnp.where(blk < 0, -jnp.inf, imp)
    ids = blk_rel.astype(F32)
    sel = jnp.zeros((n_sel, Q_BLOCK), F32)
    for _ in range(min(SEL_TOP, n_sel)):
        mx = jnp.max(imp, axis=0, keepdims=True)
        ix = jnp.min(jnp.where(imp == mx, ids, float(n_sel)), axis=0, keepdims=True)
        hit = ids == ix
        sel = jnp.where(hit, 1.0, sel)
        imp = jnp.where(hit, -jnp.inf, imp)
    unsel = jnp.where((sel > 0.0) & (blk >= 0), 0.0, NEG_INF)
    unsel = pltpu.roll(unsel.T, shift % n_sel, 1).astype(BF16)
    if n_sel < LANES:
        unsel = jnp.concatenate([unsel, jnp.zeros((Q_BLOCK, LANES - n_sel), BF16)], axis=1)

    q_aug = jnp.concatenate([q_pad, jnp.concatenate([unsel] * r, axis=0)], axis=1)
    kchunk = min(KEY_CHUNK, ks_ref.shape[2])
    pairs = kchunk // (2 * SEL_BLOCK)

    def scores(c):
        start = pl.multiple_of(c * kchunk, kchunk)
        sc = _dot_nt(q_aug, ks_ref[0, 0, pl.ds(start, kchunk), :])
        d0 = i - c * (kchunk // SEL_BLOCK)
        return sc + jnp.concatenate(
            [bt_ref[0, jnp.clip(d0 - 2 * pm, -1, nd) + 1] for pm in range(pairs)], axis=1)

    def absorb(c, sc, m, l, acc):
        start = pl.multiple_of(c * kchunk, kchunk)
        m_new = jnp.maximum(m, jnp.max(sc, axis=-1, keepdims=True))
        alpha = jnp.exp2(m - m_new)
        p = jnp.exp2(sc - m_new)
        l = alpha * l + jnp.sum(p, axis=-1, keepdims=True)
        acc = alpha * acc + _dot(p.astype(BF16), vs_ref[0, 0, pl.ds(start, kchunk), :])
        return m_new, l, acc

    def chunk(c, carry):
        return absorb(c, scores(c), *carry)

    init = (jnp.full((rq, 1), NEG_INF, F32), jnp.zeros((rq, 1), F32), jnp.zeros((rq, LANES), F32))
    _, l_s, acc_s = lax.fori_loop(0, i // (kchunk // SEL_BLOCK) + 1, chunk, init)
    o_s = acc_s / l_s

    gates = gate_ref[0]
    glane = lax.broadcasted_iota(jnp.int32, gates.shape, 1)
    outs = []
    for h in range(r):
        head = g * r + h
        gs = [jnp.sum(jnp.where(glane == br * N_HEADS + head, gates, 0.0), axis=-1, keepdims=True)
              for br in range(N_BRANCH)]
        outs.append(gs[0] * o_c[heads[h]] + gs[1] * o_s[heads[h]] + gs[2] * o_w[heads[h]])
    o_ref[0] = jnp.concatenate(
        [outs[h] + pltpu.roll(outs[h + 1], HEAD_DIM, 1) for h in range(0, r, 2)], axis=1)


def _nsa_attention(q, gates, kc_pad, vc_pad, ks_aug, vs, kw_pad, vw_pad, tables):
    b, _, s, _ = q.shape
    r = N_HEADS // N_KV_GROUPS
    rc, bt, wb, ov = tables
    n_qb = s // Q_BLOCK
    per_bg = lambda a: pl.BlockSpec((1, 1) + a.shape[2:], lambda bi, gi, qi: (bi, gi, 0, 0))
    per_g = lambda a: pl.BlockSpec((1,) + a.shape[1:], lambda bi, gi, qi: (gi,) + (0,) * (a.ndim - 1))
    return pl.pallas_call(
        _attn_body,
        grid=(b, N_KV_GROUPS, n_qb),
        in_specs=[pl.BlockSpec((1, r, Q_BLOCK, LANES), lambda bi, gi, qi: (bi, gi, qi, 0)),
                  pl.BlockSpec((1, Q_BLOCK, LANES), lambda bi, gi, qi: (bi, qi, 0)),
                  per_bg(kc_pad), per_bg(vc_pad), per_bg(ks_aug), per_bg(vs), per_bg(kw_pad), per_bg(vw_pad),
                  per_g(rc), per_g(bt), per_g(wb),
                  pl.BlockSpec(ov.shape, lambda bi, gi, qi: (0, 0))],
        out_specs=pl.BlockSpec((1, Q_BLOCK, r * HEAD_DIM), lambda bi, gi, qi: (bi, qi, gi)),
        out_shape=jax.ShapeDtypeStruct((b, s, N_HEADS * HEAD_DIM), F32),
        compiler_params=_cparams(("arbitrary", "arbitrary", "arbitrary")),
        name="nsa_attention",
    )(q, gates, kc_pad, vc_pad, ks_aug, vs, kw_pad, vw_pad, rc, bt, wb, ov)


def _outproj_body(a_ref, h_ref, w_ref, o_ref):
    o_ref[...] = h_ref[...] + _dot(a_ref[...].astype(BF16), w_ref[...])


def _out_project(attn, h, w_out):
    n, d = h.shape
    hd = attn.shape[1]
    tm = min(512, n)
    return pl.pallas_call(
        _outproj_body,
        grid=(n // tm,),
        in_specs=[pl.BlockSpec((tm, hd), lambda i: (i, 0)), pl.BlockSpec((tm, d), lambda i: (i, 0)),
                  pl.BlockSpec((hd, d), lambda i: (0, 0))],
        out_specs=pl.BlockSpec((tm, d), lambda i: (i, 0)),
        out_shape=jax.ShapeDtypeStruct((n, d), F32),
        compiler_params=_cparams(("arbitrary",)),
        name="nsa_outproj",
    )(attn, h, w_out.astype(BF16))


def kernel(x, p, rel_bias, norm_mix, norm_ffn, a_w_in, a_ln_g, a_ln_b, a_w_s, a_b_s, a_w_out, kv_norm, kv_w, cmp_pos, cmp_w1, cmp_b1, cmp_w2, k_norm, b_w_in, b_b_gate, q_norm, b_w_out, router_w, router_b, e_w_gu, e_b_gu, e_w_d, e_b_d, ple_w, ple_gate_w, ple_norm):
    b, s, d = x.shape
    n = b * s
    pf = p.reshape(p.shape[0], n, p.shape[-1])

    def moe_ple(h, i):
        return _moe_ple_layer(h, norm_ffn[i], router_w[i], router_b[i], e_w_gu[i], e_b_gu[i], e_w_d[i], e_b_d[i],
                              pf[i], ple_w[i], ple_gate_w[i], ple_norm[i])

    h = _gmlp_layer(x.reshape(n, d), norm_mix[0], a_w_in[0], a_ln_g[0], a_ln_b[0], a_w_s[0], a_b_s[0], a_w_out[0])
    h = moe_ple(h, 0)

    h3 = h.reshape(b, s, d)
    kc, vc, ks_aug, vs, kw, vw = _kv_project(h3, kv_norm, kv_w, k_norm)
    kc_pad, vc_pad = _compress(kc, vc, cmp_pos, cmp_w1, cmp_b1, cmp_w2, k_norm[0])
    front = jnp.zeros((b, N_KV_GROUPS, WINDOW, LANES), BF16).at[..., HEAD_DIM].set(1.0)
    kw_pad = jnp.concatenate([front, kw, front[:, :, :2 * Q_BLOCK]], axis=2)
    vw_pad = jnp.pad(vw, ((0, 0), (0, 0), (WINDOW, 2 * Q_BLOCK), (0, 0)))

    q, gates = _q_project(h3, norm_mix[1], b_w_in[0], b_b_gate[0], q_norm[0])
    attn = _nsa_attention(q, gates, kc_pad, vc_pad, ks_aug, vs, kw_pad, vw_pad, _attn_tables(rel_bias, s))
    h = _out_project(attn.reshape(n, -1), h, b_w_out[0])
    h = moe_ple(h, 1)
    return h.reshape(b, s, d)
```

```python
import functools
import math

import numpy as np
import jax
import jax.numpy as jnp
from jax import lax
from jax.experimental import pallas as pl
from jax.experimental.pallas import tpu as pltpu

D_MODEL = 1024
GMLP_CHUNK = 128
GMLP_GROUPS = 8
N_HEADS = 16
HEAD_DIM = 64
N_KV_GROUPS = 4
N_BRANCH = 3
CMP_BLOCK = 32
CMP_STRIDE = 16
CMP_HIDDEN = 256
SEL_BLOCK = 64
SEL_TOP = 16
WINDOW = 512
Q_BLOCK = 64
N_KV_SLOTS = 6
REL_BUCKETS = 32
REL_MAX_DIST = 2048
N_EXPERTS = 32
TOP_K = 4
SWIGLU_LIMIT = 7.0
SWIGLU_ALPHA = 1.702
NORM_EPS = 1e-6
NEG_INF = -1e30
FORCE = 1e30
LOG2E = math.log2(math.e)

LANES = 128
MOE_ROWS = 512
KEY_CHUNK = 512
Q_PAIR = 2
VMEM_LIMIT = 56 * 1024 * 1024

F32 = jnp.float32
BF16 = jnp.bfloat16


def _cparams(sem):
    return pltpu.CompilerParams(dimension_semantics=sem, vmem_limit_bytes=VMEM_LIMIT)


def _rms(x, g):
    return x * lax.rsqrt(jnp.mean(x * x, axis=-1, keepdims=True) + NORM_EPS) * g


def _dot(a, b):
    return jnp.dot(a, b, preferred_element_type=F32)


def _dot_nt(a, b):
    return lax.dot_general(a, b, (((1,), (1,)), ((), ())), preferred_element_type=F32)


def _dot_exact(a, b):
    return jnp.dot(a, b, preferred_element_type=F32, precision=lax.Precision.HIGHEST)


def _softmax2_rows(s):
    m = jnp.max(s, axis=-1, keepdims=True)
    p = jnp.exp2(s - m)
    return p / jnp.sum(p, axis=-1, keepdims=True)


def _head_lanes(x, idx):
    base = idx * HEAD_DIM // LANES * LANES
    y = x[:, base:base + LANES]
    return y if idx * HEAD_DIM == base else pltpu.roll(y, LANES - HEAD_DIM, 1)


def _low_lanes(x, fill):
    lane = lax.broadcasted_iota(jnp.int32, x.shape, 1)
    return jnp.where(lane < HEAD_DIM, x, fill)


def _argmax_first(x, ids, n):
    mx = jnp.max(x, axis=-1, keepdims=True)
    return mx, jnp.min(jnp.where(x == mx, ids, float(n)), axis=-1, keepdims=True)


def _gmlp_body(x_ref, nm_ref, win_ref, lng_ref, lnb_ref, ws_ref, bs_ref, wout_ref, o_ref):
    tm = x_ref.shape[0]
    gd = win_ref.shape[1] // 2
    gdim = gd // GMLP_GROUPS
    x = x_ref[...]
    xn = _rms(x, nm_ref[...]).astype(BF16)
    z = jax.nn.gelu(_dot(xn, win_ref[...]))
    u = z[:, :gd]
    v = z[:, gd:]
    mu = jnp.mean(v, axis=-1, keepdims=True)
    vc = v - mu
    vln = vc * lax.rsqrt(jnp.mean(vc * vc, axis=-1, keepdims=True) + NORM_EPS) * lng_ref[...] + lnb_ref[...]
    vb = vln.astype(BF16)
    row = lax.broadcasted_iota(jnp.int32, (GMLP_CHUNK, GMLP_CHUNK), 0)
    col = lax.broadcasted_iota(jnp.int32, (GMLP_CHUNK, GMLP_CHUNK), 1)
    causal = row >= col
    chunks = []
    for c in range(tm // GMLP_CHUNK):
        cols = []
        for g in range(GMLP_GROUPS):
            wsg = jnp.where(causal, ws_ref[g], 0.0).astype(BF16)
            vg = vb[c * GMLP_CHUNK:(c + 1) * GMLP_CHUNK, g * gdim:(g + 1) * gdim]
            cols.append(_dot(wsg, vg) + bs_ref[g])
        chunks.append(jnp.concatenate(cols, axis=1))
    mixed = jnp.concatenate(chunks, axis=0)
    gated = (u * mixed).astype(BF16)
    o_ref[...] = x + _dot(gated, wout_ref[...])


def _gmlp_layer(h, norm_g, w_in, ln_g, ln_b, w_s, b_s, w_out):
    n, d = h.shape
    gd2 = w_in.shape[1]
    gd = gd2 // 2
    tm = min(512, n)
    full = lambda *shape: pl.BlockSpec(shape, lambda i: (0,) * len(shape))
    return pl.pallas_call(
        _gmlp_body,
        grid=(n // tm,),
        in_specs=[pl.BlockSpec((tm, d), lambda i: (i, 0)),
                  full(1, d), full(d, gd2), full(1, gd), full(1, gd),
                  full(GMLP_GROUPS, GMLP_CHUNK, GMLP_CHUNK), full(GMLP_GROUPS, GMLP_CHUNK, 1),
                  full(gd, d)],
        out_specs=pl.BlockSpec((tm, d), lambda i: (i, 0)),
        out_shape=jax.ShapeDtypeStruct((n, d), F32),
        compiler_params=_cparams(("arbitrary",)),
        name="gmlp_layer",
    )(h, norm_g.reshape(1, d), w_in.astype(BF16), ln_g.reshape(1, gd), ln_b.reshape(1, gd),
      w_s, b_s.reshape(GMLP_GROUPS, GMLP_CHUNK, 1), w_out.astype(BF16))


def _route_body(h_ref, ng_ref, rw_ref, rb_ref, o_ref, cnt_ref, run_ref):
    i = pl.program_id(0)
    tm = h_ref.shape[0]

    @pl.when(i == 0)
    def _():
        run_ref[...] = jnp.zeros_like(run_ref)

    xn = _rms(h_ref[...], ng_ref[...])
    logits = _dot_exact(xn, rw_ref[...]) + rb_ref[...]
    eid = lax.broadcasted_iota(jnp.int32, logits.shape, 1).astype(F32)
    lane = lax.broadcasted_iota(jnp.int32, (tm, LANES), 1)
    work = logits
    vals, idxs = [], []
    for _ in range(TOP_K):
        mx, ix = _argmax_first(work, eid, N_EXPERTS)
        vals.append(mx)
        idxs.append(ix)
        work = jnp.where(eid == ix, -jnp.inf, work)
    exps = [jnp.exp(v - vals[0]) for v in vals]
    den = exps[0]
    for e in exps[1:]:
        den = den + e
    onehot = jnp.zeros(logits.shape, F32)
    for ix in idxs:
        onehot = onehot + (eid == ix).astype(F32)
    r = lax.broadcasted_iota(jnp.int32, (tm, tm), 0)
    c = lax.broadcasted_iota(jnp.int32, (tm, tm), 1)
    before = (r > c).astype(BF16)
    prefix = _dot(before, onehot.astype(BF16)) + run_ref[...]
    out = jnp.zeros((tm, LANES), F32)
    for k in range(TOP_K):
        rank = jnp.sum(jnp.where(eid == idxs[k], prefix, 0.0), axis=-1, keepdims=True)
        out = jnp.where(lane == k, idxs[k], out)
        out = jnp.where(lane == TOP_K + k, exps[k] / den, out)
        out = jnp.where(lane == 2 * TOP_K + k, rank, out)
    o_ref[...] = out
    run_ref[...] = run_ref[...] + jnp.sum(onehot, axis=0, keepdims=True)
    cnt_ref[...] = run_ref[...]


def _moe_route(h, norm_g, router_w, router_b):
    n, d = h.shape
    tm = min(512, n)
    full = lambda *shape: pl.BlockSpec(shape, lambda i: (0,) * len(shape))
    return pl.pallas_call(
        _route_body,
        grid=(n // tm,),
        in_specs=[pl.BlockSpec((tm, d), lambda i: (i, 0)), full(1, d), full(d, N_EXPERTS), full(1, N_EXPERTS)],
        out_specs=[pl.BlockSpec((tm, LANES), lambda i: (i, 0)), full(1, N_EXPERTS)],
        out_shape=[jax.ShapeDtypeStruct((n, LANES), F32), jax.ShapeDtypeStruct((1, N_EXPERTS), F32)],
        scratch_shapes=[pltpu.VMEM((1, N_EXPERTS), F32)],
        compiler_params=_cparams(("arbitrary",)),
        name="moe_route",
    )(h, norm_g.reshape(1, d), router_w, router_b.reshape(1, N_EXPERTS))


def _dispatch_body(dest_ref, h_ref, ng_ref, xs_in, xs_out, buf, sem):
    del xs_in
    tm = h_ref.shape[0]
    buf[...] = _rms(h_ref[...], ng_ref[...])

    def issue(j, carry):
        for k in range(TOP_K):
            r = dest_ref[0, 0, j * TOP_K + k]
            pltpu.make_async_copy(buf.at[pl.ds(j, 1), :], xs_out.at[pl.ds(r, 1), :], sem).start()
        return carry

    lax.fori_loop(0, tm, issue, 0)
    for _ in range(TOP_K):
        pltpu.make_async_copy(buf, xs_out.at[pl.ds(0, tm), :], sem).wait()


def _moe_dispatch(h, norm_g, dest, n_rows):
    n, d = h.shape
    tm = min(256, n)
    xs0 = jnp.zeros((n_rows, d), F32)
    return pl.pallas_call(
        _dispatch_body,
        grid=(n // tm,),
        in_specs=[pl.BlockSpec((1, 1, tm * TOP_K), lambda i: (i, 0, 0), memory_space=pltpu.SMEM),
                  pl.BlockSpec((tm, d), lambda i: (i, 0)),
                  pl.BlockSpec((1, d), lambda i: (0, 0)),
                  pl.BlockSpec(memory_space=pl.ANY)],
        out_specs=pl.BlockSpec(memory_space=pl.ANY),
        out_shape=jax.ShapeDtypeStruct((n_rows, d), F32),
        scratch_shapes=[pltpu.VMEM((tm, d), F32), pltpu.SemaphoreType.DMA(())],
        input_output_aliases={3: 0},
        compiler_params=_cparams(("arbitrary",)),
        name="moe_dispatch",
    )(dest.reshape(n // tm, 1, tm * TOP_K), h, norm_g.reshape(1, d), xs0)


def _expert_body(be_ref, nu_ref, xs_ref, wgu_ref, bgu_ref, wd_ref, bd_ref, ys_ref, wgu_bf, wd_bf):
    i = pl.program_id(0)
    ed = wd_ref.shape[1]

    @pl.when((i == 0) | (be_ref[i] != be_ref[jnp.maximum(i - 1, 0)]))
    def _():
        wgu_bf[...] = wgu_ref[0].astype(BF16)
        wd_bf[...] = wd_ref[0].astype(BF16)

    @pl.when(i < nu_ref[0])
    def _():
        x = xs_ref[...].astype(BF16)
        gu = _dot(x, wgu_bf[...]) + bgu_ref[0]
        gate = jnp.minimum(gu[:, :ed], SWIGLU_LIMIT)
        up = jnp.clip(gu[:, ed:], -SWIGLU_LIMIT, SWIGLU_LIMIT)
        glu = gate * jax.nn.sigmoid(gate * SWIGLU_ALPHA)
        ys_ref[...] = _dot(((up + 1.0) * glu).astype(BF16), wd_bf[...]) + bd_ref[0]

    @pl.when(i >= nu_ref[0])
    def _():
        ys_ref[...] = jnp.zeros_like(ys_ref)


def _moe_experts(xs, blk_expert, n_used, w_gu, b_gu, w_d, b_d):
    n_rows, d = xs.shape
    ed = w_d.shape[1]
    n_blk = n_rows // MOE_ROWS
    return pl.pallas_call(
        _expert_body,
        grid_spec=pltpu.PrefetchScalarGridSpec(
            num_scalar_prefetch=2, grid=(n_blk,),
            in_specs=[pl.BlockSpec((MOE_ROWS, d), lambda i, be, nu: (jnp.maximum(jnp.minimum(i, nu[0] - 1), 0), 0)),
                      pl.BlockSpec((1, d, 2 * ed), lambda i, be, nu: (be[i], 0, 0)),
                      pl.BlockSpec((1, 1, 2 * ed), lambda i, be, nu: (be[i], 0, 0)),
                      pl.BlockSpec((1, ed, d), lambda i, be, nu: (be[i], 0, 0)),
                      pl.BlockSpec((1, 1, d), lambda i, be, nu: (be[i], 0, 0))],
            out_specs=pl.BlockSpec((MOE_ROWS, d), lambda i, be, nu: (i, 0)),
            scratch_shapes=[pltpu.VMEM((d, 2 * ed), BF16), pltpu.VMEM((ed, d), BF16)]),
        out_shape=jax.ShapeDtypeStruct((n_rows, d), F32),
        compiler_params=_cparams(("arbitrary",)),
        name="moe_experts",
    )(blk_expert, n_used, xs, w_gu, b_gu.reshape(N_EXPERTS, 1, 2 * ed), w_d, b_d.reshape(N_EXPERTS, 1, d))


def _combine_body(dest_ref, rt_ref, h_ref, p_ref, pw_ref, pg_ref, pn_ref, ys_hbm, o_ref, buf, sem):
    tm = h_ref.shape[0]

    def issue(j, carry):
        for k in range(TOP_K):
            r = dest_ref[0, 0, j * TOP_K + k]
            pltpu.make_async_copy(ys_hbm.at[pl.ds(r, 1), :], buf.at[k, pl.ds(j, 1), :], sem).start()
        return carry

    lax.fori_loop(0, tm, issue, 0)
    for k in range(TOP_K):
        pltpu.make_async_copy(ys_hbm.at[pl.ds(0, tm), :], buf.at[k], sem).wait()
    rt = rt_ref[...]
    h = h_ref[...]
    for k in range(TOP_K):
        h = h + rt[:, TOP_K + k:TOP_K + k + 1] * buf[k]
    emb = _dot(p_ref[...].astype(BF16), pw_ref[...])
    gate = jax.nn.sigmoid(_dot(_rms(h, pn_ref[...]).astype(BF16), pg_ref[...]))
    o_ref[...] = h + emb * gate


def _moe_combine_ple(h, route, dest, ys, p, ple_w, ple_gate_w, ple_norm):
    n, d = h.shape
    pd = p.shape[1]
    tm = min(256, n)
    full = lambda *shape: pl.BlockSpec(shape, lambda i: (0,) * len(shape))
    return pl.pallas_call(
        _combine_body,
        grid=(n // tm,),
        in_specs=[pl.BlockSpec((1, 1, tm * TOP_K), lambda i: (i, 0, 0), memory_space=pltpu.SMEM),
                  pl.BlockSpec((tm, LANES), lambda i: (i, 0)),
                  pl.BlockSpec((tm, d), lambda i: (i, 0)),
                  pl.BlockSpec((tm, pd), lambda i: (i, 0)),
                  full(pd, d), full(d, d), full(1, d),
                  pl.BlockSpec(memory_space=pl.ANY)],
        out_specs=pl.BlockSpec((tm, d), lambda i: (i, 0)),
        out_shape=jax.ShapeDtypeStruct((n, d), F32),
        scratch_shapes=[pltpu.VMEM((TOP_K, tm, d), F32), pltpu.SemaphoreType.DMA(())],
        compiler_params=_cparams(("arbitrary",)),
        name="moe_combine_ple",
    )(dest.reshape(n // tm, 1, tm * TOP_K), route, h, p, ple_w.astype(BF16), ple_gate_w.astype(BF16),
      ple_norm.reshape(1, d), ys)


def _moe_ple_layer(h, norm_g, router_w, router_b, w_gu, b_gu, w_d, b_d, p, ple_w, ple_gate_w, ple_norm):
    n, _ = h.shape
    route, counts = _moe_route(h, norm_g, router_w, router_b)
    counts = counts[0].astype(jnp.int32)
    pad_counts = (counts + MOE_ROWS - 1) // MOE_ROWS * MOE_ROWS
    pad_ends = jnp.cumsum(pad_counts)
    pad_starts = pad_ends - pad_counts
    top_idx = route[:, :TOP_K].astype(jnp.int32)
    rank = route[:, 2 * TOP_K:3 * TOP_K].astype(jnp.int32)
    dest = (pad_starts[top_idx] + rank).reshape(-1)
    n_blk = -(-(n * TOP_K) // MOE_ROWS) + N_EXPERTS
    blk_start = jnp.arange(n_blk, dtype=jnp.int32) * MOE_ROWS
    blk_expert = jnp.minimum(jnp.sum((pad_ends[None, :] <= blk_start[:, None]).astype(jnp.int32), axis=1),
                             N_EXPERTS - 1)
    n_used = (pad_ends[-1:] // MOE_ROWS).astype(jnp.int32)
    xs = _moe_dispatch(h, norm_g, dest, n_blk * MOE_ROWS)
    ys = _moe_experts(xs, blk_expert, n_used, w_gu, b_gu, w_d, b_d)
    return _moe_combine_ple(h, route, dest, ys, p, ple_w, ple_gate_w, ple_norm)


def _kv_body(h_ref, ng_ref, w_ref, seg_ref, kn_ref, kc_ref, vc_ref, ks_ref, vs_ref, kw_ref, vw_ref):
    ts = h_ref.shape[1]
    gw = N_KV_GROUPS * HEAD_DIM
    st = pl.program_id(1)
    hn = _rms(h_ref[0], ng_ref[...]).astype(BF16)
    kv = _dot(hn, w_ref[...])

    def knorm(x, j):
        ms = _dot_exact(x * x, seg_ref[...])
        return x * lax.rsqrt(ms + NORM_EPS) * kn_ref[j]

    k_c, v_c = kv[:, 0:gw], kv[:, gw:2 * gw]
    k_s, v_s = knorm(kv[:, 2 * gw:3 * gw], 1), kv[:, 3 * gw:4 * gw]
    k_w, v_w = knorm(kv[:, 4 * gw:5 * gw], 2), kv[:, 5 * gw:6 * gw]
    tok = st * ts + lax.broadcasted_iota(jnp.int32, (ts, LANES), 0)
    blk = lax.broadcasted_iota(jnp.int32, (ts, LANES), 1)
    onehot = (tok // SEL_BLOCK == blk).astype(BF16)
    for g in range(N_KV_GROUPS):
        kc_ref[0, g] = _head_lanes(k_c, g)[:, :HEAD_DIM]
        vc_ref[0, g] = _head_lanes(v_c, g)[:, :HEAD_DIM]
        ks_ref[0, g] = jnp.concatenate([_low_lanes(_head_lanes(k_s, g), 0.0).astype(BF16), onehot], axis=1)
        vs_ref[0, g] = _low_lanes(_head_lanes(v_s, g), 0.0).astype(BF16)
        kw_ref[0, g] = _low_lanes(_head_lanes(k_w, g), 0.0).astype(BF16)
        vw_ref[0, g] = _low_lanes(_head_lanes(v_w, g), 0.0).astype(BF16)


def _kv_project(h3, kv_norm, kv_w, k_norm):
    b, s, d = h3.shape
    gw = N_KV_GROUPS * HEAD_DIM
    ts = min(512, s)
    seg = jnp.asarray(np.kron(np.eye(N_KV_GROUPS), np.full((HEAD_DIM, HEAD_DIM), 1.0 / HEAD_DIM)), F32)
    kn = jnp.tile(k_norm, (1, N_KV_GROUPS)).reshape(N_BRANCH, 1, gw)
    full = lambda *shape: pl.BlockSpec(shape, lambda bi, si: (0,) * len(shape))
    hd = lambda w: pl.BlockSpec((1, N_KV_GROUPS, ts, w), lambda bi, si: (bi, 0, si, 0))
    sds = lambda w, dt: jax.ShapeDtypeStruct((b, N_KV_GROUPS, s, w), dt)
    return pl.pallas_call(
        _kv_body,
        grid=(b, s // ts),
        in_specs=[pl.BlockSpec((1, ts, d), lambda bi, si: (bi, si, 0)), full(1, d), full(d, N_KV_SLOTS * gw),
                  full(gw, gw), full(N_BRANCH, 1, gw)],
        out_specs=[hd(HEAD_DIM), hd(HEAD_DIM), hd(2 * LANES), hd(LANES), hd(LANES), hd(LANES)],
        out_shape=[sds(HEAD_DIM, F32), sds(HEAD_DIM, F32), sds(2 * LANES, BF16), sds(LANES, BF16),
                   sds(LANES, BF16), sds(LANES, BF16)],
        compiler_params=_cparams(("arbitrary", "arbitrary")),
        name="kv_project",
    )(h3, kv_norm.reshape(1, d), kv_w.astype(BF16), seg, kn)


def _compress_body(kc_ref, vc_ref, pos_ref, w1_ref, b1_ref, w2_ref, kn_ref, ko_ref, vo_ref):
    nc = kc_ref.shape[2]
    half = kc_ref.shape[3]
    row = lax.broadcasted_iota(jnp.int32, (nc, 1), 0)
    valid = row < nc - 1

    def compress(c, j):
        a = _dot((c + pos_ref[j, 0]).astype(BF16), w1_ref[j, 0])
        bm = _dot((c + pos_ref[j, 1]).astype(BF16), w1_ref[j, 1])
        nxt = pltpu.roll(bm, nc - 1, 0)
        hid = jax.nn.gelu(a + nxt + b1_ref[j])
        return _dot(hid.astype(BF16), w2_ref[j])

    kraw = compress(kc_ref[0, 0], 0)
    ms = jnp.sum(kraw * kraw, axis=-1, keepdims=True) * (1.0 / HEAD_DIM)
    kcmp = kraw * lax.rsqrt(ms + NORM_EPS) * kn_ref[...]
    vcmp = jnp.where(valid, compress(vc_ref[0, 0], 1), 0.0)
    lane = lax.broadcasted_iota(jnp.int32, (nc, LANES), 1)
    flag = (lane == HEAD_DIM).astype(F32)
    ko_ref[0, 0, 0:nc, :] = flag
    ko_ref[0, 0, nc:2 * nc, :] = jnp.where(valid, kcmp, flag)
    vo_ref[0, 0, 0:nc, :] = jnp.zeros((nc, LANES), F32)
    vo_ref[0, 0, nc:2 * nc, :] = vcmp


def _compress(kc, vc, cmp_pos, cmp_w1, cmp_b1, cmp_w2, k_norm0):
    b, g, s, dh = kc.shape
    nc = s // CMP_STRIDE
    half = CMP_STRIDE * dh
    kc_r = kc.reshape(b, g, nc, half)
    vc_r = vc.reshape(b, g, nc, half)
    pos = cmp_pos.reshape(2, 2, 1, half)
    w1 = cmp_w1.reshape(2, 2, half, CMP_HIDDEN).astype(BF16)
    w2 = jnp.pad(cmp_w2, ((0, 0), (0, 0), (0, LANES - dh))).astype(BF16)
    kn = jnp.pad(k_norm0, (0, LANES - dh)).reshape(1, LANES)
    full = lambda *shape: pl.BlockSpec(shape, lambda bi, gi: (0,) * len(shape))
    blk = pl.BlockSpec((1, 1, nc, half), lambda bi, gi: (bi, gi, 0, 0))
    out = pl.BlockSpec((1, 1, 2 * nc, LANES), lambda bi, gi: (bi, gi, 0, 0))
    return pl.pallas_call(
        _compress_body,
        grid=(b, g),
        in_specs=[blk, blk, full(2, 2, 1, half), full(2, 2, half, CMP_HIDDEN), full(2, 1, CMP_HIDDEN),
                  full(2, CMP_HIDDEN, LANES), full(1, LANES)],
        out_specs=[out, out],
        out_shape=[jax.ShapeDtypeStruct((b, g, 2 * nc, LANES), F32)] * 2,
        compiler_params=_cparams(("arbitrary", "arbitrary")),
        name="kv_compress",
    )(kc_r, vc_r, pos, w1, cmp_b1.reshape(2, 1, CMP_HIDDEN), w2, kn)


def _qproj_body(h_ref, ng_ref, w_ref, bg_ref, ind_ref, indt_ref, qn_ref, q_ref, gate_ref):
    hd = N_HEADS * HEAD_DIM
    xn = _rms(h_ref[0], ng_ref[...]).astype(BF16)
    proj = _dot(xn, w_ref[...])
    q = proj[:, :hd]
    ms = _dot_exact(q * q, ind_ref[...]) * (1.0 / HEAD_DIM)
    scale = _dot_exact(lax.rsqrt(ms + NORM_EPS), indt_ref[...])
    qn = q * scale * qn_ref[...] * (HEAD_DIM ** -0.5 * LOG2E)
    lane = lax.broadcasted_iota(jnp.int32, (q.shape[0], LANES), 1)
    fill = jnp.where(lane == HEAD_DIM, NEG_INF, 0.0)
    for h in range(N_HEADS):
        q_ref[0, h] = _low_lanes(_head_lanes(qn, h), fill).astype(BF16)
    gate_ref[0] = jax.nn.sigmoid(proj[:, hd:] + bg_ref[...])


def _q_project(h3, norm_g, w_in, b_gate, q_norm):
    b, s, d = h3.shape
    hd = N_HEADS * HEAD_DIM
    ng = N_BRANCH * N_HEADS
    ts = min(512, s)
    w = jnp.pad(w_in, ((0, 0), (0, LANES - ng))).astype(BF16)
    bg = jnp.pad(b_gate, (0, LANES - ng)).reshape(1, LANES)
    ind = np.zeros((hd, LANES), np.float32)
    ind[np.arange(hd), np.arange(hd) // HEAD_DIM] = 1.0
    full = lambda *shape: pl.BlockSpec(shape, lambda bi, si: (0,) * len(shape))
    return pl.pallas_call(
        _qproj_body,
        grid=(b, s // ts),
        in_specs=[pl.BlockSpec((1, ts, d), lambda bi, si: (bi, si, 0)), full(1, d), full(d, hd + LANES),
                  full(1, LANES), full(hd, LANES), full(LANES, hd), full(1, hd)],
        out_specs=[pl.BlockSpec((1, N_HEADS, ts, LANES), lambda bi, si: (bi, 0, si, 0)),
                   pl.BlockSpec((1, ts, LANES), lambda bi, si: (bi, si, 0))],
        out_shape=[jax.ShapeDtypeStruct((b, N_HEADS, s, LANES), BF16),
                   jax.ShapeDtypeStruct((b, s, LANES), F32)],
        compiler_params=_cparams(("arbitrary", "arbitrary")),
        name="nsa_qproj",
    )(h3, norm_g.reshape(1, d), w, bg, jnp.asarray(ind), jnp.asarray(ind.T),
      jnp.tile(q_norm, N_HEADS).reshape(1, hd))


def _t5_bucket_np(dist):
    n = np.maximum(dist, 0)
    max_exact = REL_BUCKETS // 2
    nf = np.maximum(n, 1).astype(np.float64)
    large = max_exact + (np.log(nf / max_exact) / math.log(REL_MAX_DIST / max_exact)
                         * (REL_BUCKETS - max_exact)).astype(np.int64)
    return np.where(n < max_exact, n, np.minimum(large, REL_BUCKETS - 1))


def _bias_table(rel_bias, dist, valid):
    r = N_HEADS // N_KV_GROUPS
    tab = rel_bias.astype(F32).T.reshape(N_KV_GROUPS, r, REL_BUCKETS)
    onehot = jnp.asarray(_t5_bucket_np(dist)[..., None] == np.arange(REL_BUCKETS), F32)
    bias = jnp.einsum('xqln,grn->gxrql', onehot, tab, precision=lax.Precision.HIGHEST) * LOG2E
    return jnp.where(jnp.asarray(valid)[None, :, None, :, :], bias, NEG_INF)


def _n_delta(seq):
    d = np.arange(seq + SEL_BLOCK)
    bk = _t5_bucket_np(d)
    change = np.nonzero(bk[1:] != bk[:-1])[0]
    d_const = int(change[-1]) + 1 if change.size else 0
    return -(-(d_const + SEL_BLOCK - 1) // SEL_BLOCK) + 1


def _attn_tables(rel_bias, seq):
    r = N_HEADS // N_KV_GROUPS
    qi = np.arange(Q_BLOCK)[:, None]
    rows = lambda t: jnp.transpose(t, (0, 2, 1, 3, 4)).reshape(N_KV_GROUPS, r * Q_PAIR * Q_BLOCK, t.shape[-1])
    nc = seq // CMP_STRIDE
    j = np.arange(nc)[None, :]
    dist_c = np.stack([qi - (CMP_BLOCK - 1) - Q_BLOCK * (Q_PAIR - u) + CMP_STRIDE * (nc - j)
                       for u in range(Q_PAIR)])
    rc = rows(_bias_table(rel_bias, dist_c, dist_c >= 0))
    nd = _n_delta(seq)
    delta = np.arange(-1, nd + 1)[:, None, None]
    kj = np.arange(2 * SEL_BLOCK)[None, None, :]
    dist_s = SEL_BLOCK * (delta - kj // SEL_BLOCK) + qi[None] - kj % SEL_BLOCK
    bt = _bias_table(rel_bias, dist_s, dist_s >= 0)
    jw = np.arange(WINDOW + 4 * Q_BLOCK)[None, :]
    dist_w = np.stack([Q_BLOCK * u + qi - jw + WINDOW for u in range(Q_PAIR)])
    wb = rows(_bias_table(rel_bias, dist_w, (dist_w >= 0) & (dist_w < WINDOW)))
    n_sel = seq // SEL_BLOCK
    cs = np.arange(nc) * CMP_STRIDE
    ss = np.arange(n_sel) * SEL_BLOCK
    ov = np.clip(np.minimum(cs[:, None] + CMP_BLOCK, ss[None, :] + SEL_BLOCK)
                 - np.maximum(cs[:, None], ss[None, :]), 0, None) / CMP_BLOCK
    return rc, bt, wb, jnp.asarray(ov.T.astype(np.float32)).astype(BF16)


def _attn_body(q_ref, gate_ref, kc_ref, vc_ref, ks_ref, vs_ref, kw_ref, vw_ref, rc_ref, bt_ref, wb_ref, ovt_ref,
               o_ref, sa_ref, sb_ref):
    g = pl.program_id(1)
    i0 = Q_PAIR * pl.program_id(2)
    r = q_ref.shape[1]
    pq = Q_PAIR * Q_BLOCK
    rq = r * pq
    nc = rc_ref.shape[2]
    n_sel = ovt_ref.shape[0]
    nd = bt_ref.shape[1] - 2
    wl = wb_ref.shape[2]
    per = SEL_BLOCK // CMP_STRIDE

    heads = [slice(h * pq, (h + 1) * pq) for h in range(r)]
    q_pad = q_ref[0].reshape(rq, LANES)

    end = pl.multiple_of(per * (i0 + Q_PAIR), Q_PAIR * per)
    kcw = kc_ref[0, 0, pl.ds(end, nc), :].astype(BF16)
    vcw = vc_ref[0, 0, pl.ds(end, nc), :].astype(BF16)
    qpos = i0 * Q_BLOCK + lax.broadcasted_iota(jnp.int32, (rq, 1), 0) % pq
    p_c = _softmax2_rows(_dot_nt(q_pad, kcw) + rc_ref[0]) * (qpos >= CMP_BLOCK - 1).astype(F32)
    o_c = _dot(p_c.astype(BF16), vcw)
    p_sum = p_c[heads[0]]
    for h in range(1, r):
        p_sum = p_sum + p_c[heads[h]]

    p_hi = p_sum.astype(BF16)
    p_lo = (p_sum - p_hi.astype(F32)).astype(BF16)
    imp2 = _dot_nt(ovt_ref[...], jnp.concatenate([p_hi, p_lo], axis=0))
    imp = imp2[:, :pq] + imp2[:, pq:]

    ws = pl.multiple_of(i0 * Q_BLOCK, pq)
    sw = _dot_nt(q_pad, kw_ref[0, 0, pl.ds(ws, wl), :]) + wb_ref[0]
    o_w = _dot(_softmax2_rows(sw).astype(BF16), vw_ref[0, 0, pl.ds(ws, wl), :])

    shift = i0 + Q_PAIR
    blk_rel = lax.broadcasted_iota(jnp.int32, (n_sel, pq), 0)
    blk = blk_rel + shift - n_sel
    cur = i0 + lax.broadcasted_iota(jnp.int32, (n_sel, pq), 1) // Q_BLOCK
    forced = (blk == 0) | (blk == cur) | (blk == cur - 1)
    imp = jnp.where(forced, FORCE, jnp.where(blk > cur, NEG_INF, imp))
    imp = jnp.where(blk < 0, -jnp.inf, imp)
    ids = blk_rel.astype(F32)
    sel = jnp.zeros((n_sel, pq), F32)
    for _ in range(min(SEL_TOP, n_sel)):
        mx = jnp.max(imp, axis=0, keepdims=True)
        ix = jnp.min(jnp.where(imp == mx, ids, float(n_sel)), axis=0, keepdims=True)
        hit = ids == ix
        sel = jnp.where(hit, 1.0, sel)
        imp = jnp.where(hit, -jnp.inf, imp)
    unsel = jnp.where((sel > 0.0) & (blk >= 0), 0.0, NEG_INF)
    unsel = pltpu.roll(unsel.T, shift % n_sel, 1).astype(BF16)
    if n_sel < LANES:
        unsel = jnp.concatenate([unsel, jnp.zeros((pq, LANES - n_sel), BF16)], axis=1)

    q_aug = jnp.concatenate([q_pad, jnp.concatenate([unsel] * r, axis=0)], axis=1)
    kchunk = sa_ref.shape[1]
    n_chunks = ks_ref.shape[2] // kchunk
    cblocks = kchunk // SEL_BLOCK
    pairs = cblocks // 2

    def scores_to(dst, c):
        start = pl.multiple_of(jnp.minimum(c, n_chunks - 1) * kchunk, kchunk)
        sc = _dot_nt(q_aug, ks_ref[0, 0, pl.ds(start, kchunk), :])
        d0 = i0 - c * cblocks
        tiles = [[jnp.clip(d0 + u - 2 * pm, -1, nd) + 1 for pm in range(pairs)] for u in range(Q_PAIR)]
        dst[...] = sc + jnp.concatenate(
            [jnp.concatenate([bt_ref[0, t, h] for t in tiles[u]], axis=1)
             for h in range(r) for u in range(Q_PAIR)], axis=0)

    def absorb(src, c, m, l, acc):
        start = pl.multiple_of(c * kchunk, kchunk)
        sc = src[...]
        m_new = jnp.maximum(m, jnp.max(sc, axis=-1, keepdims=True))
        alpha = jnp.exp2(m - m_new)
        p = jnp.exp2(sc - m_new)
        l = alpha * l + jnp.sum(p, axis=-1, keepdims=True)
        acc = alpha * acc + _dot(p.astype(BF16), vs_ref[0, 0, pl.ds(start, kchunk), :])
        return m_new, l, acc

    def two_chunks(j, carry):
        scores_to(sb_ref, 2 * j + 1)
        carry = absorb(sa_ref, 2 * j, *carry)
        scores_to(sa_ref, 2 * j + 2)
        return absorb(sb_ref, 2 * j + 1, *carry)

    scores_to(sa_ref, 0)
    init = (jnp.full((rq, 1), NEG_INF, F32), jnp.zeros((rq, 1), F32), jnp.zeros((rq, LANES), F32))
    _, l_s, acc_s = lax.fori_loop(0, (i0 + Q_PAIR - 1) // (2 * cblocks) + 1, two_chunks, init)
    o_s = acc_s / l_s

    gates = gate_ref[0]
    glane = lax.broadcasted_iota(jnp.int32, gates.shape, 1)
    outs = []
    for h in range(r):
        head = g * r + h
        gs = [jnp.sum(jnp.where(glane == br * N_HEADS + head, gates, 0.0), axis=-1, keepdims=True)
              for br in range(N_BRANCH)]
        outs.append(gs[0] * o_c[heads[h]] + gs[1] * o_s[heads[h]] + gs[2] * o_w[heads[h]])
    o_ref[0] = jnp.concatenate(
        [outs[h] + pltpu.roll(outs[h + 1], HEAD_DIM, 1) for h in range(0, r, 2)], axis=1)


def _nsa_attention(q, gates, kc_pad, vc_pad, ks_aug, vs, kw_pad, vw_pad, tables):
    b, _, s, _ = q.shape
    r = N_HEADS // N_KV_GROUPS
    rc, bt, wb, ov = tables
    n_qb = s // Q_BLOCK
    per_bg = lambda a: pl.BlockSpec((1, 1) + a.shape[2:], lambda bi, gi, qi: (bi, gi, 0, 0))
    per_g = lambda a: pl.BlockSpec((1,) + a.shape[1:], lambda bi, gi, qi: (gi,) + (0,) * (a.ndim - 1))
    pq = Q_PAIR * Q_BLOCK
    return pl.pallas_call(
        _attn_body,
        grid=(b, N_KV_GROUPS, n_qb // Q_PAIR),
        in_specs=[pl.BlockSpec((1, r, pq, LANES), lambda bi, gi, qi: (bi, gi, qi, 0)),
                  pl.BlockSpec((1, pq, LANES), lambda bi, gi, qi: (bi, qi, 0)),
                  per_bg(kc_pad), per_bg(vc_pad), per_bg(ks_aug), per_bg(vs), per_bg(kw_pad), per_bg(vw_pad),
                  per_g(rc), per_g(bt), per_g(wb),
                  pl.BlockSpec(ov.shape, lambda bi, gi, qi: (0, 0))],
        out_specs=pl.BlockSpec((1, pq, r * HEAD_DIM), lambda bi, gi, qi: (bi, qi, gi)),
        out_shape=jax.ShapeDtypeStruct((b, s, N_HEADS * HEAD_DIM), F32),
        scratch_shapes=[pltpu.VMEM((r * pq, min(KEY_CHUNK, s // 2)), F32)] * 2,
        compiler_params=_cparams(("arbitrary", "arbitrary", "arbitrary")),
        name="nsa_attention",
    )(q, gates, kc_pad, vc_pad, ks_aug, vs, kw_pad, vw_pad, rc, bt, wb, ov)


def _outproj_body(a_ref, h_ref, w_ref, o_ref):
    o_ref[...] = h_ref[...] + _dot(a_ref[...].astype(BF16), w_ref[...])


def _out_project(attn, h, w_out):
    n, d = h.shape
    hd = attn.shape[1]
    tm = min(512, n)
    return pl.pallas_call(
        _outproj_body,
        grid=(n // tm,),
        in_specs=[pl.BlockSpec((tm, hd), lambda i: (i, 0)), pl.BlockSpec((tm, d), lambda i: (i, 0)),
                  pl.BlockSpec((hd, d), lambda i: (0, 0))],
        out_specs=pl.BlockSpec((tm, d), lambda i: (i, 0)),
        out_shape=jax.ShapeDtypeStruct((n, d), F32),
        compiler_params=_cparams(("arbitrary",)),
        name="nsa_outproj",
    )(attn, h, w_out.astype(BF16))


def kernel(x, p, rel_bias, norm_mix, norm_ffn, a_w_in, a_ln_g, a_ln_b, a_w_s, a_b_s, a_w_out, kv_norm, kv_w, cmp_pos, cmp_w1, cmp_b1, cmp_w2, k_norm, b_w_in, b_b_gate, q_norm, b_w_out, router_w, router_b, e_w_gu, e_b_gu, e_w_d, e_b_d, ple_w, ple_gate_w, ple_norm):
    b, s, d = x.shape
    n = b * s
    pf = p.reshape(p.shape[0], n, p.shape[-1])

    def moe_ple(h, i):
        return _moe_ple_layer(h, norm_ffn[i], router_w[i], router_b[i], e_w_gu[i], e_b_gu[i], e_w_d[i], e_b_d[i],
                              pf[i], ple_w[i], ple_gate_w[i], ple_norm[i])

    h = _gmlp_layer(x.reshape(n, d), norm_mix[0], a_w_in[0], a_ln_g[0], a_ln_b[0], a_w_s[0], a_b_s[0], a_w_out[0])
    h = moe_ple(h, 0)

    h3 = h.reshape(b, s, d)
    kc, vc, ks_aug, vs, kw, vw = _kv_project(h3, kv_norm, kv_w, k_norm)
    kc_pad, vc_pad = _compress(kc, vc, cmp_pos, cmp_w1, cmp_b1, cmp_w2, k_norm[0])
    front = jnp.zeros((b, N_KV_GROUPS, WINDOW, LANES), BF16).at[..., HEAD_DIM].set(1.0)
    kw_pad = jnp.concatenate([front, kw, front[:, :, :2 * Q_BLOCK]], axis=2)
    vw_pad = jnp.pad(vw, ((0, 0), (0, 0), (WINDOW, 2 * Q_BLOCK), (0, 0)))

    q, gates = _q_project(h3, norm_mix[1], b_w_in[0], b_b_gate[0], q_norm[0])
    attn = _nsa_attention(q, gates, kc_pad, vc_pad, ks_aug, vs, kw_pad, vw_pad, _attn_tables(rel_bias, s))
    h = _out_project(attn.reshape(n, -1), h, b_w_out[0])
    h = moe_ple(h, 1)
    return h.reshape(b, s, d)
```

```python
import functools
import math

import numpy as np
import jax
import jax.numpy as jnp
from jax import lax
from jax.experimental import pallas as pl
from jax.experimental.pallas import tpu as pltpu

D_MODEL = 1024
GMLP_CHUNK = 128
GMLP_GROUPS = 8
N_HEADS = 16
HEAD_DIM = 64
N_KV_GROUPS = 4
N_BRANCH = 3
CMP_BLOCK = 32
CMP_STRIDE = 16
CMP_HIDDEN = 256
SEL_BLOCK = 64
SEL_TOP = 16
WINDOW = 512
Q_BLOCK = 64
N_KV_SLOTS = 6
REL_BUCKETS = 32
REL_MAX_DIST = 2048
N_EXPERTS = 32
TOP_K = 4
SWIGLU_LIMIT = 7.0
SWIGLU_ALPHA = 1.702
NORM_EPS = 1e-6
NEG_INF = -1e30
FORCE = 1e30
LOG2E = math.log2(math.e)

LANES = 128
MOE_ROWS = 512
KEY_CHUNK = 512
Q_PAIR = 2
VMEM_LIMIT = 56 * 1024 * 1024

F32 = jnp.float32
BF16 = jnp.bfloat16


def _cparams(sem):
    return pltpu.CompilerParams(dimension_semantics=sem, vmem_limit_bytes=VMEM_LIMIT)


def _rms(x, g):
    return x * lax.rsqrt(jnp.mean(x * x, axis=-1, keepdims=True) + NORM_EPS) * g


def _dot(a, b):
    return jnp.dot(a, b, preferred_element_type=F32)


def _dot_nt(a, b):
    return lax.dot_general(a, b, (((1,), (1,)), ((), ())), preferred_element_type=F32)


def _dot_exact(a, b):
    return jnp.dot(a, b, preferred_element_type=F32, precision=lax.Precision.HIGHEST)


def _softmax2_rows(s):
    m = jnp.max(s, axis=-1, keepdims=True)
    p = jnp.exp2(s - m)
    return p / jnp.sum(p, axis=-1, keepdims=True)


def _normalize(acc):
    lane = lax.broadcasted_iota(jnp.int32, acc.shape, 1)
    denom = jnp.sum(jnp.where(lane == HEAD_DIM, acc, 0.0), axis=-1, keepdims=True)
    return jnp.where(lane < HEAD_DIM, acc / denom, 0.0)


def _head_lanes(x, idx):
    base = idx * HEAD_DIM // LANES * LANES
    y = x[:, base:base + LANES]
    return y if idx * HEAD_DIM == base else pltpu.roll(y, LANES - HEAD_DIM, 1)


def _low_lanes(x, fill):
    lane = lax.broadcasted_iota(jnp.int32, x.shape, 1)
    return jnp.where(lane < HEAD_DIM, x, fill)


def _argmax_first(x, ids, n):
    mx = jnp.max(x, axis=-1, keepdims=True)
    return mx, jnp.min(jnp.where(x == mx, ids, float(n)), axis=-1, keepdims=True)


def _gmlp_body(x_ref, nm_ref, win_ref, lng_ref, lnb_ref, ws_ref, bs_ref, wout_ref, o_ref):
    tm = x_ref.shape[0]
    gd = win_ref.shape[1] // 2
    gdim = gd // GMLP_GROUPS
    x = x_ref[...]
    xn = _rms(x, nm_ref[...]).astype(BF16)
    z = jax.nn.gelu(_dot(xn, win_ref[...]))
    u = z[:, :gd]
    v = z[:, gd:]
    mu = jnp.mean(v, axis=-1, keepdims=True)
    vc = v - mu
    vln = vc * lax.rsqrt(jnp.mean(vc * vc, axis=-1, keepdims=True) + NORM_EPS) * lng_ref[...] + lnb_ref[...]
    vb = vln.astype(BF16)
    row = lax.broadcasted_iota(jnp.int32, (GMLP_CHUNK, GMLP_CHUNK), 0)
    col = lax.broadcasted_iota(jnp.int32, (GMLP_CHUNK, GMLP_CHUNK), 1)
    causal = row >= col
    chunks = []
    for c in range(tm // GMLP_CHUNK):
        cols = []
        for g in range(GMLP_GROUPS):
            wsg = jnp.where(causal, ws_ref[g], 0.0).astype(BF16)
            vg = vb[c * GMLP_CHUNK:(c + 1) * GMLP_CHUNK, g * gdim:(g + 1) * gdim]
            cols.append(_dot(wsg, vg) + bs_ref[g])
        chunks.append(jnp.concatenate(cols, axis=1))
    mixed = jnp.concatenate(chunks, axis=0)
    gated = (u * mixed).astype(BF16)
    o_ref[...] = x + _dot(gated, wout_ref[...])


def _gmlp_layer(h, norm_g, w_in, ln_g, ln_b, w_s, b_s, w_out):
    n, d = h.shape
    gd2 = w_in.shape[1]
    gd = gd2 // 2
    tm = min(512, n)
    full = lambda *shape: pl.BlockSpec(shape, lambda i: (0,) * len(shape))
    return pl.pallas_call(
        _gmlp_body,
        grid=(n // tm,),
        in_specs=[pl.BlockSpec((tm, d), lambda i: (i, 0)),
                  full(1, d), full(d, gd2), full(1, gd), full(1, gd),
                  full(GMLP_GROUPS, GMLP_CHUNK, GMLP_CHUNK), full(GMLP_GROUPS, GMLP_CHUNK, 1),
                  full(gd, d)],
        out_specs=pl.BlockSpec((tm, d), lambda i: (i, 0)),
        out_shape=jax.ShapeDtypeStruct((n, d), F32),
        compiler_params=_cparams(("arbitrary",)),
        name="gmlp_layer",
    )(h, norm_g.reshape(1, d), w_in.astype(BF16), ln_g.reshape(1, gd), ln_b.reshape(1, gd),
      w_s, b_s.reshape(GMLP_GROUPS, GMLP_CHUNK, 1), w_out.astype(BF16))


def _route_body(h_ref, ng_ref, rw_ref, rb_ref, o_ref, cnt_ref, run_ref):
    i = pl.program_id(0)
    tm = h_ref.shape[0]

    @pl.when(i == 0)
    def _():
        run_ref[...] = jnp.zeros_like(run_ref)

    xn = _rms(h_ref[...], ng_ref[...])
    logits = _dot_exact(xn, rw_ref[...]) + rb_ref[...]
    eid = lax.broadcasted_iota(jnp.int32, logits.shape, 1).astype(F32)
    lane = lax.broadcasted_iota(jnp.int32, (tm, LANES), 1)
    work = logits
    vals, idxs = [], []
    for _ in range(TOP_K):
        mx, ix = _argmax_first(work, eid, N_EXPERTS)
        vals.append(mx)
        idxs.append(ix)
        work = jnp.where(eid == ix, -jnp.inf, work)
    exps = [jnp.exp(v - vals[0]) for v in vals]
    den = exps[0]
    for e in exps[1:]:
        den = den + e
    onehot = jnp.zeros(logits.shape, F32)
    for ix in idxs:
        onehot = onehot + (eid == ix).astype(F32)
    r = lax.broadcasted_iota(jnp.int32, (tm, tm), 0)
    c = lax.broadcasted_iota(jnp.int32, (tm, tm), 1)
    before = (r > c).astype(BF16)
    prefix = _dot(before, onehot.astype(BF16)) + run_ref[...]
    out = jnp.zeros((tm, LANES), F32)
    for k in range(TOP_K):
        rank = jnp.sum(jnp.where(eid == idxs[k], prefix, 0.0), axis=-1, keepdims=True)
        out = jnp.where(lane == k, idxs[k], out)
        out = jnp.where(lane == TOP_K + k, exps[k] / den, out)
        out = jnp.where(lane == 2 * TOP_K + k, rank, out)
    o_ref[...] = out
    run_ref[...] = run_ref[...] + jnp.sum(onehot, axis=0, keepdims=True)
    cnt_ref[...] = run_ref[...]


def _moe_route(h, norm_g, router_w, router_b):
    n, d = h.shape
    tm = min(512, n)
    full = lambda *shape: pl.BlockSpec(shape, lambda i: (0,) * len(shape))
    return pl.pallas_call(
        _route_body,
        grid=(n // tm,),
        in_specs=[pl.BlockSpec((tm, d), lambda i: (i, 0)), full(1, d), full(d, N_EXPERTS), full(1, N_EXPERTS)],
        out_specs=[pl.BlockSpec((tm, LANES), lambda i: (i, 0)), full(1, N_EXPERTS)],
        out_shape=[jax.ShapeDtypeStruct((n, LANES), F32), jax.ShapeDtypeStruct((1, N_EXPERTS), F32)],
        scratch_shapes=[pltpu.VMEM((1, N_EXPERTS), F32)],
        compiler_params=_cparams(("arbitrary",)),
        name="moe_route",
    )(h, norm_g.reshape(1, d), router_w, router_b.reshape(1, N_EXPERTS))


def _dispatch_body(dest_ref, h_ref, ng_ref, xs_in, xs_out, buf, sem):
    del xs_in
    tm = h_ref.shape[0]
    buf[...] = _rms(h_ref[...], ng_ref[...])

    def issue(j, carry):
        for k in range(TOP_K):
            r = dest_ref[0, 0, j * TOP_K + k]
            pltpu.make_async_copy(buf.at[pl.ds(j, 1), :], xs_out.at[pl.ds(r, 1), :], sem).start()
        return carry

    lax.fori_loop(0, tm, issue, 0)
    for _ in range(TOP_K):
        pltpu.make_async_copy(buf, xs_out.at[pl.ds(0, tm), :], sem).wait()


def _moe_dispatch(h, norm_g, dest, n_rows):
    n, d = h.shape
    tm = min(256, n)
    xs0 = jnp.zeros((n_rows, d), F32)
    return pl.pallas_call(
        _dispatch_body,
        grid=(n // tm,),
        in_specs=[pl.BlockSpec((1, 1, tm * TOP_K), lambda i: (i, 0, 0), memory_space=pltpu.SMEM),
                  pl.BlockSpec((tm, d), lambda i: (i, 0)),
                  pl.BlockSpec((1, d), lambda i: (0, 0)),
                  pl.BlockSpec(memory_space=pl.ANY)],
        out_specs=pl.BlockSpec(memory_space=pl.ANY),
        out_shape=jax.ShapeDtypeStruct((n_rows, d), F32),
        scratch_shapes=[pltpu.VMEM((tm, d), F32), pltpu.SemaphoreType.DMA(())],
        input_output_aliases={3: 0},
        compiler_params=_cparams(("arbitrary",)),
        name="moe_dispatch",
    )(dest.reshape(n // tm, 1, tm * TOP_K), h, norm_g.reshape(1, d), xs0)


def _expert_body(be_ref, nu_ref, xs_ref, wgu_ref, bgu_ref, wd_ref, bd_ref, ys_ref, wgu_bf, wd_bf):
    i = pl.program_id(0)
    ed = wd_ref.shape[2]

    @pl.when((i == 0) | (be_ref[i] != be_ref[jnp.maximum(i - 1, 0)]))
    def _():
        wgu_bf[...] = wgu_ref[0, 0].astype(BF16)
        wd_bf[...] = wd_ref[0, 0].astype(BF16)

    @pl.when(i < nu_ref[0])
    def _():
        x = xs_ref[...].astype(BF16)
        gu = _dot(x, wgu_bf[...]) + bgu_ref[0]
        gate = jnp.minimum(gu[:, :ed], SWIGLU_LIMIT)
        up = jnp.clip(gu[:, ed:], -SWIGLU_LIMIT, SWIGLU_LIMIT)
        glu = gate * jax.nn.sigmoid(gate * SWIGLU_ALPHA)
        ys_ref[...] = _dot(((up + 1.0) * glu).astype(BF16), wd_bf[...]) + bd_ref[0]

    @pl.when(i >= nu_ref[0])
    def _():
        ys_ref[...] = jnp.zeros_like(ys_ref)


def _moe_experts(xs, blk_expert, n_used, w_gu, b_gu, w_d, b_d, layer):
    n_rows, d = xs.shape
    ed = w_d.shape[2]
    n_blk = n_rows // MOE_ROWS
    return pl.pallas_call(
        _expert_body,
        grid_spec=pltpu.PrefetchScalarGridSpec(
            num_scalar_prefetch=2, grid=(n_blk,),
            in_specs=[pl.BlockSpec((MOE_ROWS, d), lambda i, be, nu: (jnp.maximum(jnp.minimum(i, nu[0] - 1), 0), 0)),
                      pl.BlockSpec((1, 1, d, 2 * ed), lambda i, be, nu: (layer, be[i], 0, 0)),
                      pl.BlockSpec((1, 1, 2 * ed), lambda i, be, nu: (be[i], 0, 0)),
                      pl.BlockSpec((1, 1, ed, d), lambda i, be, nu: (layer, be[i], 0, 0)),
                      pl.BlockSpec((1, 1, d), lambda i, be, nu: (be[i], 0, 0))],
            out_specs=pl.BlockSpec((MOE_ROWS, d), lambda i, be, nu: (i, 0)),
            scratch_shapes=[pltpu.VMEM((d, 2 * ed), BF16), pltpu.VMEM((ed, d), BF16)]),
        out_shape=jax.ShapeDtypeStruct((n_rows, d), F32),
        compiler_params=_cparams(("arbitrary",)),
        name="moe_experts",
    )(blk_expert, n_used, xs, w_gu, b_gu.reshape(N_EXPERTS, 1, 2 * ed), w_d, b_d.reshape(N_EXPERTS, 1, d))


def _combine_body(dest_ref, next_ref, rt_ref, h_ref, p_ref, pw_ref, pg_ref, pn_ref, ys_hbm, o_ref, buf, sem):
    i = pl.program_id(0)
    n = pl.num_programs(0)
    tm = h_ref.shape[0]
    slot = i % 2

    def row_copy(idx_ref, j, k, s):
        r = idx_ref[0, 0, j * TOP_K + k]
        return pltpu.make_async_copy(ys_hbm.at[pl.ds(r, 1), :], buf.at[s, k, pl.ds(j, 1), :], sem.at[s])

    def wait_tile(s):
        for k in range(TOP_K):
            pltpu.make_async_copy(ys_hbm.at[pl.ds(0, tm), :], buf.at[s, k], sem.at[s]).wait()

    @pl.when(i == 0)
    def _():
        def first(j, carry):
            for k in range(TOP_K):
                row_copy(dest_ref, j, k, 0).start()
            return carry
        lax.fori_loop(0, tm, first, 0)

    wait_tile(slot)
    for j in range(tm):
        for k in range(TOP_K):
            row_copy(next_ref, j, k, 1 - slot).start()
    rt = rt_ref[...]
    h = h_ref[...]
    for k in range(TOP_K):
        h = h + rt[:, TOP_K + k:TOP_K + k + 1] * buf[slot, k]
    emb = _dot(p_ref[...].astype(BF16), pw_ref[...])
    gate = jax.nn.sigmoid(_dot(_rms(h, pn_ref[...]).astype(BF16), pg_ref[...]))
    o_ref[...] = h + emb * gate

    @pl.when(i == n - 1)
    def _():
        wait_tile(1 - slot)


def _moe_combine_ple(h, route, dest, ys, p, ple_w, ple_gate_w, ple_norm):
    n, d = h.shape
    pd = p.shape[1]
    tm = min(256, n)
    nt = n // tm
    full = lambda *shape: pl.BlockSpec(shape, lambda i: (0,) * len(shape))
    dest3 = dest.reshape(nt, 1, tm * TOP_K)
    return pl.pallas_call(
        _combine_body,
        grid=(nt,),
        in_specs=[pl.BlockSpec((1, 1, tm * TOP_K), lambda i: (i, 0, 0), memory_space=pltpu.SMEM),
                  pl.BlockSpec((1, 1, tm * TOP_K), lambda i: (jnp.minimum(i + 1, nt - 1), 0, 0),
                               memory_space=pltpu.SMEM),
                  pl.BlockSpec((tm, LANES), lambda i: (i, 0)),
                  pl.BlockSpec((tm, d), lambda i: (i, 0)),
                  pl.BlockSpec((tm, pd), lambda i: (i, 0)),
                  full(pd, d), full(d, d), full(1, d),
                  pl.BlockSpec(memory_space=pl.ANY)],
        out_specs=pl.BlockSpec((tm, d), lambda i: (i, 0)),
        out_shape=jax.ShapeDtypeStruct((n, d), F32),
        scratch_shapes=[pltpu.VMEM((2, TOP_K, tm, d), F32), pltpu.SemaphoreType.DMA((2,))],
        compiler_params=_cparams(("arbitrary",)),
        name="moe_combine_ple",
    )(dest3, dest3, route, h, p, ple_w.astype(BF16), ple_gate_w.astype(BF16), ple_norm.reshape(1, d), ys)


def _moe_ple_layer(h, norm_g, router_w, router_b, w_gu, b_gu, w_d, b_d, layer, p, ple_w, ple_gate_w, ple_norm):
    n, _ = h.shape
    route, counts = _moe_route(h, norm_g, router_w, router_b)
    counts = counts[0].astype(jnp.int32)
    pad_counts = (counts + MOE_ROWS - 1) // MOE_ROWS * MOE_ROWS
    pad_ends = jnp.cumsum(pad_counts)
    pad_starts = pad_ends - pad_counts
    top_idx = route[:, :TOP_K].astype(jnp.int32)
    rank = route[:, 2 * TOP_K:3 * TOP_K].astype(jnp.int32)
    dest = (pad_starts[top_idx] + rank).reshape(-1)
    n_blk = -(-(n * TOP_K) // MOE_ROWS) + N_EXPERTS
    blk_start = jnp.arange(n_blk, dtype=jnp.int32) * MOE_ROWS
    blk_expert = jnp.minimum(jnp.sum((pad_ends[None, :] <= blk_start[:, None]).astype(jnp.int32), axis=1),
                             N_EXPERTS - 1)
    n_used = (pad_ends[-1:] // MOE_ROWS).astype(jnp.int32)
    xs = _moe_dispatch(h, norm_g, dest, n_blk * MOE_ROWS)
    ys = _moe_experts(xs, blk_expert, n_used, w_gu, b_gu, w_d, b_d, layer)
    return _moe_combine_ple(h, route, dest, ys, p, ple_w, ple_gate_w, ple_norm)


def _kv_body(h_ref, ng_ref, w_ref, seg_ref, kn_ref, kc_ref, vc_ref, ks_ref, vs_ref, kw_ref, vw_ref):
    ts = h_ref.shape[1]
    gw = N_KV_GROUPS * HEAD_DIM
    st = pl.program_id(1)
    hn = _rms(h_ref[0], ng_ref[...]).astype(BF16)
    kv = _dot(hn, w_ref[...])

    def knorm(x, j):
        ms = _dot_exact(x * x, seg_ref[...])
        return x * lax.rsqrt(ms + NORM_EPS) * kn_ref[j]

    k_c, v_c = kv[:, 0:gw], kv[:, gw:2 * gw]
    k_s, v_s = knorm(kv[:, 2 * gw:3 * gw], 1), kv[:, 3 * gw:4 * gw]
    k_w, v_w = knorm(kv[:, 4 * gw:5 * gw], 2), kv[:, 5 * gw:6 * gw]
    tok = st * ts + lax.broadcasted_iota(jnp.int32, (ts, LANES), 0)
    blk = lax.broadcasted_iota(jnp.int32, (ts, LANES), 1)
    onehot = (tok // SEL_BLOCK == blk).astype(BF16)
    one_col = (blk == HEAD_DIM).astype(F32)
    for g in range(N_KV_GROUPS):
        kc_ref[0, g] = _head_lanes(k_c, g)[:, :HEAD_DIM]
        vc_ref[0, g] = _head_lanes(v_c, g)[:, :HEAD_DIM]
        ks_ref[0, g] = jnp.concatenate([_low_lanes(_head_lanes(k_s, g), 0.0).astype(BF16), onehot], axis=1)
        vs_ref[0, g] = _low_lanes(_head_lanes(v_s, g), one_col).astype(BF16)
        kw_ref[0, g] = _low_lanes(_head_lanes(k_w, g), 0.0).astype(BF16)
        vw_ref[0, g] = _low_lanes(_head_lanes(v_w, g), one_col).astype(BF16)


def _kv_project(h3, kv_norm, kv_w, k_norm):
    b, s, d = h3.shape
    gw = N_KV_GROUPS * HEAD_DIM
    ts = min(512, s)
    seg = jnp.asarray(np.kron(np.eye(N_KV_GROUPS), np.full((HEAD_DIM, HEAD_DIM), 1.0 / HEAD_DIM)), F32)
    kn = jnp.tile(k_norm, (1, N_KV_GROUPS)).reshape(N_BRANCH, 1, gw)
    full = lambda *shape: pl.BlockSpec(shape, lambda bi, si: (0,) * len(shape))
    hd = lambda w: pl.BlockSpec((1, N_KV_GROUPS, ts, w), lambda bi, si: (bi, 0, si, 0))
    sds = lambda w, dt: jax.ShapeDtypeStruct((b, N_KV_GROUPS, s, w), dt)
    return pl.pallas_call(
        _kv_body,
        grid=(b, s // ts),
        in_specs=[pl.BlockSpec((1, ts, d), lambda bi, si: (bi, si, 0)), full(1, d), full(d, N_KV_SLOTS * gw),
                  full(gw, gw), full(N_BRANCH, 1, gw)],
        out_specs=[hd(HEAD_DIM), hd(HEAD_DIM), hd(2 * LANES), hd(LANES), hd(LANES), hd(LANES)],
        out_shape=[sds(HEAD_DIM, F32), sds(HEAD_DIM, F32), sds(2 * LANES, BF16), sds(LANES, BF16),
                   sds(LANES, BF16), sds(LANES, BF16)],
        compiler_params=_cparams(("arbitrary", "arbitrary")),
        name="kv_project",
    )(h3, kv_norm.reshape(1, d), kv_w.astype(BF16), seg, kn)


def _compress_body(kc_ref, vc_ref, pos_ref, w1_ref, b1_ref, w2_ref, kn_ref, ko_ref, vo_ref):
    nc = kc_ref.shape[2]
    half = kc_ref.shape[3]
    row = lax.broadcasted_iota(jnp.int32, (nc, 1), 0)
    valid = row < nc - 1

    def compress(c, j):
        a = _dot((c + pos_ref[j, 0]).astype(BF16), w1_ref[j, 0])
        bm = _dot((c + pos_ref[j, 1]).astype(BF16), w1_ref[j, 1])
        nxt = pltpu.roll(bm, nc - 1, 0)
        hid = jax.nn.gelu(a + nxt + b1_ref[j])
        return _dot(hid.astype(BF16), w2_ref[j])

    kraw = compress(kc_ref[0, 0], 0)
    ms = jnp.sum(kraw * kraw, axis=-1, keepdims=True) * (1.0 / HEAD_DIM)
    kcmp = kraw * lax.rsqrt(ms + NORM_EPS) * kn_ref[...]
    vcmp = jnp.where(valid, compress(vc_ref[0, 0], 1), 0.0)
    lane = lax.broadcasted_iota(jnp.int32, (nc, LANES), 1)
    flag = (lane == HEAD_DIM).astype(F32)
    ko_ref[0, 0, 0:nc, :] = flag
    ko_ref[0, 0, nc:2 * nc, :] = jnp.where(valid, kcmp, flag)
    vo_ref[0, 0, 0:nc, :] = jnp.zeros((nc, LANES), F32)
    vo_ref[0, 0, nc:2 * nc, :] = vcmp


def _compress(kc, vc, cmp_pos, cmp_w1, cmp_b1, cmp_w2, k_norm0):
    b, g, s, dh = kc.shape
    nc = s // CMP_STRIDE
    half = CMP_STRIDE * dh
    kc_r = kc.reshape(b, g, nc, half)
    vc_r = vc.reshape(b, g, nc, half)
    pos = cmp_pos.reshape(2, 2, 1, half)
    w1 = cmp_w1.reshape(2, 2, half, CMP_HIDDEN).astype(BF16)
    w2 = jnp.pad(cmp_w2, ((0, 0), (0, 0), (0, LANES - dh))).astype(BF16)
    kn = jnp.pad(k_norm0, (0, LANES - dh)).reshape(1, LANES)
    full = lambda *shape: pl.BlockSpec(shape, lambda bi, gi: (0,) * len(shape))
    blk = pl.BlockSpec((1, 1, nc, half), lambda bi, gi: (bi, gi, 0, 0))
    out = pl.BlockSpec((1, 1, 2 * nc, LANES), lambda bi, gi: (bi, gi, 0, 0))
    return pl.pallas_call(
        _compress_body,
        grid=(b, g),
        in_specs=[blk, blk, full(2, 2, 1, half), full(2, 2, half, CMP_HIDDEN), full(2, 1, CMP_HIDDEN),
                  full(2, CMP_HIDDEN, LANES), full(1, LANES)],
        out_specs=[out, out],
        out_shape=[jax.ShapeDtypeStruct((b, g, 2 * nc, LANES), F32)] * 2,
        compiler_params=_cparams(("arbitrary", "arbitrary")),
        name="kv_compress",
    )(kc_r, vc_r, pos, w1, cmp_b1.reshape(2, 1, CMP_HIDDEN), w2, kn)


def _qproj_body(h_ref, ng_ref, w_ref, bg_ref, ind_ref, indt_ref, qn_ref, q_ref, gate_ref):
    hd = N_HEADS * HEAD_DIM
    xn = _rms(h_ref[0], ng_ref[...]).astype(BF16)
    proj = _dot(xn, w_ref[...])
    q = proj[:, :hd]
    ms = _dot_exact(q * q, ind_ref[...]) * (1.0 / HEAD_DIM)
    scale = _dot_exact(lax.rsqrt(ms + NORM_EPS), indt_ref[...])
    qn = q * scale * qn_ref[...] * (HEAD_DIM ** -0.5 * LOG2E)
    lane = lax.broadcasted_iota(jnp.int32, (q.shape[0], LANES), 1)
    fill = jnp.where(lane == HEAD_DIM, NEG_INF, 0.0)
    for h in range(N_HEADS):
        q_ref[0, h] = _low_lanes(_head_lanes(qn, h), fill).astype(BF16)
    gate_ref[0] = jax.nn.sigmoid(proj[:, hd:] + bg_ref[...])


def _q_project(h3, norm_g, w_in, b_gate, q_norm):
    b, s, d = h3.shape
    hd = N_HEADS * HEAD_DIM
    ng = N_BRANCH * N_HEADS
    ts = min(512, s)
    w = jnp.pad(w_in, ((0, 0), (0, LANES - ng))).astype(BF16)
    bg = jnp.pad(b_gate, (0, LANES - ng)).reshape(1, LANES)
    ind = np.zeros((hd, LANES), np.float32)
    ind[np.arange(hd), np.arange(hd) // HEAD_DIM] = 1.0
    full = lambda *shape: pl.BlockSpec(shape, lambda bi, si: (0,) * len(shape))
    return pl.pallas_call(
        _qproj_body,
        grid=(b, s // ts),
        in_specs=[pl.BlockSpec((1, ts, d), lambda bi, si: (bi, si, 0)), full(1, d), full(d, hd + LANES),
                  full(1, LANES), full(hd, LANES), full(LANES, hd), full(1, hd)],
        out_specs=[pl.BlockSpec((1, N_HEADS, ts, LANES), lambda bi, si: (bi, 0, si, 0)),
                   pl.BlockSpec((1, ts, LANES), lambda bi, si: (bi, si, 0))],
        out_shape=[jax.ShapeDtypeStruct((b, N_HEADS, s, LANES), BF16),
                   jax.ShapeDtypeStruct((b, s, LANES), F32)],
        compiler_params=_cparams(("arbitrary", "arbitrary")),
        name="nsa_qproj",
    )(h3, norm_g.reshape(1, d), w, bg, jnp.asarray(ind), jnp.asarray(ind.T),
      jnp.tile(q_norm, N_HEADS).reshape(1, hd))


def _t5_bucket_np(dist):
    n = np.maximum(dist, 0)
    max_exact = REL_BUCKETS // 2
    nf = np.maximum(n, 1).astype(np.float64)
    large = max_exact + (np.log(nf / max_exact) / math.log(REL_MAX_DIST / max_exact)
                         * (REL_BUCKETS - max_exact)).astype(np.int64)
    return np.where(n < max_exact, n, np.minimum(large, REL_BUCKETS - 1))


def _bias_table(rel_bias, dist, valid):
    r = N_HEADS // N_KV_GROUPS
    tab = rel_bias.astype(F32).T.reshape(N_KV_GROUPS, r, REL_BUCKETS)
    onehot = jnp.asarray(_t5_bucket_np(dist)[..., None] == np.arange(REL_BUCKETS), F32)
    bias = jnp.einsum('xqln,grn->gxrql', onehot, tab, precision=lax.Precision.HIGHEST) * LOG2E
    return jnp.where(jnp.asarray(valid)[None, :, None, :, :], bias, NEG_INF)


def _n_delta(seq):
    d = np.arange(seq + SEL_BLOCK)
    bk = _t5_bucket_np(d)
    change = np.nonzero(bk[1:] != bk[:-1])[0]
    d_const = int(change[-1]) + 1 if change.size else 0
    return -(-(d_const + SEL_BLOCK - 1) // SEL_BLOCK) + 1


def _attn_tables(rel_bias, seq):
    r = N_HEADS // N_KV_GROUPS
    qi = np.arange(Q_BLOCK)[:, None]
    rows = lambda t: jnp.transpose(t, (0, 2, 1, 3, 4)).reshape(N_KV_GROUPS, r * Q_PAIR * Q_BLOCK, t.shape[-1])
    nc = seq // CMP_STRIDE
    j = np.arange(nc)[None, :]
    dist_c = np.stack([qi - (CMP_BLOCK - 1) - Q_BLOCK * (Q_PAIR - u) + CMP_STRIDE * (nc - j)
                       for u in range(Q_PAIR)])
    rc = rows(_bias_table(rel_bias, dist_c, dist_c >= 0))
    nd = _n_delta(seq)
    delta = np.arange(-1, nd + 1)[:, None, None]
    kj = np.arange(2 * SEL_BLOCK)[None, None, :]
    dist_s = SEL_BLOCK * (delta - kj // SEL_BLOCK) + qi[None] - kj % SEL_BLOCK
    bt = _bias_table(rel_bias, dist_s, dist_s >= 0)
    jw = np.arange(WINDOW + 4 * Q_BLOCK)[None, :]
    dist_w = np.stack([Q_BLOCK * u + qi - jw + WINDOW for u in range(Q_PAIR)])
    wb = rows(_bias_table(rel_bias, dist_w, (dist_w >= 0) & (dist_w < WINDOW)))
    n_sel = seq // SEL_BLOCK
    cs = np.arange(nc) * CMP_STRIDE
    ss = np.arange(n_sel) * SEL_BLOCK
    ov = np.clip(np.minimum(cs[:, None] + CMP_BLOCK, ss[None, :] + SEL_BLOCK)
                 - np.maximum(cs[:, None], ss[None, :]), 0, None) / CMP_BLOCK
    return rc, bt, wb, jnp.asarray(ov.T.astype(np.float32)).astype(BF16)


def _attn_body(q_ref, gate_ref, kc_ref, vc_ref, ks_ref, vs_ref, kw_ref, vw_ref, rc_ref, bt_ref, wb_ref, ovt_ref,
               o_ref, sa_ref, sb_ref):
    g = pl.program_id(1)
    i0 = Q_PAIR * pl.program_id(2)
    r = q_ref.shape[1]
    pq = Q_PAIR * Q_BLOCK
    rq = r * pq
    nc = rc_ref.shape[2]
    n_sel = ovt_ref.shape[0]
    nd = bt_ref.shape[1] - 2
    wl = wb_ref.shape[2]
    per = SEL_BLOCK // CMP_STRIDE

    heads = [slice(h * pq, (h + 1) * pq) for h in range(r)]
    q_pad = q_ref[0].reshape(rq, LANES)

    end = pl.multiple_of(per * (i0 + Q_PAIR), Q_PAIR * per)
    kcw = kc_ref[0, 0, pl.ds(end, nc), :].astype(BF16)
    vcw = vc_ref[0, 0, pl.ds(end, nc), :].astype(BF16)
    qpos = i0 * Q_BLOCK + lax.broadcasted_iota(jnp.int32, (rq, 1), 0) % pq
    p_c = _softmax2_rows(_dot_nt(q_pad, kcw) + rc_ref[0]) * (qpos >= CMP_BLOCK - 1).astype(F32)
    o_c = _dot(p_c.astype(BF16), vcw)
    p_sum = p_c[heads[0]]
    for h in range(1, r):
        p_sum = p_sum + p_c[heads[h]]

    p_hi = p_sum.astype(BF16)
    p_lo = (p_sum - p_hi.astype(F32)).astype(BF16)
    imp2 = _dot_nt(ovt_ref[...], jnp.concatenate([p_hi, p_lo], axis=0))
    imp = imp2[:, :pq] + imp2[:, pq:]

    ws = pl.multiple_of(i0 * Q_BLOCK, pq)
    sw = _dot_nt(q_pad, kw_ref[0, 0, pl.ds(ws, wl), :]) + wb_ref[0]
    p_w = jnp.exp2((sw - jnp.max(sw, axis=-1, keepdims=True)).astype(BF16))
    o_w = _normalize(_dot(p_w, vw_ref[0, 0, pl.ds(ws, wl), :]))

    shift = i0 + Q_PAIR
    blk_rel = lax.broadcasted_iota(jnp.int32, (n_sel, pq), 0)
    blk = blk_rel + shift - n_sel
    cur = i0 + lax.broadcasted_iota(jnp.int32, (n_sel, pq), 1) // Q_BLOCK
    forced = (blk == 0) | (blk == cur) | (blk == cur - 1)
    imp = jnp.where(forced, FORCE, jnp.where(blk > cur, NEG_INF, imp))
    imp = jnp.where(blk < 0, -jnp.inf, imp)
    ids = blk_rel.astype(F32)
    sel = jnp.zeros((n_sel, pq), F32)
    for _ in range(min(SEL_TOP, n_sel)):
        mx = jnp.max(imp, axis=0, keepdims=True)
        ix = jnp.min(jnp.where(imp == mx, ids, float(n_sel)), axis=0, keepdims=True)
        hit = ids == ix
        sel = jnp.where(hit, 1.0, sel)
        imp = jnp.where(hit, -jnp.inf, imp)
    unsel = jnp.where((sel > 0.0) & (blk >= 0), 0.0, NEG_INF)
    unsel = pltpu.roll(unsel.T, shift % n_sel, 1).astype(BF16)
    if n_sel < LANES:
        unsel = jnp.concatenate([unsel, jnp.zeros((pq, LANES - n_sel), BF16)], axis=1)

    q_aug = jnp.concatenate([q_pad, jnp.concatenate([unsel] * r, axis=0)], axis=1)
    kchunk = sa_ref.shape[1]
    n_chunks = ks_ref.shape[2] // kchunk
    cblocks = kchunk // SEL_BLOCK
    pairs = cblocks // 2

    def scores_to(dst, c):
        start = pl.multiple_of(jnp.minimum(c, n_chunks - 1) * kchunk, kchunk)
        sc = _dot_nt(q_aug, ks_ref[0, 0, pl.ds(start, kchunk), :])
        d0 = i0 - c * cblocks
        tiles = [[jnp.clip(d0 + u - 2 * pm, -1, nd) + 1 for pm in range(pairs)] for u in range(Q_PAIR)]
        dst[...] = sc + jnp.concatenate(
            [jnp.concatenate([bt_ref[0, t, h] for t in tiles[u]], axis=1)
             for h in range(r) for u in range(Q_PAIR)], axis=0)

    def absorb(src, c, m, acc):
        start = pl.multiple_of(c * kchunk, kchunk)
        sc = src[...]
        m_new = jnp.maximum(m, jnp.max(sc, axis=-1, keepdims=True))
        p = jnp.exp2((sc - m_new).astype(BF16))
        acc = jnp.exp2(m - m_new) * acc + _dot(p, vs_ref[0, 0, pl.ds(start, kchunk), :])
        return m_new, acc

    def two_chunks(j, carry):
        scores_to(sb_ref, 2 * j + 1)
        carry = absorb(sa_ref, 2 * j, *carry)
        scores_to(sa_ref, 2 * j + 2)
        return absorb(sb_ref, 2 * j + 1, *carry)

    scores_to(sa_ref, 0)
    init = (jnp.full((rq, 1), NEG_INF, F32), jnp.zeros((rq, LANES), F32))
    _, acc_s = lax.fori_loop(0, (i0 + Q_PAIR - 1) // (2 * cblocks) + 1, two_chunks, init)
    o_s = _normalize(acc_s)

    gates = gate_ref[0]
    glane = lax.broadcasted_iota(jnp.int32, gates.shape, 1)
    outs = []
    for h in range(r):
        head = g * r + h
        gs = [jnp.sum(jnp.where(glane == br * N_HEADS + head, gates, 0.0), axis=-1, keepdims=True)
              for br in range(N_BRANCH)]
        outs.append(gs[0] * o_c[heads[h]] + gs[1] * o_s[heads[h]] + gs[2] * o_w[heads[h]])
    o_ref[0] = jnp.concatenate(
        [outs[h] + pltpu.roll(outs[h + 1], HEAD_DIM, 1) for h in range(0, r, 2)], axis=1)


def _nsa_attention(q, gates, kc_pad, vc_pad, ks_aug, vs, kw_pad, vw_pad, tables):
    b, _, s, _ = q.shape
    r = N_HEADS // N_KV_GROUPS
    rc, bt, wb, ov = tables
    n_qb = s // Q_BLOCK
    per_bg = lambda a: pl.BlockSpec((1, 1) + a.shape[2:], lambda bi, gi, qi: (bi, gi, 0, 0))
    per_g = lambda a: pl.BlockSpec((1,) + a.shape[1:], lambda bi, gi, qi: (gi,) + (0,) * (a.ndim - 1))
    pq = Q_PAIR * Q_BLOCK
    return pl.pallas_call(
        _attn_body,
        grid=(b, N_KV_GROUPS, n_qb // Q_PAIR),
        in_specs=[pl.BlockSpec((1, r, pq, LANES), lambda bi, gi, qi: (bi, gi, qi, 0)),
                  pl.BlockSpec((1, pq, LANES), lambda bi, gi, qi: (bi, qi, 0)),
                  per_bg(kc_pad), per_bg(vc_pad), per_bg(ks_aug), per_bg(vs), per_bg(kw_pad), per_bg(vw_pad),
                  per_g(rc), per_g(bt), per_g(wb),
                  pl.BlockSpec(ov.shape, lambda bi, gi, qi: (0, 0))],
        out_specs=pl.BlockSpec((1, pq, r * HEAD_DIM), lambda bi, gi, qi: (bi, qi, gi)),
        out_shape=jax.ShapeDtypeStruct((b, s, N_HEADS * HEAD_DIM), F32),
        scratch_shapes=[pltpu.VMEM((r * pq, min(KEY_CHUNK, s // 2)), F32)] * 2,
        compiler_params=_cparams(("arbitrary", "arbitrary", "arbitrary")),
        name="nsa_attention",
    )(q, gates, kc_pad, vc_pad, ks_aug, vs, kw_pad, vw_pad, rc, bt, wb, ov)


def _outproj_body(a_ref, h_ref, w_ref, o_ref):
    o_ref[...] = h_ref[...] + _dot(a_ref[...].astype(BF16), w_ref[...])


def _out_project(attn, h, w_out):
    n, d = h.shape
    hd = attn.shape[1]
    tm = min(512, n)
    return pl.pallas_call(
        _outproj_body,
        grid=(n // tm,),
        in_specs=[pl.BlockSpec((tm, hd), lambda i: (i, 0)), pl.BlockSpec((tm, d), lambda i: (i, 0)),
                  pl.BlockSpec((hd, d), lambda i: (0, 0))],
        out_specs=pl.BlockSpec((tm, d), lambda i: (i, 0)),
        out_shape=jax.ShapeDtypeStruct((n, d), F32),
        compiler_params=_cparams(("arbitrary",)),
        name="nsa_outproj",
    )(attn, h, w_out.astype(BF16))


def kernel(x, p, rel_bias, norm_mix, norm_ffn, a_w_in, a_ln_g, a_ln_b, a_w_s, a_b_s, a_w_out, kv_norm, kv_w, cmp_pos, cmp_w1, cmp_b1, cmp_w2, k_norm, b_w_in, b_b_gate, q_norm, b_w_out, router_w, router_b, e_w_gu, e_b_gu, e_w_d, e_b_d, ple_w, ple_gate_w, ple_norm):
    b, s, d = x.shape
    n = b * s
    pf = p.reshape(p.shape[0], n, p.shape[-1])

    def moe_ple(h, i):
        return _moe_ple_layer(h, norm_ffn[i], router_w[i], router_b[i], e_w_gu, e_b_gu[i], e_w_d, e_b_d[i], i,
                              pf[i], ple_w[i], ple_gate_w[i], ple_norm[i])

    h = _gmlp_layer(x.reshape(n, d), norm_mix[0], a_w_in[0], a_ln_g[0], a_ln_b[0], a_w_s[0], a_b_s[0], a_w_out[0])
    h = moe_ple(h, 0)

    h3 = h.reshape(b, s, d)
    kc, vc, ks_aug, vs, kw, vw = _kv_project(h3, kv_norm, kv_w, k_norm)
    kc_pad, vc_pad = _compress(kc, vc, cmp_pos, cmp_w1, cmp_b1, cmp_w2, k_norm[0])
    front = jnp.zeros((b, N_KV_GROUPS, WINDOW, LANES), BF16).at[..., HEAD_DIM].set(1.0)
    kw_pad = jnp.concatenate([front, kw, front[:, :, :2 * Q_BLOCK]], axis=2)
    vw_pad = jnp.pad(vw, ((0, 0), (0, 0), (WINDOW, 2 * Q_BLOCK), (0, 0)))

    q, gates = _q_project(h3, norm_mix[1], b_w_in[0], b_b_gate[0], q_norm[0])
    attn = _nsa_attention(q, gates, kc_pad, vc_pad, ks_aug, vs, kw_pad, vw_pad, _attn_tables(rel_bias, s))
    h = _out_project(attn.reshape(n, -1), h, b_w_out[0])
    h = moe_ple(h, 1)
    return h.reshape(b, s, d)
```

```python
import functools
import math

import numpy as np
import jax
import jax.numpy as jnp
from jax import lax
from jax.experimental import pallas as pl
from jax.experimental.pallas import tpu as pltpu

D_MODEL = 1024
GMLP_CHUNK = 128
GMLP_GROUPS = 8
N_HEADS = 16
HEAD_DIM = 64
N_KV_GROUPS = 4
N_BRANCH = 3
CMP_BLOCK = 32
CMP_STRIDE = 16
CMP_HIDDEN = 256
SEL_BLOCK = 64
SEL_TOP = 16
WINDOW = 512
Q_BLOCK = 64
N_KV_SLOTS = 6
REL_BUCKETS = 32
REL_MAX_DIST = 2048
N_EXPERTS = 32
TOP_K = 4
SWIGLU_LIMIT = 7.0
SWIGLU_ALPHA = 1.702
NORM_EPS = 1e-6
NEG_INF = -1e30
FORCE = 1e30
LOG2E = math.log2(math.e)

LANES = 128
MOE_ROWS = 512
KEY_CHUNK = 512
Q_PAIR = 2
VMEM_LIMIT = 56 * 1024 * 1024

F32 = jnp.float32
BF16 = jnp.bfloat16


def _cparams(sem):
    return pltpu.CompilerParams(dimension_semantics=sem, vmem_limit_bytes=VMEM_LIMIT)


def _rms(x, g):
    return x * lax.rsqrt(jnp.mean(x * x, axis=-1, keepdims=True) + NORM_EPS) * g


def _dot(a, b):
    return jnp.dot(a, b, preferred_element_type=F32)


def _dot_nt(a, b):
    return lax.dot_general(a, b, (((1,), (1,)), ((), ())), preferred_element_type=F32)


def _dot_exact(a, b):
    return jnp.dot(a, b, preferred_element_type=F32, precision=lax.Precision.HIGHEST)


def _softmax2_rows(s):
    m = jnp.max(s, axis=-1, keepdims=True)
    p = jnp.exp2(s - m)
    return p / jnp.sum(p, axis=-1, keepdims=True)


def _normalize(acc):
    lane = lax.broadcasted_iota(jnp.int32, acc.shape, 1)
    denom = jnp.sum(jnp.where(lane == HEAD_DIM, acc, 0.0), axis=-1, keepdims=True)
    return jnp.where(lane < HEAD_DIM, acc / denom, 0.0)


def _head_lanes(x, idx):
    base = idx * HEAD_DIM // LANES * LANES
    y = x[:, base:base + LANES]
    return y if idx * HEAD_DIM == base else pltpu.roll(y, LANES - HEAD_DIM, 1)


def _low_lanes(x, fill):
    lane = lax.broadcasted_iota(jnp.int32, x.shape, 1)
    return jnp.where(lane < HEAD_DIM, x, fill)


def _argmax_first(x, ids, n):
    mx = jnp.max(x, axis=-1, keepdims=True)
    return mx, jnp.min(jnp.where(x == mx, ids, float(n)), axis=-1, keepdims=True)


def _gmlp_body(x_ref, nm_ref, win_ref, lng_ref, lnb_ref, ws_ref, bs_ref, wout_ref, o_ref):
    tm = x_ref.shape[0]
    gd = win_ref.shape[1] // 2
    gdim = gd // GMLP_GROUPS
    x = x_ref[...]
    xn = _rms(x, nm_ref[...]).astype(BF16)
    z = jax.nn.gelu(_dot(xn, win_ref[...]))
    u = z[:, :gd]
    v = z[:, gd:]
    mu = jnp.mean(v, axis=-1, keepdims=True)
    vc = v - mu
    vln = vc * lax.rsqrt(jnp.mean(vc * vc, axis=-1, keepdims=True) + NORM_EPS) * lng_ref[...] + lnb_ref[...]
    vb = vln.astype(BF16)
    row = lax.broadcasted_iota(jnp.int32, (GMLP_CHUNK, GMLP_CHUNK), 0)
    col = lax.broadcasted_iota(jnp.int32, (GMLP_CHUNK, GMLP_CHUNK), 1)
    causal = row >= col
    chunks = []
    for c in range(tm // GMLP_CHUNK):
        cols = []
        for g in range(GMLP_GROUPS):
            wsg = jnp.where(causal, ws_ref[g], 0.0).astype(BF16)
            vg = vb[c * GMLP_CHUNK:(c + 1) * GMLP_CHUNK, g * gdim:(g + 1) * gdim]
            cols.append(_dot(wsg, vg) + bs_ref[g])
        chunks.append(jnp.concatenate(cols, axis=1))
    mixed = jnp.concatenate(chunks, axis=0)
    gated = (u * mixed).astype(BF16)
    o_ref[...] = x + _dot(gated, wout_ref[...])


def _gmlp_layer(h, norm_g, w_in, ln_g, ln_b, w_s, b_s, w_out):
    n, d = h.shape
    gd2 = w_in.shape[1]
    gd = gd2 // 2
    tm = min(512, n)
    full = lambda *shape: pl.BlockSpec(shape, lambda i: (0,) * len(shape))
    return pl.pallas_call(
        _gmlp_body,
        grid=(n // tm,),
        in_specs=[pl.BlockSpec((tm, d), lambda i: (i, 0)),
                  full(1, d), full(d, gd2), full(1, gd), full(1, gd),
                  full(GMLP_GROUPS, GMLP_CHUNK, GMLP_CHUNK), full(GMLP_GROUPS, GMLP_CHUNK, 1),
                  full(gd, d)],
        out_specs=pl.BlockSpec((tm, d), lambda i: (i, 0)),
        out_shape=jax.ShapeDtypeStruct((n, d), F32),
        compiler_params=_cparams(("arbitrary",)),
        name="gmlp_layer",
    )(h, norm_g.reshape(1, d), w_in.astype(BF16), ln_g.reshape(1, gd), ln_b.reshape(1, gd),
      w_s, b_s.reshape(GMLP_GROUPS, GMLP_CHUNK, 1), w_out.astype(BF16))


def _route_body(h_ref, ng_ref, rw_ref, rb_ref, o_ref, cnt_ref, run_ref):
    i = pl.program_id(0)
    tm = h_ref.shape[0]

    @pl.when(i == 0)
    def _():
        run_ref[...] = jnp.zeros_like(run_ref)

    xn = _rms(h_ref[...], ng_ref[...])
    logits = _dot_exact(xn, rw_ref[...]) + rb_ref[...]
    eid = lax.broadcasted_iota(jnp.int32, logits.shape, 1).astype(F32)
    lane = lax.broadcasted_iota(jnp.int32, (tm, LANES), 1)
    work = logits
    vals, idxs = [], []
    for _ in range(TOP_K):
        mx, ix = _argmax_first(work, eid, N_EXPERTS)
        vals.append(mx)
        idxs.append(ix)
        work = jnp.where(eid == ix, -jnp.inf, work)
    exps = [jnp.exp(v - vals[0]) for v in vals]
    den = exps[0]
    for e in exps[1:]:
        den = den + e
    onehot = jnp.zeros(logits.shape, F32)
    for ix in idxs:
        onehot = onehot + (eid == ix).astype(F32)
    r = lax.broadcasted_iota(jnp.int32, (tm, tm), 0)
    c = lax.broadcasted_iota(jnp.int32, (tm, tm), 1)
    before = (r > c).astype(BF16)
    prefix = _dot(before, onehot.astype(BF16)) + run_ref[...]
    out = jnp.zeros((tm, LANES), F32)
    for k in range(TOP_K):
        rank = jnp.sum(jnp.where(eid == idxs[k], prefix, 0.0), axis=-1, keepdims=True)
        out = jnp.where(lane == k, idxs[k], out)
        out = jnp.where(lane == TOP_K + k, exps[k] / den, out)
        out = jnp.where(lane == 2 * TOP_K + k, rank, out)
    o_ref[...] = out
    run_ref[...] = run_ref[...] + jnp.sum(onehot, axis=0, keepdims=True)
    cnt_ref[...] = run_ref[...]


def _moe_route(h, norm_g, router_w, router_b):
    n, d = h.shape
    tm = min(512, n)
    full = lambda *shape: pl.BlockSpec(shape, lambda i: (0,) * len(shape))
    return pl.pallas_call(
        _route_body,
        grid=(n // tm,),
        in_specs=[pl.BlockSpec((tm, d), lambda i: (i, 0)), full(1, d), full(d, N_EXPERTS), full(1, N_EXPERTS)],
        out_specs=[pl.BlockSpec((tm, LANES), lambda i: (i, 0)), full(1, N_EXPERTS)],
        out_shape=[jax.ShapeDtypeStruct((n, LANES), F32), jax.ShapeDtypeStruct((1, N_EXPERTS), F32)],
        scratch_shapes=[pltpu.VMEM((1, N_EXPERTS), F32)],
        compiler_params=_cparams(("arbitrary",)),
        name="moe_route",
    )(h, norm_g.reshape(1, d), router_w, router_b.reshape(1, N_EXPERTS))


def _expert_body(be_ref, tok_ref, nxt_ref, prv_ref, cur_ref, h_hbm, ng_ref, wgu_ref, bgu_ref, wd_ref, bd_ref,
                 out_hbm, xbuf, ybuf, gsem, ssem, wgu_bf, wd_bf):
    j = pl.program_id(0)
    n = pl.num_programs(0)
    slot = j % 2
    other = 1 - slot
    ed = wd_ref.shape[2]

    def gather(idx_ref, r, s):
        return pltpu.make_async_copy(h_hbm.at[pl.ds(idx_ref[0, 0, r], 1), :], xbuf.at[s, pl.ds(r, 1), :],
                                     gsem.at[s])

    def scatter(idx_ref, r, s):
        return pltpu.make_async_copy(ybuf.at[s, pl.ds(r, 1), :], out_hbm.at[pl.ds(idx_ref[0, 0, r], 1), :],
                                     ssem.at[s])

    def wait_gather(s):
        pltpu.make_async_copy(h_hbm.at[pl.ds(0, MOE_ROWS), :], xbuf.at[s], gsem.at[s]).wait()

    def wait_scatter(s):
        pltpu.make_async_copy(ybuf.at[s], out_hbm.at[pl.ds(0, MOE_ROWS), :], ssem.at[s]).wait()

    @pl.when(j == 0)
    def _():
        ybuf[1] = jnp.zeros(ybuf.shape[1:], F32)

        def first(r, carry):
            gather(tok_ref, r, 0).start()
            return carry
        lax.fori_loop(0, MOE_ROWS, first, 0)

    @pl.when((j == 0) | (be_ref[j] != be_ref[jnp.maximum(j - 1, 0)]))
    def _():
        wgu_bf[...] = wgu_ref[0, 0].astype(BF16)
        wd_bf[...] = wd_ref[0, 0].astype(BF16)

    def step(s):
        o = 1 - s
        wait_gather(s)

        @pl.when(j >= 1)
        def _():
            wait_scatter(s)

        x = _rms(xbuf[s], ng_ref[...]).astype(BF16)
        for r in range(MOE_ROWS):
            scatter(prv_ref, r, o).start()
        for r in range(MOE_ROWS):
            gather(nxt_ref, r, o).start()
        gu = _dot(x, wgu_bf[...]) + bgu_ref[0]
        gate = jnp.minimum(gu[:, :ed], SWIGLU_LIMIT)
        up = jnp.clip(gu[:, ed:], -SWIGLU_LIMIT, SWIGLU_LIMIT)
        glu = gate * jax.nn.sigmoid(gate * SWIGLU_ALPHA)
        ybuf[s] = _dot(((up + 1.0) * glu).astype(BF16), wd_bf[...]) + bd_ref[0]

        @pl.when(j == n - 1)
        def _():
            wait_gather(o)
            wait_scatter(o)

            def last(r, carry):
                scatter(cur_ref, r, s).start()
                return carry
            lax.fori_loop(0, MOE_ROWS, last, 0)
            wait_scatter(s)

    for s in range(2):
        pl.when(slot == s)(functools.partial(step, s))


def _moe_experts(h, norm_g, row_tok, row_dst, n_out, blk_expert, w_gu, b_gu, w_d, b_d, layer):
    _, d = h.shape
    ed = w_d.shape[2]
    n_blk = row_tok.shape[0] // MOE_ROWS
    tok3 = row_tok.reshape(n_blk, 1, MOE_ROWS)
    dst3 = row_dst.reshape(n_blk, 1, MOE_ROWS)
    rows = lambda f: pl.BlockSpec((1, 1, MOE_ROWS), lambda i, be: (f(i), 0, 0), memory_space=pltpu.SMEM)
    return pl.pallas_call(
        _expert_body,
        grid_spec=pltpu.PrefetchScalarGridSpec(
            num_scalar_prefetch=1, grid=(n_blk,),
            in_specs=[rows(lambda i: i), rows(lambda i: jnp.minimum(i + 1, n_blk - 1)),
                      rows(lambda i: jnp.maximum(i - 1, 0)), rows(lambda i: i),
                      pl.BlockSpec(memory_space=pl.ANY),
                      pl.BlockSpec((1, d), lambda i, be: (0, 0)),
                      pl.BlockSpec((1, 1, d, 2 * ed), lambda i, be: (layer, be[i], 0, 0)),
                      pl.BlockSpec((1, 1, 2 * ed), lambda i, be: (be[i], 0, 0)),
                      pl.BlockSpec((1, 1, ed, d), lambda i, be: (layer, be[i], 0, 0)),
                      pl.BlockSpec((1, 1, d), lambda i, be: (be[i], 0, 0))],
            out_specs=pl.BlockSpec(memory_space=pl.ANY),
            scratch_shapes=[pltpu.VMEM((2, MOE_ROWS, d), F32), pltpu.VMEM((2, MOE_ROWS, d), F32),
                            pltpu.SemaphoreType.DMA((2,)), pltpu.SemaphoreType.DMA((2,)),
                            pltpu.VMEM((d, 2 * ed), BF16), pltpu.VMEM((ed, d), BF16)]),
        out_shape=jax.ShapeDtypeStruct((n_out, d), F32),
        compiler_params=_cparams(("arbitrary",)),
        name="moe_experts",
    )(blk_expert, tok3, tok3, dst3, dst3, h, norm_g.reshape(1, d), w_gu, b_gu.reshape(N_EXPERTS, 1, 2 * ed),
      w_d, b_d.reshape(N_EXPERTS, 1, d))


def _combine_body(rt_ref, h_ref, p_ref, y0_ref, y1_ref, y2_ref, y3_ref, pw_ref, pg_ref, pn_ref, o_ref):
    rt = rt_ref[...]
    h = h_ref[...]
    for k, y_ref in enumerate((y0_ref, y1_ref, y2_ref, y3_ref)):
        h = h + rt[:, TOP_K + k:TOP_K + k + 1] * y_ref[...]
    emb = _dot(p_ref[...].astype(BF16), pw_ref[...])
    gate = jax.nn.sigmoid(_dot(_rms(h, pn_ref[...]).astype(BF16), pg_ref[...]))
    o_ref[...] = h + emb * gate


def _moe_combine_ple(h, route, ys, p, ple_w, ple_gate_w, ple_norm):
    n, d = h.shape
    pd = p.shape[1]
    tm = min(256, n)
    nt = n // tm
    full = lambda *shape: pl.BlockSpec(shape, lambda i: (0,) * len(shape))
    tile = lambda w: pl.BlockSpec((tm, w), lambda i: (i, 0))
    return pl.pallas_call(
        _combine_body,
        grid=(nt,),
        in_specs=[tile(LANES), tile(d), tile(pd)]
                 + [pl.BlockSpec((tm, d), lambda i, k=k: (k * nt + i, 0)) for k in range(TOP_K)]
                 + [full(pd, d), full(d, d), full(1, d)],
        out_specs=tile(d),
        out_shape=jax.ShapeDtypeStruct((n, d), F32),
        compiler_params=_cparams(("arbitrary",)),
        name="moe_combine_ple",
    )(route, h, p, ys, ys, ys, ys, ple_w.astype(BF16), ple_gate_w.astype(BF16), ple_norm.reshape(1, d))


def _moe_ple_layer(h, norm_g, router_w, router_b, w_gu, b_gu, w_d, b_d, layer, p, ple_w, ple_gate_w, ple_norm):
    n, _ = h.shape
    route, counts = _moe_route(h, norm_g, router_w, router_b)
    counts = counts[0].astype(jnp.int32)
    pad_counts = (counts + MOE_ROWS - 1) // MOE_ROWS * MOE_ROWS
    pad_ends = jnp.cumsum(pad_counts)
    pad_starts = pad_ends - pad_counts
    top_idx = route[:, :TOP_K].astype(jnp.int32)
    rank = route[:, 2 * TOP_K:3 * TOP_K].astype(jnp.int32)
    dest = (pad_starts[top_idx] + rank).reshape(-1)
    n_blk = -(-(n * TOP_K) // MOE_ROWS) + N_EXPERTS
    blk_start = jnp.arange(n_blk, dtype=jnp.int32) * MOE_ROWS
    blk_expert = jnp.minimum(jnp.sum((pad_ends[None, :] <= blk_start[:, None]).astype(jnp.int32), axis=1),
                             N_EXPERTS - 1)
    n_rows = n_blk * MOE_ROWS
    n_asg = n * TOP_K
    row_asg = jnp.full((n_rows,), -1, jnp.int32).at[dest].set(jnp.arange(n_asg, dtype=jnp.int32),
                                                              unique_indices=True)
    r = jnp.arange(n_rows, dtype=jnp.int32)
    spare = n_asg + (r // MOE_ROWS) % 2 * MOE_ROWS + r % MOE_ROWS
    row_tok = jnp.where(row_asg >= 0, row_asg // TOP_K, 0)
    row_dst = jnp.where(row_asg >= 0, row_asg % TOP_K * n + row_asg // TOP_K, spare)
    ys = _moe_experts(h, norm_g, row_tok, row_dst, n_asg + 2 * MOE_ROWS, blk_expert, w_gu, b_gu, w_d, b_d, layer)
    return _moe_combine_ple(h, route, ys, p, ple_w, ple_gate_w, ple_norm)


def _kv_body(h_ref, ng_ref, w_ref, seg_ref, kn_ref, kc_ref, vc_ref, ks_ref, vs_ref, kw_ref, vw_ref):
    ts = h_ref.shape[1]
    gw = N_KV_GROUPS * HEAD_DIM
    st = pl.program_id(1)
    hn = _rms(h_ref[0], ng_ref[...]).astype(BF16)
    kv = _dot(hn, w_ref[...])

    def knorm(x, j):
        ms = _dot_exact(x * x, seg_ref[...])
        return x * lax.rsqrt(ms + NORM_EPS) * kn_ref[j]

    k_c, v_c = kv[:, 0:gw], kv[:, gw:2 * gw]
    k_s, v_s = knorm(kv[:, 2 * gw:3 * gw], 1), kv[:, 3 * gw:4 * gw]
    k_w, v_w = knorm(kv[:, 4 * gw:5 * gw], 2), kv[:, 5 * gw:6 * gw]
    tok = st * ts + lax.broadcasted_iota(jnp.int32, (ts, LANES), 0)
    blk = lax.broadcasted_iota(jnp.int32, (ts, LANES), 1)
    onehot = (tok // SEL_BLOCK == blk).astype(BF16)
    one_col = (blk == HEAD_DIM).astype(F32)
    for g in range(N_KV_GROUPS):
        kc_ref[0, g] = _head_lanes(k_c, g)[:, :HEAD_DIM]
        vc_ref[0, g] = _head_lanes(v_c, g)[:, :HEAD_DIM]
        ks_ref[0, g] = jnp.concatenate([_low_lanes(_head_lanes(k_s, g), 0.0).astype(BF16), onehot], axis=1)
        vs_ref[0, g] = _low_lanes(_head_lanes(v_s, g), one_col).astype(BF16)
        kw_ref[0, g] = _low_lanes(_head_lanes(k_w, g), 0.0).astype(BF16)
        vw_ref[0, g] = _low_lanes(_head_lanes(v_w, g), one_col).astype(BF16)


def _kv_project(h3, kv_norm, kv_w, k_norm):
    b, s, d = h3.shape
    gw = N_KV_GROUPS * HEAD_DIM
    ts = min(512, s)
    seg = jnp.asarray(np.kron(np.eye(N_KV_GROUPS), np.full((HEAD_DIM, HEAD_DIM), 1.0 / HEAD_DIM)), F32)
    kn = jnp.tile(k_norm, (1, N_KV_GROUPS)).reshape(N_BRANCH, 1, gw)
    full = lambda *shape: pl.BlockSpec(shape, lambda bi, si: (0,) * len(shape))
    hd = lambda w: pl.BlockSpec((1, N_KV_GROUPS, ts, w), lambda bi, si: (bi, 0, si, 0))
    sds = lambda w, dt: jax.ShapeDtypeStruct((b, N_KV_GROUPS, s, w), dt)
    return pl.pallas_call(
        _kv_body,
        grid=(b, s // ts),
        in_specs=[pl.BlockSpec((1, ts, d), lambda bi, si: (bi, si, 0)), full(1, d), full(d, N_KV_SLOTS * gw),
                  full(gw, gw), full(N_BRANCH, 1, gw)],
        out_specs=[hd(HEAD_DIM), hd(HEAD_DIM), hd(2 * LANES), hd(LANES), hd(LANES), hd(LANES)],
        out_shape=[sds(HEAD_DIM, F32), sds(HEAD_DIM, F32), sds(2 * LANES, BF16), sds(LANES, BF16),
                   sds(LANES, BF16), sds(LANES, BF16)],
        compiler_params=_cparams(("arbitrary", "arbitrary")),
        name="kv_project",
    )(h3, kv_norm.reshape(1, d), kv_w.astype(BF16), seg, kn)


def _compress_body(kc_ref, vc_ref, pos_ref, w1_ref, b1_ref, w2_ref, kn_ref, ko_ref, vo_ref):
    nc = kc_ref.shape[2]
    half = kc_ref.shape[3]
    row = lax.broadcasted_iota(jnp.int32, (nc, 1), 0)
    valid = row < nc - 1

    def compress(c, j):
        a = _dot((c + pos_ref[j, 0]).astype(BF16), w1_ref[j, 0])
        bm = _dot((c + pos_ref[j, 1]).astype(BF16), w1_ref[j, 1])
        nxt = pltpu.roll(bm, nc - 1, 0)
        hid = jax.nn.gelu(a + nxt + b1_ref[j])
        return _dot(hid.astype(BF16), w2_ref[j])

    kraw = compress(kc_ref[0, 0], 0)
    ms = jnp.sum(kraw * kraw, axis=-1, keepdims=True) * (1.0 / HEAD_DIM)
    kcmp = kraw * lax.rsqrt(ms + NORM_EPS) * kn_ref[...]
    vcmp = jnp.where(valid, compress(vc_ref[0, 0], 1), 0.0)
    lane = lax.broadcasted_iota(jnp.int32, (nc, LANES), 1)
    flag = (lane == HEAD_DIM).astype(F32)
    ko_ref[0, 0, 0:nc, :] = flag
    ko_ref[0, 0, nc:2 * nc, :] = jnp.where(valid, kcmp, flag)
    vo_ref[0, 0, 0:nc, :] = jnp.zeros((nc, LANES), F32)
    vo_ref[0, 0, nc:2 * nc, :] = vcmp


def _compress(kc, vc, cmp_pos, cmp_w1, cmp_b1, cmp_w2, k_norm0):
    b, g, s, dh = kc.shape
    nc = s // CMP_STRIDE
    half = CMP_STRIDE * dh
    kc_r = kc.reshape(b, g, nc, half)
    vc_r = vc.reshape(b, g, nc, half)
    pos = cmp_pos.reshape(2, 2, 1, half)
    w1 = cmp_w1.reshape(2, 2, half, CMP_HIDDEN).astype(BF16)
    w2 = jnp.pad(cmp_w2, ((0, 0), (0, 0), (0, LANES - dh))).astype(BF16)
    kn = jnp.pad(k_norm0, (0, LANES - dh)).reshape(1, LANES)
    full = lambda *shape: pl.BlockSpec(shape, lambda bi, gi: (0,) * len(shape))
    blk = pl.BlockSpec((1, 1, nc, half), lambda bi, gi: (bi, gi, 0, 0))
    out = pl.BlockSpec((1, 1, 2 * nc, LANES), lambda bi, gi: (bi, gi, 0, 0))
    return pl.pallas_call(
        _compress_body,
        grid=(b, g),
        in_specs=[blk, blk, full(2, 2, 1, half), full(2, 2, half, CMP_HIDDEN), full(2, 1, CMP_HIDDEN),
                  full(2, CMP_HIDDEN, LANES), full(1, LANES)],
        out_specs=[out, out],
        out_shape=[jax.ShapeDtypeStruct((b, g, 2 * nc, LANES), F32)] * 2,
        compiler_params=_cparams(("arbitrary", "arbitrary")),
        name="kv_compress",
    )(kc_r, vc_r, pos, w1, cmp_b1.reshape(2, 1, CMP_HIDDEN), w2, kn)


def _qproj_body(h_ref, ng_ref, w_ref, bg_ref, ind_ref, indt_ref, qn_ref, q_ref, gate_ref):
    hd = N_HEADS * HEAD_DIM
    xn = _rms(h_ref[0], ng_ref[...]).astype(BF16)
    proj = _dot(xn, w_ref[...])
    q = proj[:, :hd]
    ms = _dot_exact(q * q, ind_ref[...]) * (1.0 / HEAD_DIM)
    scale = _dot_exact(lax.rsqrt(ms + NORM_EPS), indt_ref[...])
    qn = q * scale * qn_ref[...] * (HEAD_DIM ** -0.5 * LOG2E)
    lane = lax.broadcasted_iota(jnp.int32, (q.shape[0], LANES), 1)
    fill = jnp.where(lane == HEAD_DIM, NEG_INF, 0.0)
    for h in range(N_HEADS):
        q_ref[0, h] = _low_lanes(_head_lanes(qn, h), fill).astype(BF16)
    gate_ref[0] = jax.nn.sigmoid(proj[:, hd:] + bg_ref[...])


def _q_project(h3, norm_g, w_in, b_gate, q_norm):
    b, s, d = h3.shape
    hd = N_HEADS * HEAD_DIM
    ng = N_BRANCH * N_HEADS
    ts = min(512, s)
    w = jnp.pad(w_in, ((0, 0), (0, LANES - ng))).astype(BF16)
    bg = jnp.pad(b_gate, (0, LANES - ng)).reshape(1, LANES)
    ind = np.zeros((hd, LANES), np.float32)
    ind[np.arange(hd), np.arange(hd) // HEAD_DIM] = 1.0
    full = lambda *shape: pl.BlockSpec(shape, lambda bi, si: (0,) * len(shape))
    return pl.pallas_call(
        _qproj_body,
        grid=(b, s // ts),
        in_specs=[pl.BlockSpec((1, ts, d), lambda bi, si: (bi, si, 0)), full(1, d), full(d, hd + LANES),
                  full(1, LANES), full(hd, LANES), full(LANES, hd), full(1, hd)],
        out_specs=[pl.BlockSpec((1, N_HEADS, ts, LANES), lambda bi, si: (bi, 0, si, 0)),
                   pl.BlockSpec((1, ts, LANES), lambda bi, si: (bi, si, 0))],
        out_shape=[jax.ShapeDtypeStruct((b, N_HEADS, s, LANES), BF16),
                   jax.ShapeDtypeStruct((b, s, LANES), F32)],
        compiler_params=_cparams(("arbitrary", "arbitrary")),
        name="nsa_qproj",
    )(h3, norm_g.reshape(1, d), w, bg, jnp.asarray(ind), jnp.asarray(ind.T),
      jnp.tile(q_norm, N_HEADS).reshape(1, hd))


def _t5_bucket_np(dist):
    n = np.maximum(dist, 0)
    max_exact = REL_BUCKETS // 2
    nf = np.maximum(n, 1).astype(np.float64)
    large = max_exact + (np.log(nf / max_exact) / math.log(REL_MAX_DIST / max_exact)
                         * (REL_BUCKETS - max_exact)).astype(np.int64)
    return np.where(n < max_exact, n, np.minimum(large, REL_BUCKETS - 1))


def _bias_table(rel_bias, dist, valid):
    r = N_HEADS // N_KV_GROUPS
    tab = rel_bias.astype(F32).T.reshape(N_KV_GROUPS, r, REL_BUCKETS)
    onehot = jnp.asarray(_t5_bucket_np(dist)[..., None] == np.arange(REL_BUCKETS), F32)
    bias = jnp.einsum('xqln,grn->gxrql', onehot, tab, precision=lax.Precision.HIGHEST) * LOG2E
    return jnp.where(jnp.asarray(valid)[None, :, None, :, :], bias, NEG_INF)


def _n_delta(seq):
    d = np.arange(seq + SEL_BLOCK)
    bk = _t5_bucket_np(d)
    change = np.nonzero(bk[1:] != bk[:-1])[0]
    d_const = int(change[-1]) + 1 if change.size else 0
    return -(-(d_const + SEL_BLOCK - 1) // SEL_BLOCK) + 1


def _attn_tables(rel_bias, seq):
    r = N_HEADS // N_KV_GROUPS
    qi = np.arange(Q_BLOCK)[:, None]
    rows = lambda t: jnp.transpose(t, (0, 2, 1, 3, 4)).reshape(N_KV_GROUPS, r * Q_PAIR * Q_BLOCK, t.shape[-1])
    nc = seq // CMP_STRIDE
    j = np.arange(nc)[None, :]
    dist_c = np.stack([qi - (CMP_BLOCK - 1) - Q_BLOCK * (Q_PAIR - u) + CMP_STRIDE * (nc - j)
                       for u in range(Q_PAIR)])
    rc = rows(_bias_table(rel_bias, dist_c, dist_c >= 0))
    nd = _n_delta(seq)
    delta = np.arange(-1, nd + 1)[:, None, None]
    kj = np.arange(2 * SEL_BLOCK)[None, None, :]
    dist_s = SEL_BLOCK * (delta - kj // SEL_BLOCK) + qi[None] - kj % SEL_BLOCK
    bt = _bias_table(rel_bias, dist_s, dist_s >= 0)
    jw = np.arange(WINDOW + 4 * Q_BLOCK)[None, :]
    dist_w = np.stack([Q_BLOCK * u + qi - jw + WINDOW for u in range(Q_PAIR)])
    wb = rows(_bias_table(rel_bias, dist_w, (dist_w >= 0) & (dist_w < WINDOW)))
    n_sel = seq // SEL_BLOCK
    cs = np.arange(nc) * CMP_STRIDE
    ss = np.arange(n_sel) * SEL_BLOCK
    ov = np.clip(np.minimum(cs[:, None] + CMP_BLOCK, ss[None, :] + SEL_BLOCK)
                 - np.maximum(cs[:, None], ss[None, :]), 0, None) / CMP_BLOCK
    return rc, bt, wb, jnp.asarray(ov.T.astype(np.float32)).astype(BF16)


def _attn_body(q_ref, gate_ref, kc_ref, vc_ref, ks_ref, vs_ref, kw_ref, vw_ref, rc_ref, bt_ref, wb_ref, ovt_ref,
               o_ref, sa_ref, sb_ref):
    g = pl.program_id(1)
    i0 = Q_PAIR * pl.program_id(2)
    r = q_ref.shape[1]
    pq = Q_PAIR * Q_BLOCK
    rq = r * pq
    nc = rc_ref.shape[2]
    n_sel = ovt_ref.shape[0]
    nd = bt_ref.shape[1] - 2
    wl = wb_ref.shape[2]
    per = SEL_BLOCK // CMP_STRIDE

    heads = [slice(h * pq, (h + 1) * pq) for h in range(r)]
    q_pad = q_ref[0].reshape(rq, LANES)

    end = pl.multiple_of(per * (i0 + Q_PAIR), Q_PAIR * per)
    kcw = kc_ref[0, 0, pl.ds(end, nc), :].astype(BF16)
    vcw = vc_ref[0, 0, pl.ds(end, nc), :].astype(BF16)
    qpos = i0 * Q_BLOCK + lax.broadcasted_iota(jnp.int32, (rq, 1), 0) % pq
    p_c = _softmax2_rows(_dot_nt(q_pad, kcw) + rc_ref[0]) * (qpos >= CMP_BLOCK - 1).astype(F32)
    o_c = _dot(p_c.astype(BF16), vcw)
    p_sum = p_c[heads[0]]
    for h in range(1, r):
        p_sum = p_sum + p_c[heads[h]]

    p_hi = p_sum.astype(BF16)
    p_lo = (p_sum - p_hi.astype(F32)).astype(BF16)
    imp2 = _dot_nt(ovt_ref[...], jnp.concatenate([p_hi, p_lo], axis=0))
    imp = imp2[:, :pq] + imp2[:, pq:]

    ws = pl.multiple_of(i0 * Q_BLOCK, pq)
    sw = _dot_nt(q_pad, kw_ref[0, 0, pl.ds(ws, wl), :]) + wb_ref[0]
    p_w = jnp.exp2((sw - jnp.max(sw, axis=-1, keepdims=True)).astype(BF16))
    o_w = _normalize(_dot(p_w, vw_ref[0, 0, pl.ds(ws, wl), :]))

    shift = i0 + Q_PAIR
    blk_rel = lax.broadcasted_iota(jnp.int32, (n_sel, pq), 0)
    blk = blk_rel + shift - n_sel
    cur = i0 + lax.broadcasted_iota(jnp.int32, (n_sel, pq), 1) // Q_BLOCK
    forced = (blk == 0) | (blk == cur) | (blk == cur - 1)
    imp = jnp.where(forced, FORCE, jnp.where(blk > cur, NEG_INF, imp))
    imp = jnp.where(blk < 0, -jnp.inf, imp)
    ids = blk_rel.astype(F32)
    sel = jnp.zeros((n_sel, pq), F32)
    for _ in range(min(SEL_TOP, n_sel)):
        mx = jnp.max(imp, axis=0, keepdims=True)
        ix = jnp.min(jnp.where(imp == mx, ids, float(n_sel)), axis=0, keepdims=True)
        hit = ids == ix
        sel = jnp.where(hit, 1.0, sel)
        imp = jnp.where(hit, -jnp.inf, imp)
    unsel = jnp.where((sel > 0.0) & (blk >= 0), 0.0, NEG_INF)
    unsel = pltpu.roll(unsel.T, shift % n_sel, 1).astype(BF16)
    if n_sel < LANES:
        unsel = jnp.concatenate([unsel, jnp.zeros((pq, LANES - n_sel), BF16)], axis=1)

    q_aug = jnp.concatenate([q_pad, jnp.concatenate([unsel] * r, axis=0)], axis=1)
    kchunk = sa_ref.shape[1]
    n_chunks = ks_ref.shape[2] // kchunk
    cblocks = kchunk // SEL_BLOCK
    pairs = cblocks // 2

    def scores_to(dst, c):
        start = pl.multiple_of(jnp.minimum(c, n_chunks - 1) * kchunk, kchunk)
        sc = _dot_nt(q_aug, ks_ref[0, 0, pl.ds(start, kchunk), :])
        d0 = i0 - c * cblocks
        tiles = [[jnp.clip(d0 + u - 2 * pm, -1, nd) + 1 for pm in range(pairs)] for u in range(Q_PAIR)]
        dst[...] = sc + jnp.concatenate(
            [jnp.concatenate([bt_ref[0, t, h] for t in tiles[u]], axis=1)
             for h in range(r) for u in range(Q_PAIR)], axis=0)

    def absorb(src, c, m, acc):
        start = pl.multiple_of(c * kchunk, kchunk)
        sc = src[...]
        m_new = jnp.maximum(m, jnp.max(sc, axis=-1, keepdims=True))
        p = jnp.exp2((sc - m_new).astype(BF16))
        acc = jnp.exp2(m - m_new) * acc + _dot(p, vs_ref[0, 0, pl.ds(start, kchunk), :])
        return m_new, acc

    def two_chunks(j, carry):
        scores_to(sb_ref, 2 * j + 1)
        carry = absorb(sa_ref, 2 * j, *carry)
        scores_to(sa_ref, 2 * j + 2)
        return absorb(sb_ref, 2 * j + 1, *carry)

    scores_to(sa_ref, 0)
    init = (jnp.full((rq, 1), NEG_INF, F32), jnp.zeros((rq, LANES), F32))
    _, acc_s = lax.fori_loop(0, (i0 + Q_PAIR - 1) // (2 * cblocks) + 1, two_chunks, init)
    o_s = _normalize(acc_s)

    gates = gate_ref[0]
    glane = lax.broadcasted_iota(jnp.int32, gates.shape, 1)
    outs = []
    for h in range(r):
        head = g * r + h
        gs = [jnp.sum(jnp.where(glane == br * N_HEADS + head, gates, 0.0), axis=-1, keepdims=True)
              for br in range(N_BRANCH)]
        outs.append(gs[0] * o_c[heads[h]] + gs[1] * o_s[heads[h]] + gs[2] * o_w[heads[h]])
    o_ref[0] = jnp.concatenate(
        [outs[h] + pltpu.roll(outs[h + 1], HEAD_DIM, 1) for h in range(0, r, 2)], axis=1)


def _nsa_attention(q, gates, kc_pad, vc_pad, ks_aug, vs, kw_pad, vw_pad, tables):
    b, _, s, _ = q.shape
    r = N_HEADS // N_KV_GROUPS
    rc, bt, wb, ov = tables
    n_qb = s // Q_BLOCK
    per_bg = lambda a: pl.BlockSpec((1, 1) + a.shape[2:], lambda bi, gi, qi: (bi, gi, 0, 0))
    per_g = lambda a: pl.BlockSpec((1,) + a.shape[1:], lambda bi, gi, qi: (gi,) + (0,) * (a.ndim - 1))
    pq = Q_PAIR * Q_BLOCK
    return pl.pallas_call(
        _attn_body,
        grid=(b, N_KV_GROUPS, n_qb // Q_PAIR),
        in_specs=[pl.BlockSpec((1, r, pq, LANES), lambda bi, gi, qi: (bi, gi, qi, 0)),
                  pl.BlockSpec((1, pq, LANES), lambda bi, gi, qi: (bi, qi, 0)),
                  per_bg(kc_pad), per_bg(vc_pad), per_bg(ks_aug), per_bg(vs), per_bg(kw_pad), per_bg(vw_pad),
                  per_g(rc), per_g(bt), per_g(wb),
                  pl.BlockSpec(ov.shape, lambda bi, gi, qi: (0, 0))],
        out_specs=pl.BlockSpec((1, pq, r * HEAD_DIM), lambda bi, gi, qi: (bi, qi, gi)),
        out_shape=jax.ShapeDtypeStruct((b, s, N_HEADS * HEAD_DIM), F32),
        scratch_shapes=[pltpu.VMEM((r * pq, min(KEY_CHUNK, s // 2)), F32)] * 2,
        compiler_params=_cparams(("arbitrary", "arbitrary", "arbitrary")),
        name="nsa_attention",
    )(q, gates, kc_pad, vc_pad, ks_aug, vs, kw_pad, vw_pad, rc, bt, wb, ov)


def _outproj_body(a_ref, h_ref, w_ref, o_ref):
    o_ref[...] = h_ref[...] + _dot(a_ref[...].astype(BF16), w_ref[...])


def _out_project(attn, h, w_out):
    n, d = h.shape
    hd = attn.shape[1]
    tm = min(512, n)
    return pl.pallas_call(
        _outproj_body,
        grid=(n // tm,),
        in_specs=[pl.BlockSpec((tm, hd), lambda i: (i, 0)), pl.BlockSpec((tm, d), lambda i: (i, 0)),
                  pl.BlockSpec((hd, d), lambda i: (0, 0))],
        out_specs=pl.BlockSpec((tm, d), lambda i: (i, 0)),
        out_shape=jax.ShapeDtypeStruct((n, d), F32),
        compiler_params=_cparams(("arbitrary",)),
        name="nsa_outproj",
    )(attn, h, w_out.astype(BF16))


def kernel(x, p, rel_bias, norm_mix, norm_ffn, a_w_in, a_ln_g, a_ln_b, a_w_s, a_b_s, a_w_out, kv_norm, kv_w, cmp_pos, cmp_w1, cmp_b1, cmp_w2, k_norm, b_w_in, b_b_gate, q_norm, b_w_out, router_w, router_b, e_w_gu, e_b_gu, e_w_d, e_b_d, ple_w, ple_gate_w, ple_norm):
    b, s, d = x.shape
    n = b * s
    pf = p.reshape(p.shape[0], n, p.shape[-1])

    def moe_ple(h, i):
        return _moe_ple_layer(h, norm_ffn[i], router_w[i], router_b[i], e_w_gu, e_b_gu[i], e_w_d, e_b_d[i], i,
                              pf[i], ple_w[i], ple_gate_w[i], ple_norm[i])

    h = _gmlp_layer(x.reshape(n, d), norm_mix[0], a_w_in[0], a_ln_g[0], a_ln_b[0], a_w_s[0], a_b_s[0], a_w_out[0])
    h = moe_ple(h, 0)

    h3 = h.reshape(b, s, d)
    kc, vc, ks_aug, vs, kw, vw = _kv_project(h3, kv_norm, kv_w, k_norm)
    kc_pad, vc_pad = _compress(kc, vc, cmp_pos, cmp_w1, cmp_b1, cmp_w2, k_norm[0])
    front = jnp.zeros((b, N_KV_GROUPS, WINDOW, LANES), BF16).at[..., HEAD_DIM].set(1.0)
    kw_pad = jnp.concatenate([front, kw, front[:, :, :2 * Q_BLOCK]], axis=2)
    vw_pad = jnp.pad(vw, ((0, 0), (0, 0), (WINDOW, 2 * Q_BLOCK), (0, 0)))

    q, gates = _q_project(h3, norm_mix[1], b_w_in[0], b_b_gate[0], q_norm[0])
    attn = _nsa_attention(q, gates, kc_pad, vc_pad, ks_aug, vs, kw_pad, vw_pad, _attn_tables(rel_bias, s))
    h = _out_project(attn.reshape(n, -1), h, b_w_out[0])
    h = moe_ple(h, 1)
    return h.reshape(b, s, d)
```

```python
import functools
import math

import numpy as np
import jax
import jax.numpy as jnp
from jax import lax
from jax.experimental import pallas as pl
from jax.experimental.pallas import tpu as pltpu

D_MODEL = 1024
GMLP_CHUNK = 128
GMLP_GROUPS = 8
N_HEADS = 16
HEAD_DIM = 64
N_KV_GROUPS = 4
N_BRANCH = 3
CMP_BLOCK = 32
CMP_STRIDE = 16
CMP_HIDDEN = 256
SEL_BLOCK = 64
SEL_TOP = 16
WINDOW = 512
Q_BLOCK = 64
N_KV_SLOTS = 6
REL_BUCKETS = 32
REL_MAX_DIST = 2048
N_EXPERTS = 32
TOP_K = 4
SWIGLU_LIMIT = 7.0
SWIGLU_ALPHA = 1.702
NORM_EPS = 1e-6
NEG_INF = -1e30
FORCE = 1e30
LOG2E = math.log2(math.e)

LANES = 128
MOE_ROWS = 512
KEY_CHUNK = 512
Q_PAIR = 4
VMEM_LIMIT = 56 * 1024 * 1024

F32 = jnp.float32
BF16 = jnp.bfloat16


def _cparams(sem):
    return pltpu.CompilerParams(dimension_semantics=sem, vmem_limit_bytes=VMEM_LIMIT)


def _rms(x, g):
    return x * lax.rsqrt(jnp.mean(x * x, axis=-1, keepdims=True) + NORM_EPS) * g


def _dot(a, b):
    return jnp.dot(a, b, preferred_element_type=F32)


def _dot_nt(a, b):
    return lax.dot_general(a, b, (((1,), (1,)), ((), ())), preferred_element_type=F32)


def _dot_exact(a, b):
    return jnp.dot(a, b, preferred_element_type=F32, precision=lax.Precision.HIGHEST)


def _softmax2_rows(s):
    m = jnp.max(s, axis=-1, keepdims=True)
    p = jnp.exp2(s - m)
    return p / jnp.sum(p, axis=-1, keepdims=True)


def _normalize(acc):
    lane = lax.broadcasted_iota(jnp.int32, acc.shape, 1)
    denom = jnp.sum(jnp.where(lane == HEAD_DIM, acc, 0.0), axis=-1, keepdims=True)
    return jnp.where(lane < HEAD_DIM, acc / denom, 0.0)


def _head_lanes(x, idx):
    base = idx * HEAD_DIM // LANES * LANES
    y = x[:, base:base + LANES]
    return y if idx * HEAD_DIM == base else pltpu.roll(y, LANES - HEAD_DIM, 1)


def _low_lanes(x, fill):
    lane = lax.broadcasted_iota(jnp.int32, x.shape, 1)
    return jnp.where(lane < HEAD_DIM, x, fill)


def _argmax_first(x, ids, n):
    mx = jnp.max(x, axis=-1, keepdims=True)
    return mx, jnp.min(jnp.where(x == mx, ids, float(n)), axis=-1, keepdims=True)


def _gmlp_body(x_ref, nm_ref, win_ref, lng_ref, lnb_ref, ws_ref, bs_ref, wout_ref, o_ref):
    tm = x_ref.shape[0]
    gd = win_ref.shape[1] // 2
    gdim = gd // GMLP_GROUPS
    x = x_ref[...]
    xn = _rms(x, nm_ref[...]).astype(BF16)
    z = jax.nn.gelu(_dot(xn, win_ref[...]))
    u = z[:, :gd]
    v = z[:, gd:]
    mu = jnp.mean(v, axis=-1, keepdims=True)
    vc = v - mu
    vln = vc * lax.rsqrt(jnp.mean(vc * vc, axis=-1, keepdims=True) + NORM_EPS) * lng_ref[...] + lnb_ref[...]
    vb = vln.astype(BF16)
    row = lax.broadcasted_iota(jnp.int32, (GMLP_CHUNK, GMLP_CHUNK), 0)
    col = lax.broadcasted_iota(jnp.int32, (GMLP_CHUNK, GMLP_CHUNK), 1)
    causal = row >= col
    chunks = []
    for c in range(tm // GMLP_CHUNK):
        cols = []
        for g in range(GMLP_GROUPS):
            wsg = jnp.where(causal, ws_ref[g], 0.0).astype(BF16)
            vg = vb[c * GMLP_CHUNK:(c + 1) * GMLP_CHUNK, g * gdim:(g + 1) * gdim]
            cols.append(_dot(wsg, vg) + bs_ref[g])
        chunks.append(jnp.concatenate(cols, axis=1))
    mixed = jnp.concatenate(chunks, axis=0)
    gated = (u * mixed).astype(BF16)
    o_ref[...] = x + _dot(gated, wout_ref[...])


def _gmlp_layer(h, norm_g, w_in, ln_g, ln_b, w_s, b_s, w_out):
    n, d = h.shape
    gd2 = w_in.shape[1]
    gd = gd2 // 2
    tm = min(512, n)
    full = lambda *shape: pl.BlockSpec(shape, lambda i: (0,) * len(shape))
    return pl.pallas_call(
        _gmlp_body,
        grid=(n // tm,),
        in_specs=[pl.BlockSpec((tm, d), lambda i: (i, 0)),
                  full(1, d), full(d, gd2), full(1, gd), full(1, gd),
                  full(GMLP_GROUPS, GMLP_CHUNK, GMLP_CHUNK), full(GMLP_GROUPS, GMLP_CHUNK, 1),
                  full(gd, d)],
        out_specs=pl.BlockSpec((tm, d), lambda i: (i, 0)),
        out_shape=jax.ShapeDtypeStruct((n, d), F32),
        compiler_params=_cparams(("arbitrary",)),
        name="gmlp_layer",
    )(h, norm_g.reshape(1, d), w_in.astype(BF16), ln_g.reshape(1, gd), ln_b.reshape(1, gd),
      w_s, b_s.reshape(GMLP_GROUPS, GMLP_CHUNK, 1), w_out.astype(BF16))


def _route_body(h_ref, ng_ref, rw_ref, rb_ref, o_ref, cnt_ref, run_ref):
    i = pl.program_id(0)
    tm = h_ref.shape[0]

    @pl.when(i == 0)
    def _():
        run_ref[...] = jnp.zeros_like(run_ref)

    xn = _rms(h_ref[...], ng_ref[...])
    logits = _dot_exact(xn, rw_ref[...]) + rb_ref[...]
    eid = lax.broadcasted_iota(jnp.int32, logits.shape, 1).astype(F32)
    lane = lax.broadcasted_iota(jnp.int32, (tm, LANES), 1)
    work = logits
    vals, idxs = [], []
    for _ in range(TOP_K):
        mx, ix = _argmax_first(work, eid, N_EXPERTS)
        vals.append(mx)
        idxs.append(ix)
        work = jnp.where(eid == ix, -jnp.inf, work)
    exps = [jnp.exp(v - vals[0]) for v in vals]
    den = exps[0]
    for e in exps[1:]:
        den = den + e
    onehot = jnp.zeros(logits.shape, F32)
    for ix in idxs:
        onehot = onehot + (eid == ix).astype(F32)
    r = lax.broadcasted_iota(jnp.int32, (tm, tm), 0)
    c = lax.broadcasted_iota(jnp.int32, (tm, tm), 1)
    before = (r > c).astype(BF16)
    prefix = _dot(before, onehot.astype(BF16)) + run_ref[...]
    out = jnp.zeros((tm, LANES), F32)
    for k in range(TOP_K):
        rank = jnp.sum(jnp.where(eid == idxs[k], prefix, 0.0), axis=-1, keepdims=True)
        out = jnp.where(lane == k, idxs[k], out)
        out = jnp.where(lane == TOP_K + k, exps[k] / den, out)
        out = jnp.where(lane == 2 * TOP_K + k, rank, out)
    o_ref[...] = out
    run_ref[...] = run_ref[...] + jnp.sum(onehot, axis=0, keepdims=True)
    cnt_ref[...] = run_ref[...]


def _moe_route(h, norm_g, router_w, router_b):
    n, d = h.shape
    tm = min(512, n)
    full = lambda *shape: pl.BlockSpec(shape, lambda i: (0,) * len(shape))
    return pl.pallas_call(
        _route_body,
        grid=(n // tm,),
        in_specs=[pl.BlockSpec((tm, d), lambda i: (i, 0)), full(1, d), full(d, N_EXPERTS), full(1, N_EXPERTS)],
        out_specs=[pl.BlockSpec((tm, LANES), lambda i: (i, 0)), full(1, N_EXPERTS)],
        out_shape=[jax.ShapeDtypeStruct((n, LANES), F32), jax.ShapeDtypeStruct((1, N_EXPERTS), F32)],
        scratch_shapes=[pltpu.VMEM((1, N_EXPERTS), F32)],
        compiler_params=_cparams(("arbitrary",)),
        name="moe_route",
    )(h, norm_g.reshape(1, d), router_w, router_b.reshape(1, N_EXPERTS))


def _dispatch_body(pad_ref, dest_ref, h_ref, ng_ref, xs_out, buf, zbuf, sem, zsem):
    i = pl.program_id(0)
    tm = h_ref.shape[0]

    @pl.when(i == 0)
    def _():
        zbuf[...] = jnp.zeros_like(zbuf)
        for e in range(2 * N_EXPERTS):
            first = pl.multiple_of(pad_ref[e], MOE_ROWS)
            zero = pltpu.make_async_copy(zbuf, xs_out.at[pl.ds(first, MOE_ROWS), :], zsem)
            zero.start()
            zero.wait()

    buf[...] = _rms(h_ref[...], ng_ref[...])
    for j in range(tm):
        for k in range(TOP_K):
            r = dest_ref[0, 0, j * TOP_K + k]
            pltpu.make_async_copy(buf.at[pl.ds(j, 1), :], xs_out.at[pl.ds(r, 1), :], sem).start(priority=k % 2)
    for _ in range(TOP_K):
        pltpu.make_async_copy(buf, xs_out.at[pl.ds(0, tm), :], sem).wait()


def _moe_dispatch(h, norm_g, dest, pad_rows, n_rows):
    n, d = h.shape
    tm = min(256, n)
    return pl.pallas_call(
        _dispatch_body,
        grid_spec=pltpu.PrefetchScalarGridSpec(
            num_scalar_prefetch=1, grid=(n // tm,),
            in_specs=[pl.BlockSpec((1, 1, tm * TOP_K), lambda i, pr: (i, 0, 0), memory_space=pltpu.SMEM),
                      pl.BlockSpec((tm, d), lambda i, pr: (i, 0)),
                      pl.BlockSpec((1, d), lambda i, pr: (0, 0))],
            out_specs=pl.BlockSpec(memory_space=pl.ANY),
            scratch_shapes=[pltpu.VMEM((tm, d), F32), pltpu.VMEM((MOE_ROWS, d), F32),
                            pltpu.SemaphoreType.DMA(()), pltpu.SemaphoreType.DMA(())]),
        out_shape=jax.ShapeDtypeStruct((n_rows, d), F32),
        compiler_params=_cparams(("arbitrary",)),
        name="moe_dispatch",
    )(pad_rows, dest.reshape(n // tm, 1, tm * TOP_K), h, norm_g.reshape(1, d))


def _expert_body(be_ref, nu_ref, xs_ref, wgu_ref, bgu_ref, wd_ref, bd_ref, ys_ref, wgu_bf, wd_bf):
    i = pl.program_id(0)
    ed = wd_ref.shape[2]

    @pl.when((i == 0) | (be_ref[i] != be_ref[jnp.maximum(i - 1, 0)]))
    def _():
        wgu_bf[...] = wgu_ref[0, 0].astype(BF16)
        wd_bf[...] = wd_ref[0, 0].astype(BF16)

    @pl.when(i < nu_ref[0])
    def _():
        x = xs_ref[...].astype(BF16)
        gu = _dot(x, wgu_bf[...]) + bgu_ref[0]
        gate = jnp.minimum(gu[:, :ed], SWIGLU_LIMIT)
        up = jnp.clip(gu[:, ed:], -SWIGLU_LIMIT, SWIGLU_LIMIT)
        glu = gate * jax.nn.sigmoid(gate * SWIGLU_ALPHA)
        ys_ref[...] = _dot(((up + 1.0) * glu).astype(BF16), wd_bf[...]) + bd_ref[0]

    @pl.when(i >= nu_ref[0])
    def _():
        ys_ref[...] = jnp.zeros_like(ys_ref)


def _moe_experts(xs, blk_expert, n_used, w_gu, b_gu, w_d, b_d, layer):
    n_rows, d = xs.shape
    ed = w_d.shape[2]
    n_blk = n_rows // MOE_ROWS
    return pl.pallas_call(
        _expert_body,
        grid_spec=pltpu.PrefetchScalarGridSpec(
            num_scalar_prefetch=2, grid=(n_blk,),
            in_specs=[pl.BlockSpec((MOE_ROWS, d), lambda i, be, nu: (jnp.maximum(jnp.minimum(i, nu[0] - 1), 0), 0)),
                      pl.BlockSpec((1, 1, d, 2 * ed), lambda i, be, nu: (layer, be[i], 0, 0)),
                      pl.BlockSpec((1, 1, 2 * ed), lambda i, be, nu: (be[i], 0, 0)),
                      pl.BlockSpec((1, 1, ed, d), lambda i, be, nu: (layer, be[i], 0, 0)),
                      pl.BlockSpec((1, 1, d), lambda i, be, nu: (be[i], 0, 0))],
            out_specs=pl.BlockSpec((MOE_ROWS, d), lambda i, be, nu: (i, 0)),
            scratch_shapes=[pltpu.VMEM((d, 2 * ed), BF16), pltpu.VMEM((ed, d), BF16)]),
        out_shape=jax.ShapeDtypeStruct((n_rows, d), F32),
        compiler_params=_cparams(("arbitrary",)),
        name="moe_experts",
    )(blk_expert, n_used, xs, w_gu, b_gu.reshape(N_EXPERTS, 1, 2 * ed), w_d, b_d.reshape(N_EXPERTS, 1, d))


def _combine_body(dest_ref, next_ref, rt_ref, h_ref, p_ref, pw_ref, pg_ref, pn_ref, ys_hbm, o_ref, buf, sem):
    i = pl.program_id(0)
    n = pl.num_programs(0)
    tm = h_ref.shape[0]
    slot = i % 2

    def row_copy(idx_ref, j, k, s):
        r = idx_ref[0, 0, j * TOP_K + k]
        return pltpu.make_async_copy(ys_hbm.at[pl.ds(r, 1), :], buf.at[s, k, pl.ds(j, 1), :], sem.at[s])

    def wait_tile(s):
        for k in range(TOP_K):
            pltpu.make_async_copy(ys_hbm.at[pl.ds(0, tm), :], buf.at[s, k], sem.at[s]).wait()

    @pl.when(i == 0)
    def _():
        def first(j, carry):
            for k in range(TOP_K):
                row_copy(dest_ref, j, k, 0).start()
            return carry
        lax.fori_loop(0, tm, first, 0)

    def step(s):
        wait_tile(s)
        for j in range(tm):
            for k in range(TOP_K):
                row_copy(next_ref, j, k, 1 - s).start(priority=k % 2)
        rt = rt_ref[...]
        h = h_ref[...]
        for k in range(TOP_K):
            h = h + rt[:, TOP_K + k:TOP_K + k + 1] * buf[s, k]
        emb = _dot(p_ref[...].astype(BF16), pw_ref[...])
        gate = jax.nn.sigmoid(_dot(_rms(h, pn_ref[...]).astype(BF16), pg_ref[...]))
        o_ref[...] = h + emb * gate

        @pl.when(i == n - 1)
        def _():
            wait_tile(1 - s)

    for s in range(2):
        pl.when(slot == s)(functools.partial(step, s))


def _moe_combine_ple(h, route, dest, ys, p, ple_w, ple_gate_w, ple_norm):
    n, d = h.shape
    pd = p.shape[1]
    tm = min(256, n)
    nt = n // tm
    full = lambda *shape: pl.BlockSpec(shape, lambda i: (0,) * len(shape))
    dest3 = dest.reshape(nt, 1, tm * TOP_K)
    return pl.pallas_call(
        _combine_body,
        grid=(nt,),
        in_specs=[pl.BlockSpec((1, 1, tm * TOP_K), lambda i: (i, 0, 0), memory_space=pltpu.SMEM),
                  pl.BlockSpec((1, 1, tm * TOP_K), lambda i: (jnp.minimum(i + 1, nt - 1), 0, 0),
                               memory_space=pltpu.SMEM),
                  pl.BlockSpec((tm, LANES), lambda i: (i, 0)),
                  pl.BlockSpec((tm, d), lambda i: (i, 0)),
                  pl.BlockSpec((tm, pd), lambda i: (i, 0)),
                  full(pd, d), full(d, d), full(1, d),
                  pl.BlockSpec(memory_space=pl.ANY)],
        out_specs=pl.BlockSpec((tm, d), lambda i: (i, 0)),
        out_shape=jax.ShapeDtypeStruct((n, d), F32),
        scratch_shapes=[pltpu.VMEM((2, TOP_K, tm, d), F32), pltpu.SemaphoreType.DMA((2,))],
        compiler_params=_cparams(("arbitrary",)),
        name="moe_combine_ple",
    )(dest3, dest3, route, h, p, ple_w.astype(BF16), ple_gate_w.astype(BF16), ple_norm.reshape(1, d), ys)


def _moe_ple_layer(h, norm_g, router_w, router_b, w_gu, b_gu, w_d, b_d, layer, p, ple_w, ple_gate_w, ple_norm):
    n, _ = h.shape
    route, counts = _moe_route(h, norm_g, router_w, router_b)
    counts = counts[0].astype(jnp.int32)
    pad_counts = (counts + MOE_ROWS - 1) // MOE_ROWS * MOE_ROWS
    pad_ends = jnp.cumsum(pad_counts)
    pad_starts = pad_ends - pad_counts
    top_idx = route[:, :TOP_K].astype(jnp.int32)
    rank = route[:, 2 * TOP_K:3 * TOP_K].astype(jnp.int32)
    dest = (pad_starts[top_idx] + rank).reshape(-1)
    n_blk = -(-(n * TOP_K) // MOE_ROWS) + N_EXPERTS
    blk_start = jnp.arange(n_blk, dtype=jnp.int32) * MOE_ROWS
    blk_expert = jnp.minimum(jnp.sum((pad_ends[None, :] <= blk_start[:, None]).astype(jnp.int32), axis=1),
                             N_EXPERTS - 1)
    n_used = (pad_ends[-1:] // MOE_ROWS).astype(jnp.int32)
    tail = jnp.minimum(pad_ends[-1] + jnp.arange(N_EXPERTS, dtype=jnp.int32) * MOE_ROWS, (n_blk - 1) * MOE_ROWS)
    clear = jnp.concatenate([jnp.maximum(pad_ends - MOE_ROWS, 0), tail]).astype(jnp.int32)
    xs = _moe_dispatch(h, norm_g, dest, clear, n_blk * MOE_ROWS)
    ys = _moe_experts(xs, blk_expert, n_used, w_gu, b_gu, w_d, b_d, layer)
    return _moe_combine_ple(h, route, dest, ys, p, ple_w, ple_gate_w, ple_norm)


def _kv_body(h_ref, ng_ref, w_ref, seg_ref, kn_ref, kc_ref, vc_ref, ks_ref, vs_ref, kw_ref, vw_ref):
    ts = h_ref.shape[1]
    gw = N_KV_GROUPS * HEAD_DIM
    st = pl.program_id(1)
    hn = _rms(h_ref[0], ng_ref[...]).astype(BF16)
    kv = _dot(hn, w_ref[...])

    def knorm(x, j):
        ms = _dot_exact(x * x, seg_ref[...])
        return x * lax.rsqrt(ms + NORM_EPS) * kn_ref[j]

    k_c, v_c = kv[:, 0:gw], kv[:, gw:2 * gw]
    k_s, v_s = knorm(kv[:, 2 * gw:3 * gw], 1), kv[:, 3 * gw:4 * gw]
    k_w, v_w = knorm(kv[:, 4 * gw:5 * gw], 2), kv[:, 5 * gw:6 * gw]
    tok = st * ts + lax.broadcasted_iota(jnp.int32, (ts, LANES), 0)
    blk = lax.broadcasted_iota(jnp.int32, (ts, LANES), 1)
    onehot = (tok // SEL_BLOCK == blk).astype(BF16)
    one_col = (blk == HEAD_DIM).astype(F32)
    for g in range(N_KV_GROUPS):
        kc_ref[0, g] = _head_lanes(k_c, g)[:, :HEAD_DIM]
        vc_ref[0, g] = _head_lanes(v_c, g)[:, :HEAD_DIM]
        ks_ref[0, g] = jnp.concatenate([_low_lanes(_head_lanes(k_s, g), 0.0).astype(BF16), onehot], axis=1)
        vs_ref[0, g] = _low_lanes(_head_lanes(v_s, g), one_col).astype(BF16)
        kw_ref[0, g] = _low_lanes(_head_lanes(k_w, g), 0.0).astype(BF16)
        vw_ref[0, g] = _low_lanes(_head_lanes(v_w, g), one_col).astype(BF16)


def _kv_project(h3, kv_norm, kv_w, k_norm):
    b, s, d = h3.shape
    gw = N_KV_GROUPS * HEAD_DIM
    ts = min(512, s)
    seg = jnp.asarray(np.kron(np.eye(N_KV_GROUPS), np.full((HEAD_DIM, HEAD_DIM), 1.0 / HEAD_DIM)), F32)
    kn = jnp.tile(k_norm, (1, N_KV_GROUPS)).reshape(N_BRANCH, 1, gw)
    full = lambda *shape: pl.BlockSpec(shape, lambda bi, si: (0,) * len(shape))
    hd = lambda w: pl.BlockSpec((1, N_KV_GROUPS, ts, w), lambda bi, si: (bi, 0, si, 0))
    sds = lambda w, dt: jax.ShapeDtypeStruct((b, N_KV_GROUPS, s, w), dt)
    return pl.pallas_call(
        _kv_body,
        grid=(b, s // ts),
        in_specs=[pl.BlockSpec((1, ts, d), lambda bi, si: (bi, si, 0)), full(1, d), full(d, N_KV_SLOTS * gw),
                  full(gw, gw), full(N_BRANCH, 1, gw)],
        out_specs=[hd(HEAD_DIM), hd(HEAD_DIM), hd(2 * LANES), hd(LANES), hd(LANES), hd(LANES)],
        out_shape=[sds(HEAD_DIM, F32), sds(HEAD_DIM, F32), sds(2 * LANES, BF16), sds(LANES, BF16),
                   sds(LANES, BF16), sds(LANES, BF16)],
        compiler_params=_cparams(("arbitrary", "arbitrary")),
        name="kv_project",
    )(h3, kv_norm.reshape(1, d), kv_w.astype(BF16), seg, kn)


def _compress_body(kc_ref, vc_ref, pos_ref, w1_ref, b1_ref, w2_ref, kn_ref, ko_ref, vo_ref):
    nc = kc_ref.shape[2]
    half = kc_ref.shape[3]
    row = lax.broadcasted_iota(jnp.int32, (nc, 1), 0)
    valid = row < nc - 1

    def compress(c, j):
        a = _dot((c + pos_ref[j, 0]).astype(BF16), w1_ref[j, 0])
        bm = _dot((c + pos_ref[j, 1]).astype(BF16), w1_ref[j, 1])
        nxt = pltpu.roll(bm, nc - 1, 0)
        hid = jax.nn.gelu(a + nxt + b1_ref[j])
        return _dot(hid.astype(BF16), w2_ref[j])

    kraw = compress(kc_ref[0, 0], 0)
    ms = jnp.sum(kraw * kraw, axis=-1, keepdims=True) * (1.0 / HEAD_DIM)
    kcmp = kraw * lax.rsqrt(ms + NORM_EPS) * kn_ref[...]
    vcmp = jnp.where(valid, compress(vc_ref[0, 0], 1), 0.0)
    lane = lax.broadcasted_iota(jnp.int32, (nc, LANES), 1)
    flag = (lane == HEAD_DIM).astype(F32)
    ko_ref[0, 0, 0:nc, :] = flag
    ko_ref[0, 0, nc:2 * nc, :] = jnp.where(valid, kcmp, flag)
    vo_ref[0, 0, 0:nc, :] = jnp.zeros((nc, LANES), F32)
    vo_ref[0, 0, nc:2 * nc, :] = vcmp


def _compress(kc, vc, cmp_pos, cmp_w1, cmp_b1, cmp_w2, k_norm0):
    b, g, s, dh = kc.shape
    nc = s // CMP_STRIDE
    half = CMP_STRIDE * dh
    kc_r = kc.reshape(b, g, nc, half)
    vc_r = vc.reshape(b, g, nc, half)
    pos = cmp_pos.reshape(2, 2, 1, half)
    w1 = cmp_w1.reshape(2, 2, half, CMP_HIDDEN).astype(BF16)
    w2 = jnp.pad(cmp_w2, ((0, 0), (0, 0), (0, LANES - dh))).astype(BF16)
    kn = jnp.pad(k_norm0, (0, LANES - dh)).reshape(1, LANES)
    full = lambda *shape: pl.BlockSpec(shape, lambda bi, gi: (0,) * len(shape))
    blk = pl.BlockSpec((1, 1, nc, half), lambda bi, gi: (bi, gi, 0, 0))
    out = pl.BlockSpec((1, 1, 2 * nc, LANES), lambda bi, gi: (bi, gi, 0, 0))
    return pl.pallas_call(
        _compress_body,
        grid=(b, g),
        in_specs=[blk, blk, full(2, 2, 1, half), full(2, 2, half, CMP_HIDDEN), full(2, 1, CMP_HIDDEN),
                  full(2, CMP_HIDDEN, LANES), full(1, LANES)],
        out_specs=[out, out],
        out_shape=[jax.ShapeDtypeStruct((b, g, 2 * nc, LANES), F32)] * 2,
        compiler_params=_cparams(("arbitrary", "arbitrary")),
        name="kv_compress",
    )(kc_r, vc_r, pos, w1, cmp_b1.reshape(2, 1, CMP_HIDDEN), w2, kn)


def _qproj_body(h_ref, ng_ref, w_ref, bg_ref, ind_ref, indt_ref, qn_ref, q_ref, gate_ref):
    hd = N_HEADS * HEAD_DIM
    xn = _rms(h_ref[0], ng_ref[...]).astype(BF16)
    proj = _dot(xn, w_ref[...])
    q = proj[:, :hd]
    ms = _dot_exact(q * q, ind_ref[...]) * (1.0 / HEAD_DIM)
    scale = _dot_exact(lax.rsqrt(ms + NORM_EPS), indt_ref[...])
    qn = q * scale * qn_ref[...] * (HEAD_DIM ** -0.5 * LOG2E)
    lane = lax.broadcasted_iota(jnp.int32, (q.shape[0], LANES), 1)
    fill = jnp.where(lane == HEAD_DIM, NEG_INF, 0.0)
    for h in range(N_HEADS):
        q_ref[0, h] = _low_lanes(_head_lanes(qn, h), fill).astype(BF16)
    gate_ref[0] = jax.nn.sigmoid(proj[:, hd:] + bg_ref[...])


def _q_project(h3, norm_g, w_in, b_gate, q_norm):
    b, s, d = h3.shape
    hd = N_HEADS * HEAD_DIM
    ng = N_BRANCH * N_HEADS
    ts = min(512, s)
    w = jnp.pad(w_in, ((0, 0), (0, LANES - ng))).astype(BF16)
    bg = jnp.pad(b_gate, (0, LANES - ng)).reshape(1, LANES)
    ind = np.zeros((hd, LANES), np.float32)
    ind[np.arange(hd), np.arange(hd) // HEAD_DIM] = 1.0
    full = lambda *shape: pl.BlockSpec(shape, lambda bi, si: (0,) * len(shape))
    return pl.pallas_call(
        _qproj_body,
        grid=(b, s // ts),
        in_specs=[pl.BlockSpec((1, ts, d), lambda bi, si: (bi, si, 0)), full(1, d), full(d, hd + LANES),
                  full(1, LANES), full(hd, LANES), full(LANES, hd), full(1, hd)],
        out_specs=[pl.BlockSpec((1, N_HEADS, ts, LANES), lambda bi, si: (bi, 0, si, 0)),
                   pl.BlockSpec((1, ts, LANES), lambda bi, si: (bi, si, 0))],
        out_shape=[jax.ShapeDtypeStruct((b, N_HEADS, s, LANES), BF16),
                   jax.ShapeDtypeStruct((b, s, LANES), F32)],
        compiler_params=_cparams(("arbitrary", "arbitrary")),
        name="nsa_qproj",
    )(h3, norm_g.reshape(1, d), w, bg, jnp.asarray(ind), jnp.asarray(ind.T),
      jnp.tile(q_norm, N_HEADS).reshape(1, hd))


def _t5_bucket_np(dist):
    n = np.maximum(dist, 0)
    max_exact = REL_BUCKETS // 2
    nf = np.maximum(n, 1).astype(np.float64)
    large = max_exact + (np.log(nf / max_exact) / math.log(REL_MAX_DIST / max_exact)
                         * (REL_BUCKETS - max_exact)).astype(np.int64)
    return np.where(n < max_exact, n, np.minimum(large, REL_BUCKETS - 1))


def _bias_table(rel_bias, dist, valid):
    r = N_HEADS // N_KV_GROUPS
    tab = rel_bias.astype(F32).T.reshape(N_KV_GROUPS, r, REL_BUCKETS)
    onehot = jnp.asarray(_t5_bucket_np(dist)[..., None] == np.arange(REL_BUCKETS), F32)
    bias = jnp.einsum('xqln,grn->gxrql', onehot, tab, precision=lax.Precision.HIGHEST) * LOG2E
    return jnp.where(jnp.asarray(valid)[None, :, None, :, :], bias, NEG_INF)


def _n_delta(seq):
    d = np.arange(seq + SEL_BLOCK)
    bk = _t5_bucket_np(d)
    change = np.nonzero(bk[1:] != bk[:-1])[0]
    d_const = int(change[-1]) + 1 if change.size else 0
    return -(-(d_const + SEL_BLOCK - 1) // SEL_BLOCK) + 1


def _attn_tables(rel_bias, seq):
    r = N_HEADS // N_KV_GROUPS
    qi = np.arange(Q_BLOCK)[:, None]
    rows = lambda t: jnp.transpose(t, (0, 2, 1, 3, 4)).reshape(N_KV_GROUPS, r * Q_PAIR * Q_BLOCK, t.shape[-1])
    nc = seq // CMP_STRIDE
    j = np.arange(nc)[None, :]
    dist_c = np.stack([qi - (CMP_BLOCK - 1) - Q_BLOCK * (Q_PAIR - u) + CMP_STRIDE * (nc - j)
                       for u in range(Q_PAIR)])
    rc = rows(_bias_table(rel_bias, dist_c, dist_c >= 0))
    nd = _n_delta(seq)
    delta = np.arange(-1, nd + 1)[:, None, None]
    kj = np.arange(2 * SEL_BLOCK)[None, None, :]
    dist_s = SEL_BLOCK * (delta - kj // SEL_BLOCK) + qi[None] - kj % SEL_BLOCK
    bt = _bias_table(rel_bias, dist_s, dist_s >= 0)
    jw = np.arange(WINDOW + 4 * Q_BLOCK)[None, :]
    dist_w = np.stack([Q_BLOCK * u + qi - jw + WINDOW for u in range(Q_PAIR)])
    wb = rows(_bias_table(rel_bias, dist_w, (dist_w >= 0) & (dist_w < WINDOW)))
    n_sel = seq // SEL_BLOCK
    cs = np.arange(nc) * CMP_STRIDE
    ss = np.arange(n_sel) * SEL_BLOCK
    ov = np.clip(np.minimum(cs[:, None] + CMP_BLOCK, ss[None, :] + SEL_BLOCK)
                 - np.maximum(cs[:, None], ss[None, :]), 0, None) / CMP_BLOCK
    return rc, bt, wb, jnp.asarray(ov.T.astype(np.float32)).astype(BF16)


def _attn_body(q_ref, gate_ref, kc_ref, vc_ref, ks_ref, vs_ref, kw_ref, vw_ref, rc_ref, bt_ref, wb_ref, ovt_ref,
               o_ref, sa_ref, sb_ref):
    g = pl.program_id(1)
    i0 = Q_PAIR * pl.program_id(2)
    r = q_ref.shape[1]
    pq = Q_PAIR * Q_BLOCK
    rq = r * pq
    nc = rc_ref.shape[2]
    n_sel = ovt_ref.shape[0]
    nd = bt_ref.shape[1] - 2
    wl = wb_ref.shape[2]
    per = SEL_BLOCK // CMP_STRIDE

    heads = [slice(h * pq, (h + 1) * pq) for h in range(r)]
    q_pad = q_ref[0].reshape(rq, LANES)

    end = pl.multiple_of(per * (i0 + Q_PAIR), Q_PAIR * per)
    kcw = kc_ref[0, 0, pl.ds(end, nc), :].astype(BF16)
    vcw = vc_ref[0, 0, pl.ds(end, nc), :].astype(BF16)
    qpos = i0 * Q_BLOCK + lax.broadcasted_iota(jnp.int32, (rq, 1), 0) % pq
    sc = _dot_nt(q_pad, kcw) + rc_ref[0]
    ws = pl.multiple_of(i0 * Q_BLOCK, pq)
    sw = _dot_nt(q_pad, kw_ref[0, 0, pl.ds(ws, wl), :]) + wb_ref[0]
    p_c = _softmax2_rows(sc) * (qpos >= CMP_BLOCK - 1).astype(F32)
    o_c = _dot(p_c.astype(BF16), vcw)
    p_w = jnp.exp2((sw - jnp.max(sw, axis=-1, keepdims=True)).astype(BF16))
    o_w = _normalize(_dot(p_w, vw_ref[0, 0, pl.ds(ws, wl), :]))
    p_sum = p_c[heads[0]]
    for h in range(1, r):
        p_sum = p_sum + p_c[heads[h]]

    p_hi = p_sum.astype(BF16)
    p_lo = (p_sum - p_hi.astype(F32)).astype(BF16)
    imp2 = _dot_nt(ovt_ref[...], jnp.concatenate([p_hi, p_lo], axis=0))
    imp = imp2[:, :pq] + imp2[:, pq:]

    shift = i0 + Q_PAIR
    blk_rel = lax.broadcasted_iota(jnp.int32, (n_sel, pq), 0)
    blk = blk_rel + shift - n_sel
    cur = i0 + lax.broadcasted_iota(jnp.int32, (n_sel, pq), 1) // Q_BLOCK
    forced = (blk == 0) | (blk == cur) | (blk == cur - 1)
    imp = jnp.where(forced, FORCE, jnp.where(blk > cur, NEG_INF, imp))
    imp = jnp.where(blk < 0, -jnp.inf, imp)
    ids = blk_rel.astype(F32)
    sel = jnp.zeros((n_sel, pq), F32)
    for _ in range(min(SEL_TOP, n_sel)):
        mx = jnp.max(imp, axis=0, keepdims=True)
        ix = jnp.min(jnp.where(imp == mx, ids, float(n_sel)), axis=0, keepdims=True)
        hit = ids == ix
        sel = jnp.where(hit, 1.0, sel)
        imp = jnp.where(hit, -jnp.inf, imp)
    unsel = jnp.where((sel > 0.0) & (blk >= 0), 0.0, NEG_INF)
    unsel = pltpu.roll(unsel.T, shift % n_sel, 1).astype(BF16)
    if n_sel < LANES:
        unsel = jnp.concatenate([unsel, jnp.zeros((pq, LANES - n_sel), BF16)], axis=1)

    q_aug = jnp.concatenate([q_pad, jnp.concatenate([unsel] * r, axis=0)], axis=1)
    kchunk = sa_ref.shape[1]
    n_chunks = ks_ref.shape[2] // kchunk
    cblocks = kchunk // SEL_BLOCK
    pairs = cblocks // 2

    def scores_to(dst, c):
        start = pl.multiple_of(jnp.minimum(c, n_chunks - 1) * kchunk, kchunk)
        sc = _dot_nt(q_aug, ks_ref[0, 0, pl.ds(start, kchunk), :])
        d0 = i0 - c * cblocks
        tiles = [[jnp.clip(d0 + u - 2 * pm, -1, nd) + 1 for pm in range(pairs)] for u in range(Q_PAIR)]
        dst[...] = sc + jnp.concatenate(
            [jnp.concatenate([bt_ref[0, t, h] for t in tiles[u]], axis=1)
             for h in range(r) for u in range(Q_PAIR)], axis=0)

    def absorb(src, c, m, acc):
        start = pl.multiple_of(c * kchunk, kchunk)
        sc = src[...]
        m_new = jnp.maximum(m, jnp.max(sc, axis=-1, keepdims=True))
        p = jnp.exp2((sc - m_new).astype(BF16))
        acc = jnp.exp2(m - m_new) * acc + _dot(p, vs_ref[0, 0, pl.ds(start, kchunk), :])
        return m_new, acc

    def two_chunks(j, carry):
        scores_to(sb_ref, 2 * j + 1)
        carry = absorb(sa_ref, 2 * j, *carry)
        scores_to(sa_ref, 2 * j + 2)
        return absorb(sb_ref, 2 * j + 1, *carry)

    scores_to(sa_ref, 0)
    init = (jnp.full((rq, 1), NEG_INF, F32), jnp.zeros((rq, LANES), F32))
    _, acc_s = lax.fori_loop(0, (i0 + Q_PAIR - 1) // (2 * cblocks) + 1, two_chunks, init)
    o_s = _normalize(acc_s)

    gates = gate_ref[0]
    glane = lax.broadcasted_iota(jnp.int32, gates.shape, 1)
    outs = []
    for h in range(r):
        head = g * r + h
        gs = [jnp.sum(jnp.where(glane == br * N_HEADS + head, gates, 0.0), axis=-1, keepdims=True)
              for br in range(N_BRANCH)]
        outs.append(gs[0] * o_c[heads[h]] + gs[1] * o_s[heads[h]] + gs[2] * o_w[heads[h]])
    o_ref[0] = jnp.concatenate(
        [outs[h] + pltpu.roll(outs[h + 1], HEAD_DIM, 1) for h in range(0, r, 2)], axis=1)


def _nsa_attention(q, gates, kc_pad, vc_pad, ks_aug, vs, kw_pad, vw_pad, tables):
    b, _, s, _ = q.shape
    r = N_HEADS // N_KV_GROUPS
    rc, bt, wb, ov = tables
    n_qb = s // Q_BLOCK
    per_bg = lambda a: pl.BlockSpec((1, 1) + a.shape[2:], lambda bi, gi, qi: (bi, gi, 0, 0))
    per_g = lambda a: pl.BlockSpec((1,) + a.shape[1:], lambda bi, gi, qi: (gi,) + (0,) * (a.ndim - 1))
    pq = Q_PAIR * Q_BLOCK
    return pl.pallas_call(
        _attn_body,
        grid=(b, N_KV_GROUPS, n_qb // Q_PAIR),
        in_specs=[pl.BlockSpec((1, r, pq, LANES), lambda bi, gi, qi: (bi, gi, qi, 0)),
                  pl.BlockSpec((1, pq, LANES), lambda bi, gi, qi: (bi, qi, 0)),
                  per_bg(kc_pad), per_bg(vc_pad), per_bg(ks_aug), per_bg(vs), per_bg(kw_pad), per_bg(vw_pad),
                  per_g(rc), per_g(bt), per_g(wb),
                  pl.BlockSpec(ov.shape, lambda bi, gi, qi: (0, 0))],
        out_specs=pl.BlockSpec((1, pq, r * HEAD_DIM), lambda bi, gi, qi: (bi, qi, gi)),
        out_shape=jax.ShapeDtypeStruct((b, s, N_HEADS * HEAD_DIM), F32),
        scratch_shapes=[pltpu.VMEM((r * pq, min(KEY_CHUNK, s // 2)), F32)] * 2,
        compiler_params=_cparams(("arbitrary", "arbitrary", "arbitrary")),
        name="nsa_attention",
    )(q, gates, kc_pad, vc_pad, ks_aug, vs, kw_pad, vw_pad, rc, bt, wb, ov)


def _outproj_body(a_ref, h_ref, w_ref, o_ref):
    o_ref[...] = h_ref[...] + _dot(a_ref[...].astype(BF16), w_ref[...])


def _out_project(attn, h, w_out):
    n, d = h.shape
    hd = attn.shape[1]
    tm = min(512, n)
    return pl.pallas_call(
        _outproj_body,
        grid=(n // tm,),
        in_specs=[pl.BlockSpec((tm, hd), lambda i: (i, 0)), pl.BlockSpec((tm, d), lambda i: (i, 0)),
                  pl.BlockSpec((hd, d), lambda i: (0, 0))],
        out_specs=pl.BlockSpec((tm, d), lambda i: (i, 0)),
        out_shape=jax.ShapeDtypeStruct((n, d), F32),
        compiler_params=_cparams(("arbitrary",)),
        name="nsa_outproj",
    )(attn, h, w_out.astype(BF16))


def kernel(x, p, rel_bias, norm_mix, norm_ffn, a_w_in, a_ln_g, a_ln_b, a_w_s, a_b_s, a_w_out, kv_norm, kv_w, cmp_pos, cmp_w1, cmp_b1, cmp_w2, k_norm, b_w_in, b_b_gate, q_norm, b_w_out, router_w, router_b, e_w_gu, e_b_gu, e_w_d, e_b_d, ple_w, ple_gate_w, ple_norm):
    b, s, d = x.shape
    n = b * s
    pf = p.reshape(p.shape[0], n, p.shape[-1])

    def moe_ple(h, i):
        return _moe_ple_layer(h, norm_ffn[i], router_w[i], router_b[i], e_w_gu, e_b_gu[i], e_w_d, e_b_d[i], i,
                              pf[i], ple_w[i], ple_gate_w[i], ple_norm[i])

    h = _gmlp_layer(x.reshape(n, d), norm_mix[0], a_w_in[0], a_ln_g[0], a_ln_b[0], a_w_s[0], a_b_s[0], a_w_out[0])
    h = moe_ple(h, 0)

    h3 = h.reshape(b, s, d)
    kc, vc, ks_aug, vs, kw, vw = _kv_project(h3, kv_norm, kv_w, k_norm)
    kc_pad, vc_pad = _compress(kc, vc, cmp_pos, cmp_w1, cmp_b1, cmp_w2, k_norm[0])
    front = jnp.zeros((b, N_KV_GROUPS, WINDOW, LANES), BF16).at[..., HEAD_DIM].set(1.0)
    kw_pad = jnp.concatenate([front, kw, front[:, :, :2 * Q_BLOCK]], axis=2)
    vw_pad = jnp.pad(vw, ((0, 0), (0, 0), (WINDOW, 2 * Q_BLOCK), (0, 0)))

    q, gates = _q_project(h3, norm_mix[1], b_w_in[0], b_b_gate[0], q_norm[0])
    attn = _nsa_attention(q, gates, kc_pad, vc_pad, ks_aug, vs, kw_pad, vw_pad, _attn_tables(rel_bias, s))
    h = _out_project(attn.reshape(n, -1), h, b_w_out[0])
    h = moe_ple(h, 1)
    return h.reshape(b, s, d)
```

```python
import functools
import math

import numpy as np
import jax
import jax.numpy as jnp
from jax import lax
from jax.experimental import pallas as pl
from jax.experimental.pallas import tpu as pltpu

D_MODEL = 1024
GMLP_CHUNK = 128
GMLP_GROUPS = 8
N_HEADS = 16
HEAD_DIM = 64
N_KV_GROUPS = 4
N_BRANCH = 3
CMP_BLOCK = 32
CMP_STRIDE = 16
CMP_HIDDEN = 256
SEL_BLOCK = 64
SEL_TOP = 16
WINDOW = 512
Q_BLOCK = 64
N_KV_SLOTS = 6
REL_BUCKETS = 32
REL_MAX_DIST = 2048
N_EXPERTS = 32
TOP_K = 4
SWIGLU_LIMIT = 7.0
SWIGLU_ALPHA = 1.702
NORM_EPS = 1e-6
NEG_INF = -1e30
FORCE = 1e30
LOG2E = math.log2(math.e)

LANES = 128
MOE_ROWS = 512
KEY_CHUNK = 512
Q_PAIR = 4
VMEM_LIMIT = 56 * 1024 * 1024

F32 = jnp.float32
BF16 = jnp.bfloat16


def _cparams(sem):
    return pltpu.CompilerParams(dimension_semantics=sem, vmem_limit_bytes=VMEM_LIMIT)


def _rms(x, g):
    return x * lax.rsqrt(jnp.mean(x * x, axis=-1, keepdims=True) + NORM_EPS) * g


def _dot(a, b):
    return jnp.dot(a, b, preferred_element_type=F32)


def _dot_nt(a, b):
    return lax.dot_general(a, b, (((1,), (1,)), ((), ())), preferred_element_type=F32)


def _dot_split(a, b2):
    hi = a.astype(BF16)
    lo = (a - hi.astype(F32)).astype(BF16)
    return _dot(jnp.concatenate([hi, lo], axis=1), b2)


def _softmax2_rows(s):
    m = jnp.max(s, axis=-1, keepdims=True)
    p = jnp.exp2(s - m)
    return p / jnp.sum(p, axis=-1, keepdims=True)


def _normalize(acc):
    lane = lax.broadcasted_iota(jnp.int32, acc.shape, 1)
    denom = jnp.sum(jnp.where(lane == HEAD_DIM, acc, 0.0), axis=-1, keepdims=True)
    return jnp.where(lane < HEAD_DIM, acc / denom, 0.0)


def _head_lanes(x, idx):
    base = idx * HEAD_DIM // LANES * LANES
    y = x[:, base:base + LANES]
    return y if idx * HEAD_DIM == base else pltpu.roll(y, LANES - HEAD_DIM, 1)


def _low_lanes(x, fill):
    lane = lax.broadcasted_iota(jnp.int32, x.shape, 1)
    return jnp.where(lane < HEAD_DIM, x, fill)


def _argmax_first(x, ids, n):
    mx = jnp.max(x, axis=-1, keepdims=True)
    return mx, jnp.min(jnp.where(x == mx, ids, float(n)), axis=-1, keepdims=True)


def _gmlp_body(x_ref, nm_ref, win_ref, lng_ref, lnb_ref, ws_ref, bs_ref, wout_ref, o_ref):
    tm = x_ref.shape[0]
    gd = win_ref.shape[1] // 2
    gdim = gd // GMLP_GROUPS
    x = x_ref[...]
    xn = _rms(x, nm_ref[...]).astype(BF16)
    z = jax.nn.gelu(_dot(xn, win_ref[...]))
    u = z[:, :gd]
    v = z[:, gd:]
    mu = jnp.mean(v, axis=-1, keepdims=True)
    vc = v - mu
    vln = vc * lax.rsqrt(jnp.mean(vc * vc, axis=-1, keepdims=True) + NORM_EPS) * lng_ref[...] + lnb_ref[...]
    vb = vln.astype(BF16)
    row = lax.broadcasted_iota(jnp.int32, (GMLP_CHUNK, GMLP_CHUNK), 0)
    col = lax.broadcasted_iota(jnp.int32, (GMLP_CHUNK, GMLP_CHUNK), 1)
    causal = row >= col
    chunks = []
    for c in range(tm // GMLP_CHUNK):
        cols = []
        for g in range(GMLP_GROUPS):
            wsg = jnp.where(causal, ws_ref[g], 0.0).astype(BF16)
            vg = vb[c * GMLP_CHUNK:(c + 1) * GMLP_CHUNK, g * gdim:(g + 1) * gdim]
            cols.append(_dot(wsg, vg) + bs_ref[g])
        chunks.append(jnp.concatenate(cols, axis=1))
    mixed = jnp.concatenate(chunks, axis=0)
    gated = (u * mixed).astype(BF16)
    o_ref[...] = x + _dot(gated, wout_ref[...])


def _gmlp_layer(h, norm_g, w_in, ln_g, ln_b, w_s, b_s, w_out):
    n, d = h.shape
    gd2 = w_in.shape[1]
    gd = gd2 // 2
    tm = min(512, n)
    full = lambda *shape: pl.BlockSpec(shape, lambda i: (0,) * len(shape))
    return pl.pallas_call(
        _gmlp_body,
        grid=(n // tm,),
        in_specs=[pl.BlockSpec((tm, d), lambda i: (i, 0)),
                  full(1, d), full(d, gd2), full(1, gd), full(1, gd),
                  full(GMLP_GROUPS, GMLP_CHUNK, GMLP_CHUNK), full(GMLP_GROUPS, GMLP_CHUNK, 1),
                  full(gd, d)],
        out_specs=pl.BlockSpec((tm, d), lambda i: (i, 0)),
        out_shape=jax.ShapeDtypeStruct((n, d), F32),
        compiler_params=_cparams(("arbitrary",)),
        name="gmlp_layer",
    )(h, norm_g.reshape(1, d), w_in.astype(BF16), ln_g.reshape(1, gd), ln_b.reshape(1, gd),
      w_s, b_s.reshape(GMLP_GROUPS, GMLP_CHUNK, 1), w_out.astype(BF16))


def _route_body(h_ref, ng_ref, rw_ref, rb_ref, o_ref, cnt_ref, run_ref):
    i = pl.program_id(0)
    tm = h_ref.shape[0]

    @pl.when(i == 0)
    def _():
        run_ref[...] = jnp.zeros_like(run_ref)

    xn = _rms(h_ref[...], ng_ref[...])
    x_hi = xn.astype(BF16)
    x_lo = (xn - x_hi.astype(F32)).astype(BF16)
    logits = _dot(jnp.concatenate([x_hi, x_hi, x_lo], axis=1), rw_ref[...]) + rb_ref[...]
    eid = lax.broadcasted_iota(jnp.int32, logits.shape, 1).astype(F32)
    lane = lax.broadcasted_iota(jnp.int32, (tm, LANES), 1)
    work = logits
    vals, idxs = [], []
    for _ in range(TOP_K):
        mx, ix = _argmax_first(work, eid, N_EXPERTS)
        vals.append(mx)
        idxs.append(ix)
        work = jnp.where(eid == ix, -jnp.inf, work)
    exps = [jnp.exp(v - vals[0]) for v in vals]
    den = exps[0]
    for e in exps[1:]:
        den = den + e
    onehot = jnp.zeros(logits.shape, F32)
    for ix in idxs:
        onehot = onehot + (eid == ix).astype(F32)
    r = lax.broadcasted_iota(jnp.int32, (tm, tm), 0)
    c = lax.broadcasted_iota(jnp.int32, (tm, tm), 1)
    before = (r > c).astype(BF16)
    prefix = _dot(before, onehot.astype(BF16)) + run_ref[...]
    out = jnp.zeros((tm, LANES), F32)
    for k in range(TOP_K):
        rank = jnp.sum(jnp.where(eid == idxs[k], prefix, 0.0), axis=-1, keepdims=True)
        out = jnp.where(lane == k, idxs[k], out)
        out = jnp.where(lane == TOP_K + k, exps[k] / den, out)
        out = jnp.where(lane == 2 * TOP_K + k, rank, out)
    o_ref[...] = out
    run_ref[...] = run_ref[...] + jnp.sum(onehot, axis=0, keepdims=True)
    cnt_ref[...] = run_ref[...]


def _moe_route(h, norm_g, router_w, router_b):
    n, d = h.shape
    tm = min(512, n)
    w_hi = router_w.astype(BF16)
    w_lo = (router_w - w_hi.astype(F32)).astype(BF16)
    full = lambda *shape: pl.BlockSpec(shape, lambda i: (0,) * len(shape))
    return pl.pallas_call(
        _route_body,
        grid=(n // tm,),
        in_specs=[pl.BlockSpec((tm, d), lambda i: (i, 0)), full(1, d), full(3 * d, N_EXPERTS), full(1, N_EXPERTS)],
        out_specs=[pl.BlockSpec((tm, LANES), lambda i: (i, 0)), full(1, N_EXPERTS)],
        out_shape=[jax.ShapeDtypeStruct((n, LANES), F32), jax.ShapeDtypeStruct((1, N_EXPERTS), F32)],
        scratch_shapes=[pltpu.VMEM((1, N_EXPERTS), F32)],
        compiler_params=_cparams(("arbitrary",)),
        name="moe_route",
    )(h, norm_g.reshape(1, d), jnp.concatenate([w_hi, w_lo, w_hi], axis=0), router_b.reshape(1, N_EXPERTS))


def _dispatch_body(pad_ref, dest_ref, h_ref, ng_ref, xs_out, buf, zbuf, sem, zsem):
    i = pl.program_id(0)
    tm = h_ref.shape[0]

    @pl.when(i == 0)
    def _():
        zbuf[...] = jnp.zeros_like(zbuf)
        for e in range(2 * N_EXPERTS):
            first = pl.multiple_of(pad_ref[e], MOE_ROWS)
            zero = pltpu.make_async_copy(zbuf, xs_out.at[pl.ds(first, MOE_ROWS), :], zsem)
            zero.start()
            zero.wait()

    buf[...] = _rms(h_ref[...], ng_ref[...])
    for j in range(tm):
        for k in range(TOP_K):
            r = dest_ref[0, 0, j * TOP_K + k]
            pltpu.make_async_copy(buf.at[pl.ds(j, 1), :], xs_out.at[pl.ds(r, 1), :], sem).start(priority=k % 2)
    for _ in range(TOP_K):
        pltpu.make_async_copy(buf, xs_out.at[pl.ds(0, tm), :], sem).wait()


def _moe_dispatch(h, norm_g, dest, pad_rows, n_rows):
    n, d = h.shape
    tm = min(256, n)
    return pl.pallas_call(
        _dispatch_body,
        grid_spec=pltpu.PrefetchScalarGridSpec(
            num_scalar_prefetch=1, grid=(n // tm,),
            in_specs=[pl.BlockSpec((1, 1, tm * TOP_K), lambda i, pr: (i, 0, 0), memory_space=pltpu.SMEM),
                      pl.BlockSpec((tm, d), lambda i, pr: (i, 0)),
                      pl.BlockSpec((1, d), lambda i, pr: (0, 0))],
            out_specs=pl.BlockSpec(memory_space=pl.ANY),
            scratch_shapes=[pltpu.VMEM((tm, d), F32), pltpu.VMEM((MOE_ROWS, d), F32),
                            pltpu.SemaphoreType.DMA(()), pltpu.SemaphoreType.DMA(())]),
        out_shape=jax.ShapeDtypeStruct((n_rows, d), F32),
        compiler_params=_cparams(("arbitrary",)),
        name="moe_dispatch",
    )(pad_rows, dest.reshape(n // tm, 1, tm * TOP_K), h, norm_g.reshape(1, d))


def _expert_body(be_ref, nu_ref, xs_ref, wgu_ref, bgu_ref, wd_ref, bd_ref, ys_ref, wgu_bf, wd_bf):
    i = pl.program_id(0)
    ed = wd_ref.shape[2]

    @pl.when((i == 0) | (be_ref[i] != be_ref[jnp.maximum(i - 1, 0)]))
    def _():
        wgu_bf[...] = wgu_ref[0, 0].astype(BF16)
        wd_bf[...] = wd_ref[0, 0].astype(BF16)

    @pl.when(i < nu_ref[0])
    def _():
        x = xs_ref[...].astype(BF16)
        gu = _dot(x, wgu_bf[...]) + bgu_ref[0]
        gate = jnp.minimum(gu[:, :ed], SWIGLU_LIMIT)
        up = jnp.clip(gu[:, ed:], -SWIGLU_LIMIT, SWIGLU_LIMIT)
        glu = gate * jax.nn.sigmoid(gate * SWIGLU_ALPHA)
        ys_ref[...] = _dot(((up + 1.0) * glu).astype(BF16), wd_bf[...]) + bd_ref[0]

    @pl.when(i >= nu_ref[0])
    def _():
        ys_ref[...] = jnp.zeros_like(ys_ref)


def _moe_experts(xs, blk_expert, n_used, w_gu, b_gu, w_d, b_d, layer):
    n_rows, d = xs.shape
    ed = w_d.shape[2]
    n_blk = n_rows // MOE_ROWS
    return pl.pallas_call(
        _expert_body,
        grid_spec=pltpu.PrefetchScalarGridSpec(
            num_scalar_prefetch=2, grid=(n_blk,),
            in_specs=[pl.BlockSpec((MOE_ROWS, d), lambda i, be, nu: (jnp.maximum(jnp.minimum(i, nu[0] - 1), 0), 0)),
                      pl.BlockSpec((1, 1, d, 2 * ed), lambda i, be, nu: (layer, be[i], 0, 0)),
                      pl.BlockSpec((1, 1, 2 * ed), lambda i, be, nu: (be[i], 0, 0)),
                      pl.BlockSpec((1, 1, ed, d), lambda i, be, nu: (layer, be[i], 0, 0)),
                      pl.BlockSpec((1, 1, d), lambda i, be, nu: (be[i], 0, 0))],
            out_specs=pl.BlockSpec((MOE_ROWS, d), lambda i, be, nu: (i, 0)),
            scratch_shapes=[pltpu.VMEM((d, 2 * ed), BF16), pltpu.VMEM((ed, d), BF16)]),
        out_shape=jax.ShapeDtypeStruct((n_rows, d), F32),
        compiler_params=_cparams(("arbitrary",)),
        name="moe_experts",
    )(blk_expert, n_used, xs, w_gu, b_gu.reshape(N_EXPERTS, 1, 2 * ed), w_d, b_d.reshape(N_EXPERTS, 1, d))


def _combine_body(dest_ref, next_ref, rt_ref, h_ref, p_ref, pw_ref, pg_ref, pn_ref, ys_hbm, o_ref, buf, sem):
    i = pl.program_id(0)
    n = pl.num_programs(0)
    tm = h_ref.shape[0]
    slot = i % 2

    def row_copy(idx_ref, j, k, s):
        r = idx_ref[0, 0, j * TOP_K + k]
        return pltpu.make_async_copy(ys_hbm.at[pl.ds(r, 1), :], buf.at[s, k, pl.ds(j, 1), :], sem.at[s])

    def wait_tile(s):
        for k in range(TOP_K):
            pltpu.make_async_copy(ys_hbm.at[pl.ds(0, tm), :], buf.at[s, k], sem.at[s]).wait()

    @pl.when(i == 0)
    def _():
        def first(j, carry):
            for k in range(TOP_K):
                row_copy(dest_ref, j, k, 0).start()
            return carry
        lax.fori_loop(0, tm, first, 0)

    def step(s):
        wait_tile(s)
        for j in range(tm):
            for k in range(TOP_K):
                row_copy(next_ref, j, k, 1 - s).start(priority=k % 2)
        rt = rt_ref[...]
        h = h_ref[...]
        for k in range(TOP_K):
            h = h + rt[:, TOP_K + k:TOP_K + k + 1] * buf[s, k]
        emb = _dot(p_ref[...].astype(BF16), pw_ref[...])
        gate = jax.nn.sigmoid(_dot(_rms(h, pn_ref[...]).astype(BF16), pg_ref[...]))
        o_ref[...] = h + emb * gate

        @pl.when(i == n - 1)
        def _():
            wait_tile(1 - s)

    for s in range(2):
        pl.when(slot == s)(functools.partial(step, s))


def _moe_combine_ple(h, route, dest, ys, p, ple_w, ple_gate_w, ple_norm):
    n, d = h.shape
    pd = p.shape[1]
    tm = min(256, n)
    nt = n // tm
    full = lambda *shape: pl.BlockSpec(shape, lambda i: (0,) * len(shape))
    dest3 = dest.reshape(nt, 1, tm * TOP_K)
    return pl.pallas_call(
        _combine_body,
        grid=(nt,),
        in_specs=[pl.BlockSpec((1, 1, tm * TOP_K), lambda i: (i, 0, 0), memory_space=pltpu.SMEM),
                  pl.BlockSpec((1, 1, tm * TOP_K), lambda i: (jnp.minimum(i + 1, nt - 1), 0, 0),
                               memory_space=pltpu.SMEM),
                  pl.BlockSpec((tm, LANES), lambda i: (i, 0)),
                  pl.BlockSpec((tm, d), lambda i: (i, 0)),
                  pl.BlockSpec((tm, pd), lambda i: (i, 0)),
                  full(pd, d), full(d, d), full(1, d),
                  pl.BlockSpec(memory_space=pl.ANY)],
        out_specs=pl.BlockSpec((tm, d), lambda i: (i, 0)),
        out_shape=jax.ShapeDtypeStruct((n, d), F32),
        scratch_shapes=[pltpu.VMEM((2, TOP_K, tm, d), F32), pltpu.SemaphoreType.DMA((2,))],
        compiler_params=_cparams(("arbitrary",)),
        name="moe_combine_ple",
    )(dest3, dest3, route, h, p, ple_w.astype(BF16), ple_gate_w.astype(BF16), ple_norm.reshape(1, d), ys)


def _moe_ple_layer(h, norm_g, router_w, router_b, w_gu, b_gu, w_d, b_d, layer, p, ple_w, ple_gate_w, ple_norm):
    n, _ = h.shape
    route, counts = _moe_route(h, norm_g, router_w, router_b)
    counts = counts[0].astype(jnp.int32)
    pad_counts = (counts + MOE_ROWS - 1) // MOE_ROWS * MOE_ROWS
    pad_ends = jnp.cumsum(pad_counts)
    pad_starts = pad_ends - pad_counts
    top_idx = route[:, :TOP_K].astype(jnp.int32)
    rank = route[:, 2 * TOP_K:3 * TOP_K].astype(jnp.int32)
    dest = (pad_starts[top_idx] + rank).reshape(-1)
    n_blk = -(-(n * TOP_K) // MOE_ROWS) + N_EXPERTS
    blk_start = jnp.arange(n_blk, dtype=jnp.int32) * MOE_ROWS
    blk_expert = jnp.minimum(jnp.sum((pad_ends[None, :] <= blk_start[:, None]).astype(jnp.int32), axis=1),
                             N_EXPERTS - 1)
    n_used = (pad_ends[-1:] // MOE_ROWS).astype(jnp.int32)
    tail = jnp.minimum(pad_ends[-1] + jnp.arange(N_EXPERTS, dtype=jnp.int32) * MOE_ROWS, (n_blk - 1) * MOE_ROWS)
    clear = jnp.concatenate([jnp.maximum(pad_ends - MOE_ROWS, 0), tail]).astype(jnp.int32)
    xs = _moe_dispatch(h, norm_g, dest, clear, n_blk * MOE_ROWS)
    ys = _moe_experts(xs, blk_expert, n_used, w_gu, b_gu, w_d, b_d, layer)
    return _moe_combine_ple(h, route, dest, ys, p, ple_w, ple_gate_w, ple_norm)


def _kv_body(h_ref, ng_ref, w_ref, seg_ref, kn_ref, kc_ref, vc_ref, ks_ref, vs_ref, kw_ref, vw_ref):
    ts = h_ref.shape[1]
    gw = N_KV_GROUPS * HEAD_DIM
    st = pl.program_id(1)
    hn = _rms(h_ref[0], ng_ref[...]).astype(BF16)
    kv = _dot(hn, w_ref[...])

    def knorm(x, j):
        ms = _dot_split(x * x, seg_ref[...])
        return x * lax.rsqrt(ms + NORM_EPS) * kn_ref[j]

    k_c, v_c = kv[:, 0:gw], kv[:, gw:2 * gw]
    k_s, v_s = knorm(kv[:, 2 * gw:3 * gw], 1), kv[:, 3 * gw:4 * gw]
    k_w, v_w = knorm(kv[:, 4 * gw:5 * gw], 2), kv[:, 5 * gw:6 * gw]
    tok = st * ts + lax.broadcasted_iota(jnp.int32, (ts, LANES), 0)
    blk = lax.broadcasted_iota(jnp.int32, (ts, LANES), 1)
    onehot = (tok // SEL_BLOCK == blk).astype(BF16)
    one_col = (blk == HEAD_DIM).astype(F32)
    for g in range(N_KV_GROUPS):
        kc_ref[0, g] = _head_lanes(k_c, g)[:, :HEAD_DIM]
        vc_ref[0, g] = _head_lanes(v_c, g)[:, :HEAD_DIM]
        ks_ref[0, g] = jnp.concatenate([_low_lanes(_head_lanes(k_s, g), 0.0).astype(BF16), onehot], axis=1)
        vs_ref[0, g] = _low_lanes(_head_lanes(v_s, g), one_col).astype(BF16)
        kw_ref[0, g] = _low_lanes(_head_lanes(k_w, g), 0.0).astype(BF16)
        vw_ref[0, g] = _low_lanes(_head_lanes(v_w, g), one_col).astype(BF16)


def _kv_project(h3, kv_norm, kv_w, k_norm):
    b, s, d = h3.shape
    gw = N_KV_GROUPS * HEAD_DIM
    ts = min(512, s)
    seg = jnp.asarray(np.tile(np.kron(np.eye(N_KV_GROUPS), np.full((HEAD_DIM, HEAD_DIM), 1.0 / HEAD_DIM)),
                              (2, 1)), F32).astype(BF16)
    kn = jnp.tile(k_norm, (1, N_KV_GROUPS)).reshape(N_BRANCH, 1, gw)
    full = lambda *shape: pl.BlockSpec(shape, lambda bi, si: (0,) * len(shape))
    hd = lambda w: pl.BlockSpec((1, N_KV_GROUPS, ts, w), lambda bi, si: (bi, 0, si, 0))
    sds = lambda w, dt: jax.ShapeDtypeStruct((b, N_KV_GROUPS, s, w), dt)
    return pl.pallas_call(
        _kv_body,
        grid=(b, s // ts),
        in_specs=[pl.BlockSpec((1, ts, d), lambda bi, si: (bi, si, 0)), full(1, d), full(d, N_KV_SLOTS * gw),
                  full(2 * gw, gw), full(N_BRANCH, 1, gw)],
        out_specs=[hd(HEAD_DIM), hd(HEAD_DIM), hd(2 * LANES), hd(LANES), hd(LANES), hd(LANES)],
        out_shape=[sds(HEAD_DIM, F32), sds(HEAD_DIM, F32), sds(2 * LANES, BF16), sds(LANES, BF16),
                   sds(LANES, BF16), sds(LANES, BF16)],
        compiler_params=_cparams(("arbitrary", "arbitrary")),
        name="kv_project",
    )(h3, kv_norm.reshape(1, d), kv_w.astype(BF16), seg, kn)


def _compress_body(kc_ref, vc_ref, pos_ref, w1_ref, b1_ref, w2_ref, kn_ref, ko_ref, vo_ref):
    nc = kc_ref.shape[2]
    half = kc_ref.shape[3]
    row = lax.broadcasted_iota(jnp.int32, (nc, 1), 0)
    valid = row < nc - 1

    def compress(c, j):
        a = _dot((c + pos_ref[j, 0]).astype(BF16), w1_ref[j, 0])
        bm = _dot((c + pos_ref[j, 1]).astype(BF16), w1_ref[j, 1])
        nxt = pltpu.roll(bm, nc - 1, 0)
        hid = jax.nn.gelu(a + nxt + b1_ref[j])
        return _dot(hid.astype(BF16), w2_ref[j])

    kraw = compress(kc_ref[0, 0], 0)
    ms = jnp.sum(kraw * kraw, axis=-1, keepdims=True) * (1.0 / HEAD_DIM)
    kcmp = kraw * lax.rsqrt(ms + NORM_EPS) * kn_ref[...]
    vcmp = jnp.where(valid, compress(vc_ref[0, 0], 1), 0.0)
    lane = lax.broadcasted_iota(jnp.int32, (nc, LANES), 1)
    flag = (lane == HEAD_DIM).astype(F32)
    ko_ref[0, 0, 0:nc, :] = flag
    ko_ref[0, 0, nc:2 * nc, :] = jnp.where(valid, kcmp, flag)
    vo_ref[0, 0, 0:nc, :] = flag
    vo_ref[0, 0, nc:2 * nc, :] = jnp.where(lane == HEAD_DIM, 1.0, vcmp)


def _compress(kc, vc, cmp_pos, cmp_w1, cmp_b1, cmp_w2, k_norm0):
    b, g, s, dh = kc.shape
    nc = s // CMP_STRIDE
    half = CMP_STRIDE * dh
    kc_r = kc.reshape(b, g, nc, half)
    vc_r = vc.reshape(b, g, nc, half)
    pos = cmp_pos.reshape(2, 2, 1, half)
    w1 = cmp_w1.reshape(2, 2, half, CMP_HIDDEN).astype(BF16)
    w2 = jnp.pad(cmp_w2, ((0, 0), (0, 0), (0, LANES - dh))).astype(BF16)
    kn = jnp.pad(k_norm0, (0, LANES - dh)).reshape(1, LANES)
    full = lambda *shape: pl.BlockSpec(shape, lambda bi, gi: (0,) * len(shape))
    blk = pl.BlockSpec((1, 1, nc, half), lambda bi, gi: (bi, gi, 0, 0))
    out = pl.BlockSpec((1, 1, 2 * nc, LANES), lambda bi, gi: (bi, gi, 0, 0))
    return pl.pallas_call(
        _compress_body,
        grid=(b, g),
        in_specs=[blk, blk, full(2, 2, 1, half), full(2, 2, half, CMP_HIDDEN), full(2, 1, CMP_HIDDEN),
                  full(2, CMP_HIDDEN, LANES), full(1, LANES)],
        out_specs=[out, out],
        out_shape=[jax.ShapeDtypeStruct((b, g, 2 * nc, LANES), F32)] * 2,
        compiler_params=_cparams(("arbitrary", "arbitrary")),
        name="kv_compress",
    )(kc_r, vc_r, pos, w1, cmp_b1.reshape(2, 1, CMP_HIDDEN), w2, kn)


def _qproj_body(h_ref, ng_ref, w_ref, bg_ref, ind_ref, indt_ref, qn_ref, q_ref, gate_ref):
    hd = N_HEADS * HEAD_DIM
    xn = _rms(h_ref[0], ng_ref[...]).astype(BF16)
    proj = _dot(xn, w_ref[...])
    q = proj[:, :hd]
    ms = _dot_split(q * q, ind_ref[...]) * (1.0 / HEAD_DIM)
    scale = _dot_split(lax.rsqrt(ms + NORM_EPS), indt_ref[...])
    qn = q * scale * qn_ref[...] * (HEAD_DIM ** -0.5 * LOG2E)
    lane = lax.broadcasted_iota(jnp.int32, (q.shape[0], LANES), 1)
    fill = jnp.where(lane == HEAD_DIM, NEG_INF, 0.0)
    for h in range(N_HEADS):
        q_ref[0, h] = _low_lanes(_head_lanes(qn, h), fill).astype(BF16)
    gate_ref[0] = jax.nn.sigmoid(proj[:, hd:] + bg_ref[...])


def _q_project(h3, norm_g, w_in, b_gate, q_norm):
    b, s, d = h3.shape
    hd = N_HEADS * HEAD_DIM
    ng = N_BRANCH * N_HEADS
    ts = min(512, s)
    w = jnp.pad(w_in, ((0, 0), (0, LANES - ng))).astype(BF16)
    bg = jnp.pad(b_gate, (0, LANES - ng)).reshape(1, LANES)
    ind = np.zeros((hd, LANES), np.float32)
    ind[np.arange(hd), np.arange(hd) // HEAD_DIM] = 1.0
    full = lambda *shape: pl.BlockSpec(shape, lambda bi, si: (0,) * len(shape))
    return pl.pallas_call(
        _qproj_body,
        grid=(b, s // ts),
        in_specs=[pl.BlockSpec((1, ts, d), lambda bi, si: (bi, si, 0)), full(1, d), full(d, hd + LANES),
                  full(1, LANES), full(2 * hd, LANES), full(2 * LANES, hd), full(1, hd)],
        out_specs=[pl.BlockSpec((1, N_HEADS, ts, LANES), lambda bi, si: (bi, 0, si, 0)),
                   pl.BlockSpec((1, ts, LANES), lambda bi, si: (bi, si, 0))],
        out_shape=[jax.ShapeDtypeStruct((b, N_HEADS, s, LANES), BF16),
                   jax.ShapeDtypeStruct((b, s, LANES), F32)],
        compiler_params=_cparams(("arbitrary", "arbitrary")),
        name="nsa_qproj",
    )(h3, norm_g.reshape(1, d), w, bg, jnp.asarray(np.tile(ind, (2, 1))).astype(BF16),
      jnp.asarray(np.tile(ind.T, (2, 1))).astype(BF16), jnp.tile(q_norm, N_HEADS).reshape(1, hd))


def _t5_bucket_np(dist):
    n = np.maximum(dist, 0)
    max_exact = REL_BUCKETS // 2
    nf = np.maximum(n, 1).astype(np.float64)
    large = max_exact + (np.log(nf / max_exact) / math.log(REL_MAX_DIST / max_exact)
                         * (REL_BUCKETS - max_exact)).astype(np.int64)
    return np.where(n < max_exact, n, np.minimum(large, REL_BUCKETS - 1))


def _bias_table(rel_bias, dist, valid):
    r = N_HEADS // N_KV_GROUPS
    tab = rel_bias.astype(F32).T.reshape(N_KV_GROUPS, r, REL_BUCKETS)
    onehot = jnp.asarray(_t5_bucket_np(dist)[..., None] == np.arange(REL_BUCKETS), F32)
    bias = jnp.einsum('xqln,grn->gxrql', onehot, tab, precision=lax.Precision.HIGHEST) * LOG2E
    return jnp.where(jnp.asarray(valid)[None, :, None, :, :], bias, NEG_INF)


def _n_delta(seq):
    d = np.arange(seq + SEL_BLOCK)
    bk = _t5_bucket_np(d)
    change = np.nonzero(bk[1:] != bk[:-1])[0]
    d_const = int(change[-1]) + 1 if change.size else 0
    return -(-(d_const + SEL_BLOCK - 1) // SEL_BLOCK) + 1


def _attn_tables(rel_bias, seq):
    r = N_HEADS // N_KV_GROUPS
    qi = np.arange(Q_BLOCK)[:, None]
    rows = lambda t: jnp.transpose(t, (0, 2, 1, 3, 4)).reshape(N_KV_GROUPS, r * Q_PAIR * Q_BLOCK, t.shape[-1])
    nc = seq // CMP_STRIDE
    j = np.arange(nc)[None, :]
    dist_c = np.stack([qi - (CMP_BLOCK - 1) - Q_BLOCK * (Q_PAIR - u) + CMP_STRIDE * (nc - j)
                       for u in range(Q_PAIR)])
    rc = rows(_bias_table(rel_bias, dist_c, dist_c >= 0))
    nd = _n_delta(seq)
    delta = np.arange(-1, nd + 1)[:, None, None]
    kj = np.arange(2 * SEL_BLOCK)[None, None, :]
    dist_s = SEL_BLOCK * (delta - kj // SEL_BLOCK) + qi[None] - kj % SEL_BLOCK
    bt = _bias_table(rel_bias, dist_s, dist_s >= 0)
    jw = np.arange(WINDOW + 4 * Q_BLOCK)[None, :]
    dist_w = np.stack([Q_BLOCK * u + qi - jw + WINDOW for u in range(Q_PAIR)])
    wb = rows(_bias_table(rel_bias, dist_w, (dist_w >= 0) & (dist_w < WINDOW)))
    n_sel = seq // SEL_BLOCK
    cs = np.arange(nc) * CMP_STRIDE
    ss = np.arange(n_sel) * SEL_BLOCK
    ov = np.clip(np.minimum(cs[:, None] + CMP_BLOCK, ss[None, :] + SEL_BLOCK)
                 - np.maximum(cs[:, None], ss[None, :]), 0, None) / CMP_BLOCK
    return rc, bt, wb, jnp.asarray(ov.T.astype(np.float32)).astype(BF16)


def _attn_body(q_ref, gate_ref, kc_ref, vc_ref, ks_ref, vs_ref, kw_ref, vw_ref, rc_ref, bt_ref, wb_ref, ovt_ref,
               o_ref, sa_ref, sb_ref):
    g = pl.program_id(1)
    i0 = Q_PAIR * pl.program_id(2)
    r = q_ref.shape[1]
    pq = Q_PAIR * Q_BLOCK
    rq = r * pq
    nc = rc_ref.shape[2]
    n_sel = ovt_ref.shape[0]
    nd = bt_ref.shape[1] - 2
    wl = wb_ref.shape[2]
    per = SEL_BLOCK // CMP_STRIDE

    heads = [slice(h * pq, (h + 1) * pq) for h in range(r)]
    q_pad = q_ref[0].reshape(rq, LANES)

    end = pl.multiple_of(per * (i0 + Q_PAIR), Q_PAIR * per)
    kcw = kc_ref[0, 0, pl.ds(end, nc), :].astype(BF16)
    vcw = vc_ref[0, 0, pl.ds(end, nc), :].astype(BF16)
    qpos = i0 * Q_BLOCK + lax.broadcasted_iota(jnp.int32, (rq, 1), 0) % pq
    sc = _dot_nt(q_pad, kcw) + rc_ref[0]
    p_c = jnp.exp2((sc - jnp.max(sc, axis=-1, keepdims=True)).astype(BF16))
    o_c = _normalize(_dot(p_c, vcw)) * (qpos >= CMP_BLOCK - 1).astype(F32)

    imp_r = _dot_nt(ovt_ref[...], p_c)
    inv_l = 1.0 / _dot_nt(jnp.ones((8, nc), BF16), p_c)[0:1]
    imp = imp_r[:, heads[0]] * inv_l[:, heads[0]]
    for h in range(1, r):
        imp = imp + imp_r[:, heads[h]] * inv_l[:, heads[h]]
    tpos = i0 * Q_BLOCK + lax.broadcasted_iota(jnp.int32, (1, pq), 1)
    imp = imp * (tpos >= CMP_BLOCK - 1).astype(F32)

    anchor = (jnp.max(imp, axis=(0, 1), keepdims=True) * 0.0).astype(BF16)
    ws = pl.multiple_of(i0 * Q_BLOCK, pq)
    sw = _dot_nt(q_pad + anchor, kw_ref[0, 0, pl.ds(ws, wl), :]) + wb_ref[0]
    p_w = jnp.exp2((sw - jnp.max(sw, axis=-1, keepdims=True)).astype(BF16))
    o_w = _normalize(_dot(p_w, vw_ref[0, 0, pl.ds(ws, wl), :]))

    shift = i0 + Q_PAIR
    blk_rel = lax.broadcasted_iota(jnp.int32, (n_sel, pq), 0)
    blk = blk_rel + shift - n_sel
    cur = i0 + lax.broadcasted_iota(jnp.int32, (n_sel, pq), 1) // Q_BLOCK
    forced = (blk == 0) | (blk == cur) | (blk == cur - 1)
    imp = jnp.where(forced, FORCE, jnp.where(blk > cur, NEG_INF, imp))
    imp = jnp.where(blk < 0, -jnp.inf, imp)
    ids = blk_rel.astype(F32)
    sel = jnp.zeros((n_sel, pq), F32)
    for _ in range(min(SEL_TOP, n_sel)):
        mx = jnp.max(imp, axis=0, keepdims=True)
        ix = jnp.min(jnp.where(imp == mx, ids, float(n_sel)), axis=0, keepdims=True)
        hit = ids == ix
        sel = jnp.where(hit, 1.0, sel)
        imp = jnp.where(hit, -jnp.inf, imp)
    unsel = jnp.where((sel > 0.0) & (blk >= 0), 0.0, NEG_INF)
    unsel = pltpu.roll(unsel.T, shift % n_sel, 1).astype(BF16)
    if n_sel < LANES:
        unsel = jnp.concatenate([unsel, jnp.zeros((pq, LANES - n_sel), BF16)], axis=1)

    q_aug = jnp.concatenate([q_pad, jnp.concatenate([unsel] * r, axis=0)], axis=1)
    kchunk = sa_ref.shape[1]
    n_chunks = ks_ref.shape[2] // kchunk
    cblocks = kchunk // SEL_BLOCK
    pairs = cblocks // 2

    def scores_to(dst, c):
        start = pl.multiple_of(jnp.minimum(c, n_chunks - 1) * kchunk, kchunk)
        sc = _dot_nt(q_aug, ks_ref[0, 0, pl.ds(start, kchunk), :])
        d0 = i0 - c * cblocks
        tiles = [[jnp.clip(d0 + u - 2 * pm, -1, nd) + 1 for pm in range(pairs)] for u in range(Q_PAIR)]
        dst[...] = sc + jnp.concatenate(
            [jnp.concatenate([bt_ref[0, t, h] for t in tiles[u]], axis=1)
             for h in range(r) for u in range(Q_PAIR)], axis=0)

    def absorb(src, c, m, acc):
        start = pl.multiple_of(c * kchunk, kchunk)
        sc = src[...]
        m_new = jnp.maximum(m, jnp.max(sc, axis=-1, keepdims=True))
        p = jnp.exp2((sc - m_new).astype(BF16))
        acc = jnp.exp2(m - m_new) * acc + _dot(p, vs_ref[0, 0, pl.ds(start, kchunk), :])
        return m_new, acc

    def two_chunks(j, carry):
        scores_to(sb_ref, 2 * j + 1)
        carry = absorb(sa_ref, 2 * j, *carry)
        scores_to(sa_ref, 2 * j + 2)
        return absorb(sb_ref, 2 * j + 1, *carry)

    scores_to(sa_ref, 0)
    init = (jnp.full((rq, 1), NEG_INF, F32), jnp.zeros((rq, LANES), F32))
    _, acc_s = lax.fori_loop(0, (i0 + Q_PAIR - 1) // (2 * cblocks) + 1, two_chunks, init)
    o_s = _normalize(acc_s)

    gates = gate_ref[0]
    glane = lax.broadcasted_iota(jnp.int32, gates.shape, 1)
    outs = []
    for h in range(r):
        head = g * r + h
        gs = [jnp.sum(jnp.where(glane == br * N_HEADS + head, gates, 0.0), axis=-1, keepdims=True)
              for br in range(N_BRANCH)]
        outs.append(gs[0] * o_c[heads[h]] + gs[1] * o_s[heads[h]] + gs[2] * o_w[heads[h]])
    o_ref[0] = jnp.concatenate(
        [outs[h] + pltpu.roll(outs[h + 1], HEAD_DIM, 1) for h in range(0, r, 2)], axis=1)


def _nsa_attention(q, gates, kc_pad, vc_pad, ks_aug, vs, kw_pad, vw_pad, tables):
    b, _, s, _ = q.shape
    r = N_HEADS // N_KV_GROUPS
    rc, bt, wb, ov = tables
    n_qb = s // Q_BLOCK
    per_bg = lambda a: pl.BlockSpec((1, 1) + a.shape[2:], lambda bi, gi, qi: (bi, gi, 0, 0))
    per_g = lambda a: pl.BlockSpec((1,) + a.shape[1:], lambda bi, gi, qi: (gi,) + (0,) * (a.ndim - 1))
    pq = Q_PAIR * Q_BLOCK
    return pl.pallas_call(
        _attn_body,
        grid=(b, N_KV_GROUPS, n_qb // Q_PAIR),
        in_specs=[pl.BlockSpec((1, r, pq, LANES), lambda bi, gi, qi: (bi, gi, qi, 0)),
                  pl.BlockSpec((1, pq, LANES), lambda bi, gi, qi: (bi, qi, 0)),
                  per_bg(kc_pad), per_bg(vc_pad), per_bg(ks_aug), per_bg(vs), per_bg(kw_pad), per_bg(vw_pad),
                  per_g(rc), per_g(bt), per_g(wb),
                  pl.BlockSpec(ov.shape, lambda bi, gi, qi: (0, 0))],
        out_specs=pl.BlockSpec((1, pq, r * HEAD_DIM), lambda bi, gi, qi: (bi, qi, gi)),
        out_shape=jax.ShapeDtypeStruct((b, s, N_HEADS * HEAD_DIM), F32),
        scratch_shapes=[pltpu.VMEM((r * pq, min(KEY_CHUNK, s // 2)), F32)] * 2,
        compiler_params=_cparams(("arbitrary", "arbitrary", "arbitrary")),
        name="nsa_attention",
    )(q, gates, kc_pad, vc_pad, ks_aug, vs, kw_pad, vw_pad, rc, bt, wb, ov)


def _outproj_body(a_ref, h_ref, w_ref, o_ref):
    o_ref[...] = h_ref[...] + _dot(a_ref[...].astype(BF16), w_ref[...])


def _out_project(attn, h, w_out):
    n, d = h.shape
    hd = attn.shape[1]
    tm = min(512, n)
    return pl.pallas_call(
        _outproj_body,
        grid=(n // tm,),
        in_specs=[pl.BlockSpec((tm, hd), lambda i: (i, 0)), pl.BlockSpec((tm, d), lambda i: (i, 0)),
                  pl.BlockSpec((hd, d), lambda i: (0, 0))],
        out_specs=pl.BlockSpec((tm, d), lambda i: (i, 0)),
        out_shape=jax.ShapeDtypeStruct((n, d), F32),
        compiler_params=_cparams(("arbitrary",)),
        name="nsa_outproj",
    )(attn, h, w_out.astype(BF16))


def kernel(x, p, rel_bias, norm_mix, norm_ffn, a_w_in, a_ln_g, a_ln_b, a_w_s, a_b_s, a_w_out, kv_norm, kv_w, cmp_pos, cmp_w1, cmp_b1, cmp_w2, k_norm, b_w_in, b_b_gate, q_norm, b_w_out, router_w, router_b, e_w_gu, e_b_gu, e_w_d, e_b_d, ple_w, ple_gate_w, ple_norm):
    b, s, d = x.shape
    n = b * s
    pf = p.reshape(p.shape[0], n, p.shape[-1])

    def moe_ple(h, i):
        return _moe_ple_layer(h, norm_ffn[i], router_w[i], router_b[i], e_w_gu, e_b_gu[i], e_w_d, e_b_d[i], i,
                              pf[i], ple_w[i], ple_gate_w[i], ple_norm[i])

    h = _gmlp_layer(x.reshape(n, d), norm_mix[0], a_w_in[0], a_ln_g[0], a_ln_b[0], a_w_s[0], a_b_s[0], a_w_out[0])
    h = moe_ple(h, 0)

    h3 = h.reshape(b, s, d)
    kc, vc, ks_aug, vs, kw, vw = _kv_project(h3, kv_norm, kv_w, k_norm)
    kc_pad, vc_pad = _compress(kc, vc, cmp_pos, cmp_w1, cmp_b1, cmp_w2, k_norm[0])
    front = jnp.zeros((b, N_KV_GROUPS, WINDOW, LANES), BF16).at[..., HEAD_DIM].set(1.0)
    kw_pad = jnp.concatenate([front, kw, front[:, :, :2 * Q_BLOCK]], axis=2)
    vw_pad = jnp.pad(vw, ((0, 0), (0, 0), (WINDOW, 2 * Q_BLOCK), (0, 0)))

    q, gates = _q_project(h3, norm_mix[1], b_w_in[0], b_b_gate[0], q_norm[0])
    attn = _nsa_attention(q, gates, kc_pad, vc_pad, ks_aug, vs, kw_pad, vw_pad, _attn_tables(rel_bias, s))
    h = _out_project(attn.reshape(n, -1), h, b_w_out[0])
    h = moe_ple(h, 1)
    return h.reshape(b, s, d)
```

```python
import functools
import math

import numpy as np
import jax
import jax.numpy as jnp
from jax import lax
from jax.experimental import pallas as pl
from jax.experimental.pallas import tpu as pltpu

D_MODEL = 1024
GMLP_CHUNK = 128
GMLP_GROUPS = 8
N_HEADS = 16
HEAD_DIM = 64
N_KV_GROUPS = 4
N_BRANCH = 3
CMP_BLOCK = 32
CMP_STRIDE = 16
CMP_HIDDEN = 256
SEL_BLOCK = 64
SEL_TOP = 16
WINDOW = 512
Q_BLOCK = 64
N_KV_SLOTS = 6
REL_BUCKETS = 32
REL_MAX_DIST = 2048
N_EXPERTS = 32
TOP_K = 4
SWIGLU_LIMIT = 7.0
SWIGLU_ALPHA = 1.702
NORM_EPS = 1e-6
NEG_INF = -1e30
FORCE = 1e30
LOG2E = math.log2(math.e)

LANES = 128
MOE_ROWS = 512
KEY_CHUNK = 512
Q_PAIR = 4
VMEM_LIMIT = 56 * 1024 * 1024

F32 = jnp.float32
BF16 = jnp.bfloat16


def _cparams(sem):
    return pltpu.CompilerParams(dimension_semantics=sem, vmem_limit_bytes=VMEM_LIMIT)


def _rms(x, g):
    return x * lax.rsqrt(jnp.mean(x * x, axis=-1, keepdims=True) + NORM_EPS) * g


def _dot(a, b):
    return jnp.dot(a, b, preferred_element_type=F32)


def _dot_nt(a, b):
    return lax.dot_general(a, b, (((1,), (1,)), ((), ())), preferred_element_type=F32)


def _dot_split(a, b2):
    hi = a.astype(BF16)
    lo = (a - hi.astype(F32)).astype(BF16)
    return _dot(jnp.concatenate([hi, lo], axis=1), b2)


def _softmax2_rows(s):
    m = jnp.max(s, axis=-1, keepdims=True)
    p = jnp.exp2(s - m)
    return p / jnp.sum(p, axis=-1, keepdims=True)


def _normalize(acc):
    lane = lax.broadcasted_iota(jnp.int32, acc.shape, 1)
    denom = jnp.sum(jnp.where(lane == HEAD_DIM, acc, 0.0), axis=-1, keepdims=True)
    return jnp.where(lane < HEAD_DIM, acc / denom, 0.0)


def _head_lanes(x, idx):
    base = idx * HEAD_DIM // LANES * LANES
    y = x[:, base:base + LANES]
    return y if idx * HEAD_DIM == base else pltpu.roll(y, LANES - HEAD_DIM, 1)


def _low_lanes(x, fill):
    lane = lax.broadcasted_iota(jnp.int32, x.shape, 1)
    return jnp.where(lane < HEAD_DIM, x, fill)


def _argmax_first(x, ids, n):
    mx = jnp.max(x, axis=-1, keepdims=True)
    return mx, jnp.min(jnp.where(x == mx, ids, float(n)), axis=-1, keepdims=True)


def _gmlp_body(x_ref, nm_ref, win_ref, lng_ref, lnb_ref, ws_ref, bs_ref, wout_ref, o_ref):
    tm = x_ref.shape[0]
    gd = win_ref.shape[1] // 2
    gdim = gd // GMLP_GROUPS
    x = x_ref[...]
    xn = _rms(x, nm_ref[...]).astype(BF16)
    z = jax.nn.gelu(_dot(xn, win_ref[...]))
    u = z[:, :gd]
    v = z[:, gd:]
    mu = jnp.mean(v, axis=-1, keepdims=True)
    vc = v - mu
    vln = vc * lax.rsqrt(jnp.mean(vc * vc, axis=-1, keepdims=True) + NORM_EPS) * lng_ref[...] + lnb_ref[...]
    vb = vln.astype(BF16)
    row = lax.broadcasted_iota(jnp.int32, (GMLP_CHUNK, GMLP_CHUNK), 0)
    col = lax.broadcasted_iota(jnp.int32, (GMLP_CHUNK, GMLP_CHUNK), 1)
    causal = row >= col
    chunks = []
    for c in range(tm // GMLP_CHUNK):
        cols = []
        for g in range(GMLP_GROUPS):
            wsg = jnp.where(causal, ws_ref[g], 0.0).astype(BF16)
            vg = vb[c * GMLP_CHUNK:(c + 1) * GMLP_CHUNK, g * gdim:(g + 1) * gdim]
            cols.append(_dot(wsg, vg) + bs_ref[g])
        chunks.append(jnp.concatenate(cols, axis=1))
    mixed = jnp.concatenate(chunks, axis=0)
    gated = (u * mixed).astype(BF16)
    o_ref[...] = x + _dot(gated, wout_ref[...])


def _gmlp_layer(h, norm_g, w_in, ln_g, ln_b, w_s, b_s, w_out):
    n, d = h.shape
    gd2 = w_in.shape[1]
    gd = gd2 // 2
    tm = min(512, n)
    full = lambda *shape: pl.BlockSpec(shape, lambda i: (0,) * len(shape))
    return pl.pallas_call(
        _gmlp_body,
        grid=(n // tm,),
        in_specs=[pl.BlockSpec((tm, d), lambda i: (i, 0)),
                  full(1, d), full(d, gd2), full(1, gd), full(1, gd),
                  full(GMLP_GROUPS, GMLP_CHUNK, GMLP_CHUNK), full(GMLP_GROUPS, GMLP_CHUNK, 1),
                  full(gd, d)],
        out_specs=pl.BlockSpec((tm, d), lambda i: (i, 0)),
        out_shape=jax.ShapeDtypeStruct((n, d), F32),
        compiler_params=_cparams(("arbitrary",)),
        name="gmlp_layer",
    )(h, norm_g.reshape(1, d), w_in.astype(BF16), ln_g.reshape(1, gd), ln_b.reshape(1, gd),
      w_s, b_s.reshape(GMLP_GROUPS, GMLP_CHUNK, 1), w_out.astype(BF16))


def _route_body(h_ref, ng_ref, rw_ref, rb_ref, o_ref, cnt_ref, run_ref):
    i = pl.program_id(0)
    tm = h_ref.shape[0]

    @pl.when(i == 0)
    def _():
        run_ref[...] = jnp.zeros_like(run_ref)

    xn = _rms(h_ref[...], ng_ref[...])
    x_hi = xn.astype(BF16)
    x_lo = (xn - x_hi.astype(F32)).astype(BF16)
    logits = _dot(jnp.concatenate([x_hi, x_hi, x_lo], axis=1), rw_ref[...]) + rb_ref[...]
    eid = lax.broadcasted_iota(jnp.int32, logits.shape, 1).astype(F32)
    lane = lax.broadcasted_iota(jnp.int32, (tm, LANES), 1)
    work = logits
    vals, idxs = [], []
    for _ in range(TOP_K):
        mx, ix = _argmax_first(work, eid, N_EXPERTS)
        vals.append(mx)
        idxs.append(ix)
        work = jnp.where(eid == ix, -jnp.inf, work)
    exps = [jnp.exp(v - vals[0]) for v in vals]
    den = exps[0]
    for e in exps[1:]:
        den = den + e
    onehot = jnp.zeros(logits.shape, F32)
    for ix in idxs:
        onehot = onehot + (eid == ix).astype(F32)
    r = lax.broadcasted_iota(jnp.int32, (tm, tm), 0)
    c = lax.broadcasted_iota(jnp.int32, (tm, tm), 1)
    before = (r > c).astype(BF16)
    prefix = _dot(before, onehot.astype(BF16)) + run_ref[...]
    out = jnp.zeros((tm, LANES), F32)
    for k in range(TOP_K):
        rank = jnp.sum(jnp.where(eid == idxs[k], prefix, 0.0), axis=-1, keepdims=True)
        out = jnp.where(lane == k, idxs[k], out)
        out = jnp.where(lane == TOP_K + k, exps[k] / den, out)
        out = jnp.where(lane == 2 * TOP_K + k, rank, out)
    o_ref[...] = out
    run_ref[...] = run_ref[...] + jnp.sum(onehot, axis=0, keepdims=True)
    cnt_ref[...] = run_ref[...]


def _moe_route(h, norm_g, router_w, router_b):
    n, d = h.shape
    tm = min(512, n)
    w_hi = router_w.astype(BF16)
    w_lo = (router_w - w_hi.astype(F32)).astype(BF16)
    full = lambda *shape: pl.BlockSpec(shape, lambda i: (0,) * len(shape))
    return pl.pallas_call(
        _route_body,
        grid=(n // tm,),
        in_specs=[pl.BlockSpec((tm, d), lambda i: (i, 0)), full(1, d), full(3 * d, N_EXPERTS), full(1, N_EXPERTS)],
        out_specs=[pl.BlockSpec((tm, LANES), lambda i: (i, 0)), full(1, N_EXPERTS)],
        out_shape=[jax.ShapeDtypeStruct((n, LANES), F32), jax.ShapeDtypeStruct((1, N_EXPERTS), F32)],
        scratch_shapes=[pltpu.VMEM((1, N_EXPERTS), F32)],
        compiler_params=_cparams(("arbitrary",)),
        name="moe_route",
    )(h, norm_g.reshape(1, d), jnp.concatenate([w_hi, w_lo, w_hi], axis=0), router_b.reshape(1, N_EXPERTS))


def _dispatch_body(pad_ref, dest_ref, h_ref, ng_ref, xs_out, buf, zbuf, sem, zsem):
    i = pl.program_id(0)
    tm = h_ref.shape[0]

    @pl.when(i == 0)
    def _():
        zbuf[...] = jnp.zeros_like(zbuf)
        for e in range(2 * N_EXPERTS):
            first = pl.multiple_of(pad_ref[e], MOE_ROWS)
            zero = pltpu.make_async_copy(zbuf, xs_out.at[pl.ds(first, MOE_ROWS), :], zsem)
            zero.start()
            zero.wait()

    def wait_tile(s):
        for _ in range(TOP_K):
            pltpu.make_async_copy(buf.at[s], xs_out.at[pl.ds(0, tm), :], sem.at[s]).wait()

    def step(s):
        buf[s] = _rms(h_ref[...], ng_ref[...])
        for j in range(tm):
            for k in range(TOP_K):
                r = dest_ref[0, 0, j * TOP_K + k]
                pltpu.make_async_copy(buf.at[s, pl.ds(j, 1), :], xs_out.at[pl.ds(r, 1), :],
                                      sem.at[s]).start(priority=k % 2)

        @pl.when(i >= 1)
        def _():
            wait_tile(1 - s)

        @pl.when(i == pl.num_programs(0) - 1)
        def _():
            wait_tile(s)

    for s in range(2):
        pl.when(i % 2 == s)(functools.partial(step, s))


def _moe_dispatch(h, norm_g, dest, pad_rows, n_rows):
    n, d = h.shape
    tm = min(256, n)
    return pl.pallas_call(
        _dispatch_body,
        grid_spec=pltpu.PrefetchScalarGridSpec(
            num_scalar_prefetch=1, grid=(n // tm,),
            in_specs=[pl.BlockSpec((1, 1, tm * TOP_K), lambda i, pr: (i, 0, 0), memory_space=pltpu.SMEM),
                      pl.BlockSpec((tm, d), lambda i, pr: (i, 0)),
                      pl.BlockSpec((1, d), lambda i, pr: (0, 0))],
            out_specs=pl.BlockSpec(memory_space=pl.ANY),
            scratch_shapes=[pltpu.VMEM((2, tm, d), F32), pltpu.VMEM((MOE_ROWS, d), F32),
                            pltpu.SemaphoreType.DMA((2,)), pltpu.SemaphoreType.DMA(())]),
        out_shape=jax.ShapeDtypeStruct((n_rows, d), F32),
        compiler_params=_cparams(("arbitrary",)),
        name="moe_dispatch",
    )(pad_rows, dest.reshape(n // tm, 1, tm * TOP_K), h, norm_g.reshape(1, d))


def _expert_body(be_ref, nu_ref, xs_ref, wgu_ref, bgu_ref, wd_ref, bd_ref, ys_ref, wgu_bf, wd_bf):
    i = pl.program_id(0)
    ed = wd_ref.shape[2]

    @pl.when((i == 0) | (be_ref[i] != be_ref[jnp.maximum(i - 1, 0)]))
    def _():
        wgu_bf[...] = wgu_ref[0, 0].astype(BF16)
        wd_bf[...] = wd_ref[0, 0].astype(BF16)

    @pl.when(i < nu_ref[0])
    def _():
        x = xs_ref[...].astype(BF16)
        gu = _dot(x, wgu_bf[...]) + bgu_ref[0]
        gate = jnp.minimum(gu[:, :ed], SWIGLU_LIMIT)
        up = jnp.clip(gu[:, ed:], -SWIGLU_LIMIT, SWIGLU_LIMIT)
        glu = gate * jax.nn.sigmoid(gate * SWIGLU_ALPHA)
        ys_ref[...] = _dot(((up + 1.0) * glu).astype(BF16), wd_bf[...]) + bd_ref[0]

    @pl.when(i >= nu_ref[0])
    def _():
        ys_ref[...] = jnp.zeros_like(ys_ref)


def _moe_experts(xs, blk_expert, n_used, w_gu, b_gu, w_d, b_d, layer):
    n_rows, d = xs.shape
    ed = w_d.shape[2]
    n_blk = n_rows // MOE_ROWS
    return pl.pallas_call(
        _expert_body,
        grid_spec=pltpu.PrefetchScalarGridSpec(
            num_scalar_prefetch=2, grid=(n_blk,),
            in_specs=[pl.BlockSpec((MOE_ROWS, d), lambda i, be, nu: (jnp.maximum(jnp.minimum(i, nu[0] - 1), 0), 0)),
                      pl.BlockSpec((1, 1, d, 2 * ed), lambda i, be, nu: (layer, be[i], 0, 0)),
                      pl.BlockSpec((1, 1, 2 * ed), lambda i, be, nu: (be[i], 0, 0)),
                      pl.BlockSpec((1, 1, ed, d), lambda i, be, nu: (layer, be[i], 0, 0)),
                      pl.BlockSpec((1, 1, d), lambda i, be, nu: (be[i], 0, 0))],
            out_specs=pl.BlockSpec((MOE_ROWS, d), lambda i, be, nu: (i, 0)),
            scratch_shapes=[pltpu.VMEM((d, 2 * ed), BF16), pltpu.VMEM((ed, d), BF16)]),
        out_shape=jax.ShapeDtypeStruct((n_rows, d), F32),
        compiler_params=_cparams(("arbitrary",)),
        name="moe_experts",
    )(blk_expert, n_used, xs, w_gu, b_gu.reshape(N_EXPERTS, 1, 2 * ed), w_d, b_d.reshape(N_EXPERTS, 1, d))


def _combine_body(dest_ref, next_ref, rt_ref, h_ref, p_ref, pw_ref, pg_ref, pn_ref, ys_hbm, o_ref, buf, sem):
    i = pl.program_id(0)
    n = pl.num_programs(0)
    tm = h_ref.shape[0]
    slot = i % 2

    def row_copy(idx_ref, j, k, s):
        r = idx_ref[0, 0, j * TOP_K + k]
        return pltpu.make_async_copy(ys_hbm.at[pl.ds(r, 1), :], buf.at[s, k, pl.ds(j, 1), :], sem.at[s])

    def wait_tile(s):
        for k in range(TOP_K):
            pltpu.make_async_copy(ys_hbm.at[pl.ds(0, tm), :], buf.at[s, k], sem.at[s]).wait()

    @pl.when(i == 0)
    def _():
        def first(j, carry):
            for k in range(TOP_K):
                row_copy(dest_ref, j, k, 0).start()
            return carry
        lax.fori_loop(0, tm, first, 0)

    def step(s):
        wait_tile(s)
        for j in range(tm):
            for k in range(TOP_K):
                row_copy(next_ref, j, k, 1 - s).start(priority=k % 2)
        rt = rt_ref[...]
        h = h_ref[...]
        for k in range(TOP_K):
            h = h + rt[:, TOP_K + k:TOP_K + k + 1] * buf[s, k]
        emb = _dot(p_ref[...].astype(BF16), pw_ref[...])
        gate = jax.nn.sigmoid(_dot(_rms(h, pn_ref[...]).astype(BF16), pg_ref[...]))
        o_ref[...] = h + emb * gate

        @pl.when(i == n - 1)
        def _():
            wait_tile(1 - s)

    for s in range(2):
        pl.when(slot == s)(functools.partial(step, s))


def _moe_combine_ple(h, route, dest, ys, p, ple_w, ple_gate_w, ple_norm):
    n, d = h.shape
    pd = p.shape[1]
    tm = min(256, n)
    nt = n // tm
    full = lambda *shape: pl.BlockSpec(shape, lambda i: (0,) * len(shape))
    dest3 = dest.reshape(nt, 1, tm * TOP_K)
    return pl.pallas_call(
        _combine_body,
        grid=(nt,),
        in_specs=[pl.BlockSpec((1, 1, tm * TOP_K), lambda i: (i, 0, 0), memory_space=pltpu.SMEM),
                  pl.BlockSpec((1, 1, tm * TOP_K), lambda i: (jnp.minimum(i + 1, nt - 1), 0, 0),
                               memory_space=pltpu.SMEM),
                  pl.BlockSpec((tm, LANES), lambda i: (i, 0)),
                  pl.BlockSpec((tm, d), lambda i: (i, 0)),
                  pl.BlockSpec((tm, pd), lambda i: (i, 0)),
                  full(pd, d), full(d, d), full(1, d),
                  pl.BlockSpec(memory_space=pl.ANY)],
        out_specs=pl.BlockSpec((tm, d), lambda i: (i, 0)),
        out_shape=jax.ShapeDtypeStruct((n, d), F32),
        scratch_shapes=[pltpu.VMEM((2, TOP_K, tm, d), F32), pltpu.SemaphoreType.DMA((2,))],
        compiler_params=_cparams(("arbitrary",)),
        name="moe_combine_ple",
    )(dest3, dest3, route, h, p, ple_w.astype(BF16), ple_gate_w.astype(BF16), ple_norm.reshape(1, d), ys)


def _moe_ple_layer(h, norm_g, router_w, router_b, w_gu, b_gu, w_d, b_d, layer, p, ple_w, ple_gate_w, ple_norm):
    n, _ = h.shape
    route, counts = _moe_route(h, norm_g, router_w, router_b)
    counts = counts[0].astype(jnp.int32)
    pad_counts = (counts + MOE_ROWS - 1) // MOE_ROWS * MOE_ROWS
    pad_ends = jnp.cumsum(pad_counts)
    pad_starts = pad_ends - pad_counts
    top_idx = route[:, :TOP_K].astype(jnp.int32)
    rank = route[:, 2 * TOP_K:3 * TOP_K].astype(jnp.int32)
    dest = (pad_starts[top_idx] + rank).reshape(-1)
    n_blk = -(-(n * TOP_K) // MOE_ROWS) + N_EXPERTS
    blk_start = jnp.arange(n_blk, dtype=jnp.int32) * MOE_ROWS
    blk_expert = jnp.minimum(jnp.sum((pad_ends[None, :] <= blk_start[:, None]).astype(jnp.int32), axis=1),
                             N_EXPERTS - 1)
    n_used = (pad_ends[-1:] // MOE_ROWS).astype(jnp.int32)
    tail = jnp.minimum(pad_ends[-1] + jnp.arange(N_EXPERTS, dtype=jnp.int32) * MOE_ROWS, (n_blk - 1) * MOE_ROWS)
    clear = jnp.concatenate([jnp.maximum(pad_ends - MOE_ROWS, 0), tail]).astype(jnp.int32)
    xs = _moe_dispatch(h, norm_g, dest, clear, n_blk * MOE_ROWS)
    ys = _moe_experts(xs, blk_expert, n_used, w_gu, b_gu, w_d, b_d, layer)
    return _moe_combine_ple(h, route, dest, ys, p, ple_w, ple_gate_w, ple_norm)


def _kv_body(h_ref, ng_ref, w_ref, seg_ref, kn_ref, kc_ref, vc_ref, ks_ref, vs_ref, kw_ref, vw_ref):
    ts = h_ref.shape[1]
    gw = N_KV_GROUPS * HEAD_DIM
    st = pl.program_id(1)
    hn = _rms(h_ref[0], ng_ref[...]).astype(BF16)
    kv = _dot(hn, w_ref[...])

    def knorm(x, j):
        ms = _dot_split(x * x, seg_ref[...])
        return x * lax.rsqrt(ms + NORM_EPS) * kn_ref[j]

    k_c, v_c = kv[:, 0:gw], kv[:, gw:2 * gw]
    k_s, v_s = knorm(kv[:, 2 * gw:3 * gw], 1), kv[:, 3 * gw:4 * gw]
    k_w, v_w = knorm(kv[:, 4 * gw:5 * gw], 2), kv[:, 5 * gw:6 * gw]
    tok = st * ts + lax.broadcasted_iota(jnp.int32, (ts, LANES), 0)
    blk = lax.broadcasted_iota(jnp.int32, (ts, LANES), 1)
    onehot = (tok // SEL_BLOCK == blk).astype(BF16)
    one_col = (blk == HEAD_DIM).astype(F32)
    for g in range(N_KV_GROUPS):
        kc_ref[0, g] = _head_lanes(k_c, g)[:, :HEAD_DIM]
        vc_ref[0, g] = _head_lanes(v_c, g)[:, :HEAD_DIM]
        ks_ref[0, g] = jnp.concatenate([_low_lanes(_head_lanes(k_s, g), 0.0).astype(BF16), onehot], axis=1)
        vs_ref[0, g] = _low_lanes(_head_lanes(v_s, g), one_col).astype(BF16)
        kw_ref[0, g] = _low_lanes(_head_lanes(k_w, g), 0.0).astype(BF16)
        vw_ref[0, g] = _low_lanes(_head_lanes(v_w, g), one_col).astype(BF16)


def _kv_project(h3, kv_norm, kv_w, k_norm):
    b, s, d = h3.shape
    gw = N_KV_GROUPS * HEAD_DIM
    ts = min(512, s)
    seg = jnp.asarray(np.tile(np.kron(np.eye(N_KV_GROUPS), np.full((HEAD_DIM, HEAD_DIM), 1.0 / HEAD_DIM)),
                              (2, 1)), F32).astype(BF16)
    kn = jnp.tile(k_norm, (1, N_KV_GROUPS)).reshape(N_BRANCH, 1, gw)
    full = lambda *shape: pl.BlockSpec(shape, lambda bi, si: (0,) * len(shape))
    hd = lambda w: pl.BlockSpec((1, N_KV_GROUPS, ts, w), lambda bi, si: (bi, 0, si, 0))
    sds = lambda w, dt: jax.ShapeDtypeStruct((b, N_KV_GROUPS, s, w), dt)
    return pl.pallas_call(
        _kv_body,
        grid=(b, s // ts),
        in_specs=[pl.BlockSpec((1, ts, d), lambda bi, si: (bi, si, 0)), full(1, d), full(d, N_KV_SLOTS * gw),
                  full(2 * gw, gw), full(N_BRANCH, 1, gw)],
        out_specs=[hd(HEAD_DIM), hd(HEAD_DIM), hd(2 * LANES), hd(LANES), hd(LANES), hd(LANES)],
        out_shape=[sds(HEAD_DIM, F32), sds(HEAD_DIM, F32), sds(2 * LANES, BF16), sds(LANES, BF16),
                   sds(LANES, BF16), sds(LANES, BF16)],
        compiler_params=_cparams(("arbitrary", "arbitrary")),
        name="kv_project",
    )(h3, kv_norm.reshape(1, d), kv_w.astype(BF16), seg, kn)


def _compress_body(kc_ref, vc_ref, pos_ref, w1_ref, b1_ref, w2_ref, kn_ref, ko_ref, vo_ref):
    nc = kc_ref.shape[2]
    half = kc_ref.shape[3]
    row = lax.broadcasted_iota(jnp.int32, (nc, 1), 0)
    valid = row < nc - 1

    def compress(c, j):
        a = _dot((c + pos_ref[j, 0]).astype(BF16), w1_ref[j, 0])
        bm = _dot((c + pos_ref[j, 1]).astype(BF16), w1_ref[j, 1])
        nxt = pltpu.roll(bm, nc - 1, 0)
        hid = jax.nn.gelu(a + nxt + b1_ref[j])
        return _dot(hid.astype(BF16), w2_ref[j])

    kraw = compress(kc_ref[0, 0], 0)
    ms = jnp.sum(kraw * kraw, axis=-1, keepdims=True) * (1.0 / HEAD_DIM)
    kcmp = kraw * lax.rsqrt(ms + NORM_EPS) * kn_ref[...]
    vcmp = jnp.where(valid, compress(vc_ref[0, 0], 1), 0.0)
    lane = lax.broadcasted_iota(jnp.int32, (nc, LANES), 1)
    flag = (lane == HEAD_DIM).astype(F32)
    ko_ref[0, 0, 0:nc, :] = flag
    ko_ref[0, 0, nc:2 * nc, :] = jnp.where(valid, kcmp, flag)
    vo_ref[0, 0, 0:nc, :] = flag
    vo_ref[0, 0, nc:2 * nc, :] = jnp.where(lane == HEAD_DIM, 1.0, vcmp)


def _compress(kc, vc, cmp_pos, cmp_w1, cmp_b1, cmp_w2, k_norm0):
    b, g, s, dh = kc.shape
    nc = s // CMP_STRIDE
    half = CMP_STRIDE * dh
    kc_r = kc.reshape(b, g, nc, half)
    vc_r = vc.reshape(b, g, nc, half)
    pos = cmp_pos.reshape(2, 2, 1, half)
    w1 = cmp_w1.reshape(2, 2, half, CMP_HIDDEN).astype(BF16)
    w2 = jnp.pad(cmp_w2, ((0, 0), (0, 0), (0, LANES - dh))).astype(BF16)
    kn = jnp.pad(k_norm0, (0, LANES - dh)).reshape(1, LANES)
    full = lambda *shape: pl.BlockSpec(shape, lambda bi, gi: (0,) * len(shape))
    blk = pl.BlockSpec((1, 1, nc, half), lambda bi, gi: (bi, gi, 0, 0))
    out = pl.BlockSpec((1, 1, 2 * nc, LANES), lambda bi, gi: (bi, gi, 0, 0))
    return pl.pallas_call(
        _compress_body,
        grid=(b, g),
        in_specs=[blk, blk, full(2, 2, 1, half), full(2, 2, half, CMP_HIDDEN), full(2, 1, CMP_HIDDEN),
                  full(2, CMP_HIDDEN, LANES), full(1, LANES)],
        out_specs=[out, out],
        out_shape=[jax.ShapeDtypeStruct((b, g, 2 * nc, LANES), F32)] * 2,
        compiler_params=_cparams(("arbitrary", "arbitrary")),
        name="kv_compress",
    )(kc_r, vc_r, pos, w1, cmp_b1.reshape(2, 1, CMP_HIDDEN), w2, kn)


def _qproj_body(h_ref, ng_ref, w_ref, bg_ref, ind_ref, indt_ref, qn_ref, q_ref, gate_ref):
    hd = N_HEADS * HEAD_DIM
    xn = _rms(h_ref[0], ng_ref[...]).astype(BF16)
    proj = _dot(xn, w_ref[...])
    q = proj[:, :hd]
    ms = _dot_split(q * q, ind_ref[...]) * (1.0 / HEAD_DIM)
    scale = _dot_split(lax.rsqrt(ms + NORM_EPS), indt_ref[...])
    qn = q * scale * qn_ref[...] * (HEAD_DIM ** -0.5 * LOG2E)
    lane = lax.broadcasted_iota(jnp.int32, (q.shape[0], LANES), 1)
    fill = jnp.where(lane == HEAD_DIM, NEG_INF, 0.0)
    for h in range(N_HEADS):
        q_ref[0, h] = _low_lanes(_head_lanes(qn, h), fill).astype(BF16)
    gate_ref[0] = jax.nn.sigmoid(proj[:, hd:] + bg_ref[...])


def _q_project(h3, norm_g, w_in, b_gate, q_norm):
    b, s, d = h3.shape
    hd = N_HEADS * HEAD_DIM
    ng = N_BRANCH * N_HEADS
    ts = min(512, s)
    w = jnp.pad(w_in, ((0, 0), (0, LANES - ng))).astype(BF16)
    bg = jnp.pad(b_gate, (0, LANES - ng)).reshape(1, LANES)
    ind = np.zeros((hd, LANES), np.float32)
    ind[np.arange(hd), np.arange(hd) // HEAD_DIM] = 1.0
    full = lambda *shape: pl.BlockSpec(shape, lambda bi, si: (0,) * len(shape))
    return pl.pallas_call(
        _qproj_body,
        grid=(b, s // ts),
        in_specs=[pl.BlockSpec((1, ts, d), lambda bi, si: (bi, si, 0)), full(1, d), full(d, hd + LANES),
                  full(1, LANES), full(2 * hd, LANES), full(2 * LANES, hd), full(1, hd)],
        out_specs=[pl.BlockSpec((1, N_HEADS, ts, LANES), lambda bi, si: (bi, 0, si, 0)),
                   pl.BlockSpec((1, ts, LANES), lambda bi, si: (bi, si, 0))],
        out_shape=[jax.ShapeDtypeStruct((b, N_HEADS, s, LANES), BF16),
                   jax.ShapeDtypeStruct((b, s, LANES), F32)],
        compiler_params=_cparams(("arbitrary", "arbitrary")),
        name="nsa_qproj",
    )(h3, norm_g.reshape(1, d), w, bg, jnp.asarray(np.tile(ind, (2, 1))).astype(BF16),
      jnp.asarray(np.tile(ind.T, (2, 1))).astype(BF16), jnp.tile(q_norm, N_HEADS).reshape(1, hd))


def _t5_bucket_np(dist):
    n = np.maximum(dist, 0)
    max_exact = REL_BUCKETS // 2
    nf = np.maximum(n, 1).astype(np.float64)
    large = max_exact + (np.log(nf / max_exact) / math.log(REL_MAX_DIST / max_exact)
                         * (REL_BUCKETS - max_exact)).astype(np.int64)
    return np.where(n < max_exact, n, np.minimum(large, REL_BUCKETS - 1))


def _bias_table(rel_bias, dist, valid):
    r = N_HEADS // N_KV_GROUPS
    tab = rel_bias.astype(F32).T.reshape(N_KV_GROUPS, r, REL_BUCKETS)
    onehot = jnp.asarray(_t5_bucket_np(dist)[..., None] == np.arange(REL_BUCKETS), F32)
    bias = jnp.einsum('xqln,grn->gxrql', onehot, tab, precision=lax.Precision.HIGHEST) * LOG2E
    return jnp.where(jnp.asarray(valid)[None, :, None, :, :], bias, NEG_INF)


def _n_delta(seq):
    d = np.arange(seq + SEL_BLOCK)
    bk = _t5_bucket_np(d)
    change = np.nonzero(bk[1:] != bk[:-1])[0]
    d_const = int(change[-1]) + 1 if change.size else 0
    return -(-(d_const + SEL_BLOCK - 1) // SEL_BLOCK) + 1


def _attn_tables(rel_bias, seq):
    r = N_HEADS // N_KV_GROUPS
    qi = np.arange(Q_BLOCK)[:, None]
    rows = lambda t: jnp.transpose(t, (0, 2, 1, 3, 4)).reshape(N_KV_GROUPS, r * Q_PAIR * Q_BLOCK, t.shape[-1])
    nc = seq // CMP_STRIDE
    j = np.arange(nc)[None, :]
    dist_c = np.stack([qi - (CMP_BLOCK - 1) - Q_BLOCK * (Q_PAIR - u) + CMP_STRIDE * (nc - j)
                       for u in range(Q_PAIR)])
    rc = rows(_bias_table(rel_bias, dist_c, dist_c >= 0))
    nd = _n_delta(seq)
    delta = np.arange(-1, nd + 1)[:, None, None]
    kj = np.arange(2 * SEL_BLOCK)[None, None, :]
    dist_s = SEL_BLOCK * (delta - kj // SEL_BLOCK) + qi[None] - kj % SEL_BLOCK
    bt = _bias_table(rel_bias, dist_s, dist_s >= 0)
    jw = np.arange(WINDOW + Q_PAIR * Q_BLOCK)[None, :]
    dist_w = np.stack([Q_BLOCK * u + qi - jw + WINDOW for u in range(Q_PAIR)])
    wb = rows(_bias_table(rel_bias, dist_w, (dist_w >= 0) & (dist_w < WINDOW)))
    n_sel = seq // SEL_BLOCK
    cs = np.arange(nc) * CMP_STRIDE
    ss = np.arange(n_sel) * SEL_BLOCK
    ov = np.clip(np.minimum(cs[:, None] + CMP_BLOCK, ss[None, :] + SEL_BLOCK)
                 - np.maximum(cs[:, None], ss[None, :]), 0, None) / CMP_BLOCK
    return rc, bt, wb, jnp.asarray(ov.T.astype(np.float32)).astype(BF16)


def _attn_body(q_ref, gate_ref, kc_ref, vc_ref, ks_ref, vs_ref, kw_ref, vw_ref, rc_ref, bt_ref, wb_ref, ovt_ref,
               o_ref, sa_ref, sb_ref):
    g = pl.program_id(1)
    i0 = Q_PAIR * pl.program_id(2)
    r = q_ref.shape[1]
    pq = Q_PAIR * Q_BLOCK
    rq = r * pq
    nc = rc_ref.shape[2]
    n_sel = ovt_ref.shape[0]
    nd = bt_ref.shape[1] - 2
    wl = wb_ref.shape[2]
    per = SEL_BLOCK // CMP_STRIDE

    heads = [slice(h * pq, (h + 1) * pq) for h in range(r)]
    q_pad = q_ref[0].reshape(rq, LANES)

    end = pl.multiple_of(per * (i0 + Q_PAIR), Q_PAIR * per)
    kcw = kc_ref[0, 0, pl.ds(end, nc), :].astype(BF16)
    vcw = vc_ref[0, 0, pl.ds(end, nc), :].astype(BF16)
    qpos = i0 * Q_BLOCK + lax.broadcasted_iota(jnp.int32, (rq, 1), 0) % pq
    sc = _dot_nt(q_pad, kcw) + rc_ref[0]
    p_c = jnp.exp2((sc - jnp.max(sc, axis=-1, keepdims=True)).astype(BF16))
    o_c = _normalize(_dot(p_c, vcw)) * (qpos >= CMP_BLOCK - 1).astype(F32)

    imp_r = _dot_nt(ovt_ref[...], p_c)
    inv_l = 1.0 / _dot_nt(jnp.ones((8, nc), BF16), p_c)[0:1]
    imp = imp_r[:, heads[0]] * inv_l[:, heads[0]]
    for h in range(1, r):
        imp = imp + imp_r[:, heads[h]] * inv_l[:, heads[h]]
    tpos = i0 * Q_BLOCK + lax.broadcasted_iota(jnp.int32, (1, pq), 1)
    imp = imp * (tpos >= CMP_BLOCK - 1).astype(F32)

    anchor = (jnp.max(imp, axis=(0, 1), keepdims=True) * 0.0).astype(BF16)
    ws = pl.multiple_of(i0 * Q_BLOCK, pq)
    sw = _dot_nt(q_pad + anchor, kw_ref[0, 0, pl.ds(ws, wl), :]) + wb_ref[0]
    p_w = jnp.exp2((sw - jnp.max(sw, axis=-1, keepdims=True)).astype(BF16))
    o_w = _normalize(_dot(p_w, vw_ref[0, 0, pl.ds(ws, wl), :]))

    shift = i0 + Q_PAIR
    blk_rel = lax.broadcasted_iota(jnp.int32, (n_sel, pq), 0)
    blk = blk_rel + shift - n_sel
    cur = i0 + lax.broadcasted_iota(jnp.int32, (n_sel, pq), 1) // Q_BLOCK
    forced = (blk == 0) | (blk == cur) | (blk == cur - 1)
    imp = jnp.where(forced, FORCE, jnp.where(blk > cur, NEG_INF, imp))
    imp = jnp.where(blk < 0, -jnp.inf, imp)
    ids = blk_rel.astype(F32)
    sel = jnp.zeros((n_sel, pq), F32)
    for _ in range(min(SEL_TOP, n_sel)):
        mx = jnp.max(imp, axis=0, keepdims=True)
        ix = jnp.min(jnp.where(imp == mx, ids, float(n_sel)), axis=0, keepdims=True)
        hit = ids == ix
        sel = jnp.where(hit, 1.0, sel)
        imp = jnp.where(hit, -jnp.inf, imp)
    unsel = jnp.where((sel > 0.0) & (blk >= 0), 0.0, NEG_INF)
    unsel = pltpu.roll(unsel.T, shift % n_sel, 1).astype(BF16)
    if n_sel < LANES:
        unsel = jnp.concatenate([unsel, jnp.zeros((pq, LANES - n_sel), BF16)], axis=1)

    q_aug = jnp.concatenate([q_pad, jnp.concatenate([unsel] * r, axis=0)], axis=1)
    kchunk = sa_ref.shape[1]
    n_chunks = ks_ref.shape[2] // kchunk
    cblocks = kchunk // SEL_BLOCK
    pairs = cblocks // 2

    def scores_to(dst, c):
        start = pl.multiple_of(jnp.minimum(c, n_chunks - 1) * kchunk, kchunk)
        sc = _dot_nt(q_aug, ks_ref[0, 0, pl.ds(start, kchunk), :])
        d0 = i0 - c * cblocks
        tiles = [[jnp.clip(d0 + u - 2 * pm, -1, nd) + 1 for pm in range(pairs)] for u in range(Q_PAIR)]
        dst[...] = sc + jnp.concatenate(
            [jnp.concatenate([bt_ref[0, t, h] for t in tiles[u]], axis=1)
             for h in range(r) for u in range(Q_PAIR)], axis=0)

    def absorb(src, c, m, acc):
        start = pl.multiple_of(c * kchunk, kchunk)
        sc = src[...]
        m_new = jnp.maximum(m, jnp.max(sc, axis=-1, keepdims=True))
        p = jnp.exp2((sc - m_new).astype(BF16))
        acc = jnp.exp2(m - m_new) * acc + _dot(p, vs_ref[0, 0, pl.ds(start, kchunk), :])
        return m_new, acc

    def two_chunks(j, carry):
        scores_to(sb_ref, 2 * j + 1)
        carry = absorb(sa_ref, 2 * j, *carry)
        scores_to(sa_ref, 2 * j + 2)
        return absorb(sb_ref, 2 * j + 1, *carry)

    scores_to(sa_ref, 0)
    init = (jnp.full((rq, 1), NEG_INF, F32), jnp.zeros((rq, LANES), F32))
    rem = (i0 + Q_PAIR) % (2 * cblocks)
    trips = (i0 + Q_PAIR) // (2 * cblocks) + (rem > cblocks).astype(jnp.int32)
    state = lax.fori_loop(0, trips, two_chunks, init)
    _, acc_s = lax.cond((rem > 0) & (rem <= cblocks),
                        lambda m, acc: absorb(sa_ref, 2 * trips, m, acc), lambda m, acc: (m, acc), *state)
    o_s = _normalize(acc_s)

    gates = gate_ref[0]
    glane = lax.broadcasted_iota(jnp.int32, gates.shape, 1)
    outs = []
    for h in range(r):
        head = g * r + h
        gs = [jnp.sum(jnp.where(glane == br * N_HEADS + head, gates, 0.0), axis=-1, keepdims=True)
              for br in range(N_BRANCH)]
        outs.append(gs[0] * o_c[heads[h]] + gs[1] * o_s[heads[h]] + gs[2] * o_w[heads[h]])
    o_ref[0] = jnp.concatenate(
        [outs[h] + pltpu.roll(outs[h + 1], HEAD_DIM, 1) for h in range(0, r, 2)], axis=1)


def _nsa_attention(q, gates, kc_pad, vc_pad, ks_aug, vs, kw_pad, vw_pad, tables):
    b, _, s, _ = q.shape
    r = N_HEADS // N_KV_GROUPS
    rc, bt, wb, ov = tables
    n_qb = s // Q_BLOCK
    per_bg = lambda a: pl.BlockSpec((1, 1) + a.shape[2:], lambda bi, gi, qi: (bi, gi, 0, 0))
    per_g = lambda a: pl.BlockSpec((1,) + a.shape[1:], lambda bi, gi, qi: (gi,) + (0,) * (a.ndim - 1))
    pq = Q_PAIR * Q_BLOCK
    return pl.pallas_call(
        _attn_body,
        grid=(b, N_KV_GROUPS, n_qb // Q_PAIR),
        in_specs=[pl.BlockSpec((1, r, pq, LANES), lambda bi, gi, qi: (bi, gi, qi, 0)),
                  pl.BlockSpec((1, pq, LANES), lambda bi, gi, qi: (bi, qi, 0)),
                  per_bg(kc_pad), per_bg(vc_pad), per_bg(ks_aug), per_bg(vs), per_bg(kw_pad), per_bg(vw_pad),
                  per_g(rc), per_g(bt), per_g(wb),
                  pl.BlockSpec(ov.shape, lambda bi, gi, qi: (0, 0))],
        out_specs=pl.BlockSpec((1, pq, r * HEAD_DIM), lambda bi, gi, qi: (bi, qi, gi)),
        out_shape=jax.ShapeDtypeStruct((b, s, N_HEADS * HEAD_DIM), F32),
        scratch_shapes=[pltpu.VMEM((r * pq, min(KEY_CHUNK, s // 2)), F32)] * 2,
        compiler_params=_cparams(("arbitrary", "arbitrary", "arbitrary")),
        name="nsa_attention",
    )(q, gates, kc_pad, vc_pad, ks_aug, vs, kw_pad, vw_pad, rc, bt, wb, ov)


def _outproj_body(a_ref, h_ref, w_ref, o_ref):
    o_ref[...] = h_ref[...] + _dot(a_ref[...].astype(BF16), w_ref[...])


def _out_project(attn, h, w_out):
    n, d = h.shape
    hd = attn.shape[1]
    tm = min(512, n)
    return pl.pallas_call(
        _outproj_body,
        grid=(n // tm,),
        in_specs=[pl.BlockSpec((tm, hd), lambda i: (i, 0)), pl.BlockSpec((tm, d), lambda i: (i, 0)),
                  pl.BlockSpec((hd, d), lambda i: (0, 0))],
        out_specs=pl.BlockSpec((tm, d), lambda i: (i, 0)),
        out_shape=jax.ShapeDtypeStruct((n, d), F32),
        compiler_params=_cparams(("arbitrary",)),
        name="nsa_outproj",
    )(attn, h, w_out.astype(BF16))


def kernel(x, p, rel_bias, norm_mix, norm_ffn, a_w_in, a_ln_g, a_ln_b, a_w_s, a_b_s, a_w_out, kv_norm, kv_w, cmp_pos, cmp_w1, cmp_b1, cmp_w2, k_norm, b_w_in, b_b_gate, q_norm, b_w_out, router_w, router_b, e_w_gu, e_b_gu, e_w_d, e_b_d, ple_w, ple_gate_w, ple_norm):
    b, s, d = x.shape
    n = b * s
    pf = p.reshape(p.shape[0], n, p.shape[-1])

    def moe_ple(h, i):
        return _moe_ple_layer(h, norm_ffn[i], router_w[i], router_b[i], e_w_gu, e_b_gu[i], e_w_d, e_b_d[i], i,
                              pf[i], ple_w[i], ple_gate_w[i], ple_norm[i])

    h = _gmlp_layer(x.reshape(n, d), norm_mix[0], a_w_in[0], a_ln_g[0], a_ln_b[0], a_w_s[0], a_b_s[0], a_w_out[0])
    h = moe_ple(h, 0)

    h3 = h.reshape(b, s, d)
    kc, vc, ks_aug, vs, kw, vw = _kv_project(h3, kv_norm, kv_w, k_norm)
    kc_pad, vc_pad = _compress(kc, vc, cmp_pos, cmp_w1, cmp_b1, cmp_w2, k_norm[0])
    front = jnp.zeros((b, N_KV_GROUPS, WINDOW, LANES), BF16).at[..., HEAD_DIM].set(1.0)
    kw_pad = jnp.concatenate([front, kw, front[:, :, :2 * Q_BLOCK]], axis=2)
    vw_pad = jnp.pad(vw, ((0, 0), (0, 0), (WINDOW, 2 * Q_BLOCK), (0, 0)))

    q, gates = _q_project(h3, norm_mix[1], b_w_in[0], b_b_gate[0], q_norm[0])
    attn = _nsa_attention(q, gates, kc_pad, vc_pad, ks_aug, vs, kw_pad, vw_pad, _attn_tables(rel_bias, s))
    h = _out_project(attn.reshape(n, -1), h, b_w_out[0])
    h = moe_ple(h, 1)
    return h.reshape(b, s, d)
```

```python
import functools
import math

import numpy as np
import jax
import jax.numpy as jnp
from jax import lax
from jax.experimental import pallas as pl
from jax.experimental.pallas import tpu as pltpu

GMLP_CHUNK = 128
GMLP_GROUPS = 8
N_HEADS = 16
HEAD_DIM = 64
N_KV_GROUPS = 4
N_BRANCH = 3
CMP_BLOCK = 32
CMP_STRIDE = 16
CMP_HIDDEN = 256
SEL_BLOCK = 64
SEL_TOP = 16
WINDOW = 512
Q_BLOCK = 64
N_KV_SLOTS = 6
REL_BUCKETS = 32
REL_MAX_DIST = 2048
N_EXPERTS = 32
TOP_K = 4
SWIGLU_LIMIT = 7.0
SWIGLU_ALPHA = 1.702
NORM_EPS = 1e-6
NEG_INF = -1e30
FORCE = 1e30
LOG2E = math.log2(math.e)

LANES = 128
MOE_ROWS = 512
KEY_CHUNK = 512
Q_PAIR = 4
VMEM_LIMIT = 56 * 1024 * 1024

F32 = jnp.float32
BF16 = jnp.bfloat16


def _cparams(sem):
    return pltpu.CompilerParams(dimension_semantics=sem, vmem_limit_bytes=VMEM_LIMIT)


def _rms(x, g):
    return x * lax.rsqrt(jnp.mean(x * x, axis=-1, keepdims=True) + NORM_EPS) * g


def _dot(a, b):
    return jnp.dot(a, b, preferred_element_type=F32)


def _dot_nt(a, b):
    return lax.dot_general(a, b, (((1,), (1,)), ((), ())), preferred_element_type=F32)


def _dot_split(a, b2):
    hi = a.astype(BF16)
    lo = (a - hi.astype(F32)).astype(BF16)
    return _dot(jnp.concatenate([hi, lo], axis=1), b2)


def _normalize(acc):
    lane = lax.broadcasted_iota(jnp.int32, acc.shape, 1)
    denom = jnp.sum(jnp.where(lane == HEAD_DIM, acc, 0.0), axis=-1, keepdims=True)
    return jnp.where(lane < HEAD_DIM, acc / denom, 0.0)


def _head_lanes(x, idx):
    base = idx * HEAD_DIM // LANES * LANES
    y = x[:, base:base + LANES]
    return y if idx * HEAD_DIM == base else pltpu.roll(y, LANES - HEAD_DIM, 1)


def _low_lanes(x, fill):
    lane = lax.broadcasted_iota(jnp.int32, x.shape, 1)
    return jnp.where(lane < HEAD_DIM, x, fill)


def _argmax_first(x, ids, n):
    mx = jnp.max(x, axis=-1, keepdims=True)
    return mx, jnp.min(jnp.where(x == mx, ids, float(n)), axis=-1, keepdims=True)


def _gmlp_body(x_ref, nm_ref, win_ref, lng_ref, lnb_ref, ws_ref, bs_ref, wout_ref, o_ref):
    tm = x_ref.shape[0]
    gd = win_ref.shape[1] // 2
    gdim = gd // GMLP_GROUPS
    x = x_ref[...]
    xn = _rms(x, nm_ref[...]).astype(BF16)
    z = jax.nn.gelu(_dot(xn, win_ref[...]))
    u = z[:, :gd]
    v = z[:, gd:]
    mu = jnp.mean(v, axis=-1, keepdims=True)
    vc = v - mu
    vln = vc * lax.rsqrt(jnp.mean(vc * vc, axis=-1, keepdims=True) + NORM_EPS) * lng_ref[...] + lnb_ref[...]
    vb = vln.astype(BF16)
    row = lax.broadcasted_iota(jnp.int32, (GMLP_CHUNK, GMLP_CHUNK), 0)
    col = lax.broadcasted_iota(jnp.int32, (GMLP_CHUNK, GMLP_CHUNK), 1)
    causal = row >= col
    chunks = []
    for c in range(tm // GMLP_CHUNK):
        cols = []
        for g in range(GMLP_GROUPS):
            wsg = jnp.where(causal, ws_ref[g], 0.0).astype(BF16)
            vg = vb[c * GMLP_CHUNK:(c + 1) * GMLP_CHUNK, g * gdim:(g + 1) * gdim]
            cols.append(_dot(wsg, vg) + bs_ref[g])
        chunks.append(jnp.concatenate(cols, axis=1))
    mixed = jnp.concatenate(chunks, axis=0)
    gated = (u * mixed).astype(BF16)
    o_ref[...] = x + _dot(gated, wout_ref[...])


def _gmlp_layer(h, norm_g, w_in, ln_g, ln_b, w_s, b_s, w_out):
    n, d = h.shape
    gd2 = w_in.shape[1]
    gd = gd2 // 2
    tm = min(512, n)
    full = lambda *shape: pl.BlockSpec(shape, lambda i: (0,) * len(shape))
    return pl.pallas_call(
        _gmlp_body,
        grid=(n // tm,),
        in_specs=[pl.BlockSpec((tm, d), lambda i: (i, 0)),
                  full(1, d), full(d, gd2), full(1, gd), full(1, gd),
                  full(GMLP_GROUPS, GMLP_CHUNK, GMLP_CHUNK), full(GMLP_GROUPS, GMLP_CHUNK, 1),
                  full(gd, d)],
        out_specs=pl.BlockSpec((tm, d), lambda i: (i, 0)),
        out_shape=jax.ShapeDtypeStruct((n, d), F32),
        compiler_params=_cparams(("arbitrary",)),
        name="gmlp_layer",
    )(h, norm_g.reshape(1, d), w_in.astype(BF16), ln_g.reshape(1, gd), ln_b.reshape(1, gd),
      w_s, b_s.reshape(GMLP_GROUPS, GMLP_CHUNK, 1), w_out.astype(BF16))


def _route_body(h_ref, ng_ref, rw_ref, rb_ref, o_ref, cnt_ref, run_ref):
    i = pl.program_id(0)
    tm = h_ref.shape[0]

    @pl.when(i == 0)
    def _():
        run_ref[...] = jnp.zeros_like(run_ref)

    xn = _rms(h_ref[...], ng_ref[...])
    x_hi = xn.astype(BF16)
    x_lo = (xn - x_hi.astype(F32)).astype(BF16)
    logits = _dot(jnp.concatenate([x_hi, x_hi, x_lo], axis=1), rw_ref[...]) + rb_ref[...]
    eid = lax.broadcasted_iota(jnp.int32, logits.shape, 1).astype(F32)
    lane = lax.broadcasted_iota(jnp.int32, (tm, LANES), 1)
    work = logits
    vals, idxs = [], []
    for _ in range(TOP_K):
        mx, ix = _argmax_first(work, eid, N_EXPERTS)
        vals.append(mx)
        idxs.append(ix)
        work = jnp.where(eid == ix, -jnp.inf, work)
    exps = [jnp.exp(v - vals[0]) for v in vals]
    den = exps[0]
    for e in exps[1:]:
        den = den + e
    onehot = jnp.zeros(logits.shape, F32)
    for ix in idxs:
        onehot = onehot + (eid == ix).astype(F32)
    r = lax.broadcasted_iota(jnp.int32, (tm, tm), 0)
    c = lax.broadcasted_iota(jnp.int32, (tm, tm), 1)
    before = (r > c).astype(BF16)
    prefix = _dot(before, onehot.astype(BF16)) + run_ref[...]
    out = jnp.zeros((tm, LANES), F32)
    for k in range(TOP_K):
        rank = jnp.sum(jnp.where(eid == idxs[k], prefix, 0.0), axis=-1, keepdims=True)
        out = jnp.where(lane == k, idxs[k], out)
        out = jnp.where(lane == TOP_K + k, exps[k] / den, out)
        out = jnp.where(lane == 2 * TOP_K + k, rank, out)
    o_ref[...] = out
    run_ref[...] = run_ref[...] + jnp.sum(onehot, axis=0, keepdims=True)
    cnt_ref[...] = run_ref[...]


def _moe_route(h, norm_g, router_w, router_b):
    n, d = h.shape
    tm = min(512, n)
    w_hi = router_w.astype(BF16)
    w_lo = (router_w - w_hi.astype(F32)).astype(BF16)
    full = lambda *shape: pl.BlockSpec(shape, lambda i: (0,) * len(shape))
    return pl.pallas_call(
        _route_body,
        grid=(n // tm,),
        in_specs=[pl.BlockSpec((tm, d), lambda i: (i, 0)), full(1, d), full(3 * d, N_EXPERTS), full(1, N_EXPERTS)],
        out_specs=[pl.BlockSpec((tm, LANES), lambda i: (i, 0)), full(1, N_EXPERTS)],
        out_shape=[jax.ShapeDtypeStruct((n, LANES), F32), jax.ShapeDtypeStruct((1, N_EXPERTS), F32)],
        scratch_shapes=[pltpu.VMEM((1, N_EXPERTS), F32)],
        compiler_params=_cparams(("arbitrary",)),
        name="moe_route",
    )(h, norm_g.reshape(1, d), jnp.concatenate([w_hi, w_lo, w_hi], axis=0), router_b.reshape(1, N_EXPERTS))


def _dispatch_body(pad_ref, dest_ref, h_ref, ng_ref, xs_out, buf, zbuf, sem, zsem):
    i = pl.program_id(0)
    tm = h_ref.shape[0]

    @pl.when(i == 0)
    def _():
        zbuf[...] = jnp.zeros_like(zbuf)
        for e in range(2 * N_EXPERTS):
            first = pl.multiple_of(pad_ref[e], MOE_ROWS)
            zero = pltpu.make_async_copy(zbuf, xs_out.at[pl.ds(first, MOE_ROWS), :], zsem)
            zero.start()
            zero.wait()

    def wait_tile(s):
        for _ in range(TOP_K):
            pltpu.make_async_copy(buf.at[s], xs_out.at[pl.ds(0, tm), :], sem.at[s]).wait()

    def step(s):
        buf[s] = _rms(h_ref[...], ng_ref[...])
        for j in range(tm):
            for k in range(TOP_K):
                r = dest_ref[0, 0, j * TOP_K + k]
                pltpu.make_async_copy(buf.at[s, pl.ds(j, 1), :], xs_out.at[pl.ds(r, 1), :],
                                      sem.at[s]).start(priority=k % 2)

        @pl.when(i >= 1)
        def _():
            wait_tile(1 - s)

        @pl.when(i == pl.num_programs(0) - 1)
        def _():
            wait_tile(s)

    for s in range(2):
        pl.when(i % 2 == s)(functools.partial(step, s))


def _moe_dispatch(h, norm_g, dest, pad_rows, n_rows):
    n, d = h.shape
    tm = min(256, n)
    return pl.pallas_call(
        _dispatch_body,
        grid_spec=pltpu.PrefetchScalarGridSpec(
            num_scalar_prefetch=1, grid=(n // tm,),
            in_specs=[pl.BlockSpec((1, 1, tm * TOP_K), lambda i, pr: (i, 0, 0), memory_space=pltpu.SMEM),
                      pl.BlockSpec((tm, d), lambda i, pr: (i, 0)),
                      pl.BlockSpec((1, d), lambda i, pr: (0, 0))],
            out_specs=pl.BlockSpec(memory_space=pl.ANY),
            scratch_shapes=[pltpu.VMEM((2, tm, d), F32), pltpu.VMEM((MOE_ROWS, d), F32),
                            pltpu.SemaphoreType.DMA((2,)), pltpu.SemaphoreType.DMA(())]),
        out_shape=jax.ShapeDtypeStruct((n_rows, d), F32),
        compiler_params=_cparams(("arbitrary",)),
        name="moe_dispatch",
    )(pad_rows, dest.reshape(n // tm, 1, tm * TOP_K), h, norm_g.reshape(1, d))


def _expert_body(be_ref, nu_ref, xs_ref, wgu_ref, bgu_ref, wd_ref, bd_ref, ys_ref, wgu_bf, wd_bf):
    i = pl.program_id(0)
    ed = wd_ref.shape[2]

    @pl.when((i == 0) | (be_ref[i] != be_ref[jnp.maximum(i - 1, 0)]))
    def _():
        wgu_bf[...] = wgu_ref[0, 0].astype(BF16)
        wd_bf[...] = wd_ref[0, 0].astype(BF16)

    @pl.when(i < nu_ref[0])
    def _():
        x = xs_ref[...].astype(BF16)
        gu = _dot(x, wgu_bf[...]) + bgu_ref[0]
        gate = jnp.minimum(gu[:, :ed], SWIGLU_LIMIT)
        up = jnp.clip(gu[:, ed:], -SWIGLU_LIMIT, SWIGLU_LIMIT)
        glu = gate * jax.nn.sigmoid(gate * SWIGLU_ALPHA)
        ys_ref[...] = _dot(((up + 1.0) * glu).astype(BF16), wd_bf[...]) + bd_ref[0]

    @pl.when(i >= nu_ref[0])
    def _():
        ys_ref[...] = jnp.zeros_like(ys_ref)


def _moe_experts(xs, blk_expert, n_used, w_gu, b_gu, w_d, b_d, layer):
    n_rows, d = xs.shape
    ed = w_d.shape[2]
    n_blk = n_rows // MOE_ROWS
    return pl.pallas_call(
        _expert_body,
        grid_spec=pltpu.PrefetchScalarGridSpec(
            num_scalar_prefetch=2, grid=(n_blk,),
            in_specs=[pl.BlockSpec((MOE_ROWS, d), lambda i, be, nu: (jnp.maximum(jnp.minimum(i, nu[0] - 1), 0), 0)),
                      pl.BlockSpec((1, 1, d, 2 * ed), lambda i, be, nu: (layer, be[i], 0, 0)),
                      pl.BlockSpec((1, 1, 2 * ed), lambda i, be, nu: (be[i], 0, 0)),
                      pl.BlockSpec((1, 1, ed, d), lambda i, be, nu: (layer, be[i], 0, 0)),
                      pl.BlockSpec((1, 1, d), lambda i, be, nu: (be[i], 0, 0))],
            out_specs=pl.BlockSpec((MOE_ROWS, d), lambda i, be, nu: (i, 0)),
            scratch_shapes=[pltpu.VMEM((d, 2 * ed), BF16), pltpu.VMEM((ed, d), BF16)]),
        out_shape=jax.ShapeDtypeStruct((n_rows, d), F32),
        compiler_params=_cparams(("arbitrary",)),
        name="moe_experts",
    )(blk_expert, n_used, xs, w_gu, b_gu.reshape(N_EXPERTS, 1, 2 * ed), w_d, b_d.reshape(N_EXPERTS, 1, d))


def _combine_body(dest_ref, next_ref, rt_ref, h_ref, p_ref, pw_ref, pg_ref, pn_ref, ys_hbm, o_ref, buf, sem):
    i = pl.program_id(0)
    n = pl.num_programs(0)
    tm = h_ref.shape[0]
    slot = i % 2

    def row_copy(idx_ref, j, k, s):
        r = idx_ref[0, 0, j * TOP_K + k]
        return pltpu.make_async_copy(ys_hbm.at[pl.ds(r, 1), :], buf.at[s, k, pl.ds(j, 1), :], sem.at[s])

    def wait_tile(s):
        for k in range(TOP_K):
            pltpu.make_async_copy(ys_hbm.at[pl.ds(0, tm), :], buf.at[s, k], sem.at[s]).wait()

    @pl.when(i == 0)
    def _():
        def first(j, carry):
            for k in range(TOP_K):
                row_copy(dest_ref, j, k, 0).start()
            return carry
        lax.fori_loop(0, tm, first, 0)

    def step(s):
        wait_tile(s)
        for j in range(tm):
            for k in range(TOP_K):
                row_copy(next_ref, j, k, 1 - s).start(priority=k % 2)
        rt = rt_ref[...]
        h = h_ref[...]
        for k in range(TOP_K):
            h = h + rt[:, TOP_K + k:TOP_K + k + 1] * buf[s, k]
        emb = _dot(p_ref[...].astype(BF16), pw_ref[...])
        gate = jax.nn.sigmoid(_dot(_rms(h, pn_ref[...]).astype(BF16), pg_ref[...]))
        o_ref[...] = h + emb * gate

        @pl.when(i == n - 1)
        def _():
            wait_tile(1 - s)

    for s in range(2):
        pl.when(slot == s)(functools.partial(step, s))


def _moe_combine_ple(h, route, dest, ys, p, ple_w, ple_gate_w, ple_norm):
    n, d = h.shape
    pd = p.shape[1]
    tm = min(256, n)
    nt = n // tm
    full = lambda *shape: pl.BlockSpec(shape, lambda i: (0,) * len(shape))
    dest3 = dest.reshape(nt, 1, tm * TOP_K)
    return pl.pallas_call(
        _combine_body,
        grid=(nt,),
        in_specs=[pl.BlockSpec((1, 1, tm * TOP_K), lambda i: (i, 0, 0), memory_space=pltpu.SMEM),
                  pl.BlockSpec((1, 1, tm * TOP_K), lambda i: (jnp.minimum(i + 1, nt - 1), 0, 0),
                               memory_space=pltpu.SMEM),
                  pl.BlockSpec((tm, LANES), lambda i: (i, 0)),
                  pl.BlockSpec((tm, d), lambda i: (i, 0)),
                  pl.BlockSpec((tm, pd), lambda i: (i, 0)),
                  full(pd, d), full(d, d), full(1, d),
                  pl.BlockSpec(memory_space=pl.ANY)],
        out_specs=pl.BlockSpec((tm, d), lambda i: (i, 0)),
        out_shape=jax.ShapeDtypeStruct((n, d), F32),
        scratch_shapes=[pltpu.VMEM((2, TOP_K, tm, d), F32), pltpu.SemaphoreType.DMA((2,))],
        compiler_params=_cparams(("arbitrary",)),
        name="moe_combine_ple",
    )(dest3, dest3, route, h, p, ple_w.astype(BF16), ple_gate_w.astype(BF16), ple_norm.reshape(1, d), ys)


def _moe_ple_layer(h, norm_g, router_w, router_b, w_gu, b_gu, w_d, b_d, layer, p, ple_w, ple_gate_w, ple_norm):
    n, _ = h.shape
    route, counts = _moe_route(h, norm_g, router_w, router_b)
    counts = counts[0].astype(jnp.int32)
    pad_counts = (counts + MOE_ROWS - 1) // MOE_ROWS * MOE_ROWS
    pad_ends = jnp.cumsum(pad_counts)
    pad_starts = pad_ends - pad_counts
    top_idx = route[:, :TOP_K].astype(jnp.int32)
    rank = route[:, 2 * TOP_K:3 * TOP_K].astype(jnp.int32)
    dest = (pad_starts[top_idx] + rank).reshape(-1)
    n_blk = -(-(n * TOP_K) // MOE_ROWS) + N_EXPERTS
    blk_start = jnp.arange(n_blk, dtype=jnp.int32) * MOE_ROWS
    blk_expert = jnp.minimum(jnp.sum((pad_ends[None, :] <= blk_start[:, None]).astype(jnp.int32), axis=1),
                             N_EXPERTS - 1)
    n_used = (pad_ends[-1:] // MOE_ROWS).astype(jnp.int32)
    tail = jnp.minimum(pad_ends[-1] + jnp.arange(N_EXPERTS, dtype=jnp.int32) * MOE_ROWS, (n_blk - 1) * MOE_ROWS)
    clear = jnp.concatenate([jnp.maximum(pad_ends - MOE_ROWS, 0), tail]).astype(jnp.int32)
    xs = _moe_dispatch(h, norm_g, dest, clear, n_blk * MOE_ROWS)
    ys = _moe_experts(xs, blk_expert, n_used, w_gu, b_gu, w_d, b_d, layer)
    return _moe_combine_ple(h, route, dest, ys, p, ple_w, ple_gate_w, ple_norm)


def _kv_body(h_ref, ng_ref, w_ref, seg_ref, kn_ref, kc_ref, vc_ref, ks_ref, vs_ref, kw_ref, vw_ref):
    ts = h_ref.shape[1]
    gw = N_KV_GROUPS * HEAD_DIM
    st = pl.program_id(1)
    hn = _rms(h_ref[0], ng_ref[...]).astype(BF16)
    kv = _dot(hn, w_ref[...])

    def knorm(x, j):
        ms = _dot_split(x * x, seg_ref[...])
        return x * lax.rsqrt(ms + NORM_EPS) * kn_ref[j]

    k_c, v_c = kv[:, 0:gw], kv[:, gw:2 * gw]
    k_s, v_s = knorm(kv[:, 2 * gw:3 * gw], 1), kv[:, 3 * gw:4 * gw]
    k_w, v_w = knorm(kv[:, 4 * gw:5 * gw], 2), kv[:, 5 * gw:6 * gw]
    tok = st * ts + lax.broadcasted_iota(jnp.int32, (ts, LANES), 0)
    blk = lax.broadcasted_iota(jnp.int32, (ts, LANES), 1)
    onehot = (tok // SEL_BLOCK == blk).astype(BF16)
    one_col = (blk == HEAD_DIM).astype(F32)
    for g in range(N_KV_GROUPS):
        kc_ref[0, g] = _low_lanes(_head_lanes(k_c, g), 0.0)
        vc_ref[0, g] = _low_lanes(_head_lanes(v_c, g), 0.0)
        ks_ref[0, g] = jnp.concatenate([_low_lanes(_head_lanes(k_s, g), 0.0).astype(BF16), onehot], axis=1)
        vs_ref[0, g] = _low_lanes(_head_lanes(v_s, g), one_col).astype(BF16)
        kw_ref[0, g] = _low_lanes(_head_lanes(k_w, g), 0.0).astype(BF16)
        vw_ref[0, g] = _low_lanes(_head_lanes(v_w, g), one_col).astype(BF16)


def _kv_project(h3, kv_norm, kv_w, k_norm):
    b, s, d = h3.shape
    gw = N_KV_GROUPS * HEAD_DIM
    ts = min(512, s)
    seg = jnp.asarray(np.tile(np.kron(np.eye(N_KV_GROUPS), np.full((HEAD_DIM, HEAD_DIM), 1.0 / HEAD_DIM)),
                              (2, 1)), F32).astype(BF16)
    kn = jnp.tile(k_norm, (1, N_KV_GROUPS)).reshape(N_BRANCH, 1, gw)
    full = lambda *shape: pl.BlockSpec(shape, lambda bi, si: (0,) * len(shape))
    hd = lambda w: pl.BlockSpec((1, N_KV_GROUPS, ts, w), lambda bi, si: (bi, 0, si, 0))
    sds = lambda w, dt: jax.ShapeDtypeStruct((b, N_KV_GROUPS, s, w), dt)
    return pl.pallas_call(
        _kv_body,
        grid=(b, s // ts),
        in_specs=[pl.BlockSpec((1, ts, d), lambda bi, si: (bi, si, 0)), full(1, d), full(d, N_KV_SLOTS * gw),
                  full(2 * gw, gw), full(N_BRANCH, 1, gw)],
        out_specs=[hd(LANES), hd(LANES), hd(2 * LANES), hd(LANES), hd(LANES), hd(LANES)],
        out_shape=[sds(LANES, F32), sds(LANES, F32), sds(2 * LANES, BF16), sds(LANES, BF16),
                   sds(LANES, BF16), sds(LANES, BF16)],
        compiler_params=_cparams(("arbitrary", "arbitrary")),
        name="kv_project",
    )(h3, kv_norm.reshape(1, d), kv_w.astype(BF16), seg, kn)


def _compress_body(kc_ref, vc_ref, pos_ref, w1_ref, b1_ref, w2_ref, kn_ref, ko_ref, vo_ref):
    nc = kc_ref.shape[2] // CMP_STRIDE
    row = lax.broadcasted_iota(jnp.int32, (nc, 1), 0)
    valid = row < nc - 1
    lane = lax.broadcasted_iota(jnp.int32, (nc, LANES), 1)

    def chunks(ref):
        parts = []
        for l in range(0, CMP_STRIDE, 2):
            even = ref[0, 0, pl.ds(l, nc, stride=CMP_STRIDE), :]
            odd = ref[0, 0, pl.ds(l + 1, nc, stride=CMP_STRIDE), :]
            parts.append(jnp.where(lane < HEAD_DIM, even, pltpu.roll(odd, HEAD_DIM, 1)))
        return jnp.concatenate(parts, axis=1)

    def compress(c, j):
        a = _dot((c + pos_ref[j, 0]).astype(BF16), w1_ref[j, 0])
        bm = _dot((c + pos_ref[j, 1]).astype(BF16), w1_ref[j, 1])
        nxt = pltpu.roll(bm, nc - 1, 0)
        hid = jax.nn.gelu(a + nxt + b1_ref[j])
        return _dot(hid.astype(BF16), w2_ref[j])

    kraw = compress(chunks(kc_ref), 0)
    ms = jnp.sum(kraw * kraw, axis=-1, keepdims=True) * (1.0 / HEAD_DIM)
    kcmp = kraw * lax.rsqrt(ms + NORM_EPS) * kn_ref[...]
    vcmp = jnp.where(valid, compress(chunks(vc_ref), 1), 0.0)
    flag = (lane == HEAD_DIM).astype(F32)
    ko_ref[0, 0, 0:nc, :] = flag
    ko_ref[0, 0, nc:2 * nc, :] = jnp.where(valid, kcmp, flag)
    vo_ref[0, 0, 0:nc, :] = flag
    vo_ref[0, 0, nc:2 * nc, :] = jnp.where(lane == HEAD_DIM, 1.0, vcmp)


def _compress(kc, vc, cmp_pos, cmp_w1, cmp_b1, cmp_w2, k_norm0):
    b, g, s, _ = kc.shape
    dh = HEAD_DIM
    nc = s // CMP_STRIDE
    half = CMP_STRIDE * dh
    pos = cmp_pos.reshape(2, 2, 1, half)
    w1 = cmp_w1.reshape(2, 2, half, CMP_HIDDEN).astype(BF16)
    w2 = jnp.pad(cmp_w2, ((0, 0), (0, 0), (0, LANES - dh))).astype(BF16)
    kn = jnp.pad(k_norm0, (0, LANES - dh)).reshape(1, LANES)
    full = lambda *shape: pl.BlockSpec(shape, lambda bi, gi: (0,) * len(shape))
    blk = pl.BlockSpec((1, 1, s, LANES), lambda bi, gi: (bi, gi, 0, 0))
    out = pl.BlockSpec((1, 1, 2 * nc, LANES), lambda bi, gi: (bi, gi, 0, 0))
    return pl.pallas_call(
        _compress_body,
        grid=(b, g),
        in_specs=[blk, blk, full(2, 2, 1, half), full(2, 2, half, CMP_HIDDEN), full(2, 1, CMP_HIDDEN),
                  full(2, CMP_HIDDEN, LANES), full(1, LANES)],
        out_specs=[out, out],
        out_shape=[jax.ShapeDtypeStruct((b, g, 2 * nc, LANES), F32)] * 2,
        compiler_params=_cparams(("arbitrary", "arbitrary")),
        name="kv_compress",
    )(kc, vc, pos, w1, cmp_b1.reshape(2, 1, CMP_HIDDEN), w2, kn)


def _qproj_body(h_ref, ng_ref, w_ref, bg_ref, ind_ref, indt_ref, qn_ref, q_ref, gate_ref):
    hd = N_HEADS * HEAD_DIM
    xn = _rms(h_ref[0], ng_ref[...]).astype(BF16)
    proj = _dot(xn, w_ref[...])
    q = proj[:, :hd]
    ms = _dot_split(q * q, ind_ref[...]) * (1.0 / HEAD_DIM)
    scale = _dot_split(lax.rsqrt(ms + NORM_EPS), indt_ref[...])
    qn = q * scale * qn_ref[...] * (HEAD_DIM ** -0.5 * LOG2E)
    lane = lax.broadcasted_iota(jnp.int32, (q.shape[0], LANES), 1)
    fill = jnp.where(lane == HEAD_DIM, NEG_INF, 0.0)
    for h in range(N_HEADS):
        q_ref[0, h] = _low_lanes(_head_lanes(qn, h), fill).astype(BF16)
    gate_ref[0] = jax.nn.sigmoid(proj[:, hd:] + bg_ref[...])


def _q_project(h3, norm_g, w_in, b_gate, q_norm):
    b, s, d = h3.shape
    hd = N_HEADS * HEAD_DIM
    ng = N_BRANCH * N_HEADS
    ts = min(512, s)
    w = jnp.pad(w_in, ((0, 0), (0, LANES - ng))).astype(BF16)
    bg = jnp.pad(b_gate, (0, LANES - ng)).reshape(1, LANES)
    ind = np.zeros((hd, LANES), np.float32)
    ind[np.arange(hd), np.arange(hd) // HEAD_DIM] = 1.0
    full = lambda *shape: pl.BlockSpec(shape, lambda bi, si: (0,) * len(shape))
    return pl.pallas_call(
        _qproj_body,
        grid=(b, s // ts),
        in_specs=[pl.BlockSpec((1, ts, d), lambda bi, si: (bi, si, 0)), full(1, d), full(d, hd + LANES),
                  full(1, LANES), full(2 * hd, LANES), full(2 * LANES, hd), full(1, hd)],
        out_specs=[pl.BlockSpec((1, N_HEADS, ts, LANES), lambda bi, si: (bi, 0, si, 0)),
                   pl.BlockSpec((1, ts, LANES), lambda bi, si: (bi, si, 0))],
        out_shape=[jax.ShapeDtypeStruct((b, N_HEADS, s, LANES), BF16),
                   jax.ShapeDtypeStruct((b, s, LANES), F32)],
        compiler_params=_cparams(("arbitrary", "arbitrary")),
        name="nsa_qproj",
    )(h3, norm_g.reshape(1, d), w, bg, jnp.asarray(np.tile(ind, (2, 1))).astype(BF16),
      jnp.asarray(np.tile(ind.T, (2, 1))).astype(BF16), jnp.tile(q_norm, N_HEADS).reshape(1, hd))


def _t5_bucket_np(dist):
    n = np.maximum(dist, 0)
    max_exact = REL_BUCKETS // 2
    nf = np.maximum(n, 1).astype(np.float64)
    large = max_exact + (np.log(nf / max_exact) / math.log(REL_MAX_DIST / max_exact)
                         * (REL_BUCKETS - max_exact)).astype(np.int64)
    return np.where(n < max_exact, n, np.minimum(large, REL_BUCKETS - 1))


def _bias_table(rel_bias, dist, valid):
    r = N_HEADS // N_KV_GROUPS
    tab = rel_bias.astype(F32).T.reshape(N_KV_GROUPS, r, REL_BUCKETS)
    onehot = jnp.asarray(_t5_bucket_np(dist)[..., None] == np.arange(REL_BUCKETS), F32)
    bias = jnp.einsum('xqln,grn->gxrql', onehot, tab, precision=lax.Precision.HIGHEST) * LOG2E
    return jnp.where(jnp.asarray(valid)[None, :, None, :, :], bias, NEG_INF)


def _n_delta(seq):
    d = np.arange(seq + SEL_BLOCK)
    bk = _t5_bucket_np(d)
    change = np.nonzero(bk[1:] != bk[:-1])[0]
    d_const = int(change[-1]) + 1 if change.size else 0
    return -(-(d_const + SEL_BLOCK - 1) // SEL_BLOCK) + 1


def _attn_tables(rel_bias, seq):
    r = N_HEADS // N_KV_GROUPS
    qi = np.arange(Q_BLOCK)[:, None]
    rows = lambda t: jnp.transpose(t, (0, 2, 1, 3, 4)).reshape(N_KV_GROUPS, r * Q_PAIR * Q_BLOCK, t.shape[-1])
    nc = seq // CMP_STRIDE
    j = np.arange(nc)[None, :]
    dist_c = np.stack([qi - (CMP_BLOCK - 1) - Q_BLOCK * (Q_PAIR - u) + CMP_STRIDE * (nc - j)
                       for u in range(Q_PAIR)])
    rc = rows(_bias_table(rel_bias, dist_c, dist_c >= 0))
    nd = _n_delta(seq)
    delta = np.arange(-1, nd + 1)[:, None, None]
    kj = np.arange(2 * SEL_BLOCK)[None, None, :]
    dist_s = SEL_BLOCK * (delta - kj // SEL_BLOCK) + qi[None] - kj % SEL_BLOCK
    bt = _bias_table(rel_bias, dist_s, dist_s >= 0)
    jw = np.arange(WINDOW + Q_PAIR * Q_BLOCK)[None, :]
    dist_w = np.stack([Q_BLOCK * u + qi - jw + WINDOW for u in range(Q_PAIR)])
    wb = rows(_bias_table(rel_bias, dist_w, (dist_w >= 0) & (dist_w < WINDOW)))
    n_sel = seq // SEL_BLOCK
    cs = np.arange(nc) * CMP_STRIDE
    ss = np.arange(n_sel) * SEL_BLOCK
    ov = np.clip(np.minimum(cs[:, None] + CMP_BLOCK, ss[None, :] + SEL_BLOCK)
                 - np.maximum(cs[:, None], ss[None, :]), 0, None) / CMP_BLOCK
    return rc, bt, wb, jnp.asarray(ov.T.astype(np.float32)).astype(BF16)


def _attn_body(q_ref, gate_ref, kc_ref, vc_ref, ks_ref, vs_ref, kw_ref, vw_ref, rc_ref, bt_ref, wb_ref, ovt_ref,
               o_ref, sa_ref, sb_ref):
    g = pl.program_id(1)
    i0 = Q_PAIR * pl.program_id(2)
    r = q_ref.shape[1]
    pq = Q_PAIR * Q_BLOCK
    rq = r * pq
    nc = rc_ref.shape[2]
    n_sel = ovt_ref.shape[0]
    nd = bt_ref.shape[1] - 2
    wl = wb_ref.shape[2]
    per = SEL_BLOCK // CMP_STRIDE

    heads = [slice(h * pq, (h + 1) * pq) for h in range(r)]
    q_pad = q_ref[0].reshape(rq, LANES)

    end = pl.multiple_of(per * (i0 + Q_PAIR), Q_PAIR * per)
    kcw = kc_ref[0, 0, pl.ds(end, nc), :].astype(BF16)
    vcw = vc_ref[0, 0, pl.ds(end, nc), :].astype(BF16)
    qpos = i0 * Q_BLOCK + lax.broadcasted_iota(jnp.int32, (rq, 1), 0) % pq
    sc = _dot_nt(q_pad, kcw) + rc_ref[0]
    p_c = jnp.exp2((sc - jnp.max(sc, axis=-1, keepdims=True)).astype(BF16))
    o_c = _normalize(_dot(p_c, vcw)) * (qpos >= CMP_BLOCK - 1).astype(F32)

    imp_r = _dot_nt(ovt_ref[...], p_c)
    inv_l = 1.0 / _dot_nt(jnp.ones((8, nc), BF16), p_c)[0:1]
    imp = imp_r[:, heads[0]] * inv_l[:, heads[0]]
    for h in range(1, r):
        imp = imp + imp_r[:, heads[h]] * inv_l[:, heads[h]]
    tpos = i0 * Q_BLOCK + lax.broadcasted_iota(jnp.int32, (1, pq), 1)
    imp = imp * (tpos >= CMP_BLOCK - 1).astype(F32)

    anchor = (jnp.max(imp, axis=(0, 1), keepdims=True) * 0.0).astype(BF16)
    ws = pl.multiple_of(i0 * Q_BLOCK, pq)
    sw = _dot_nt(q_pad + anchor, kw_ref[0, 0, pl.ds(ws, wl), :]) + wb_ref[0]
    p_w = jnp.exp2((sw - jnp.max(sw, axis=-1, keepdims=True)).astype(BF16))
    o_w = _normalize(_dot(p_w, vw_ref[0, 0, pl.ds(ws, wl), :]))

    shift = i0 + Q_PAIR
    blk_rel = lax.broadcasted_iota(jnp.int32, (n_sel, pq), 0)
    blk = blk_rel + shift - n_sel
    cur = i0 + lax.broadcasted_iota(jnp.int32, (n_sel, pq), 1) // Q_BLOCK
    forced = (blk == 0) | (blk == cur) | (blk == cur - 1)
    imp = jnp.where(forced, FORCE, jnp.where(blk > cur, NEG_INF, imp))
    imp = jnp.where(blk < 0, -jnp.inf, imp)
    ids = blk_rel.astype(F32)
    sel = jnp.zeros((n_sel, pq), F32)
    for _ in range(min(SEL_TOP, n_sel)):
        mx = jnp.max(imp, axis=0, keepdims=True)
        ix = jnp.min(jnp.where(imp == mx, ids, float(n_sel)), axis=0, keepdims=True)
        hit = ids == ix
        sel = jnp.where(hit, 1.0, sel)
        imp = jnp.where(hit, -jnp.inf, imp)
    unsel = jnp.where((sel > 0.0) & (blk >= 0), 0.0, NEG_INF)
    unsel = pltpu.roll(unsel.T, shift % n_sel, 1).astype(BF16)
    if n_sel < LANES:
        unsel = jnp.concatenate([unsel, jnp.zeros((pq, LANES - n_sel), BF16)], axis=1)

    q_aug = jnp.concatenate([q_pad, jnp.concatenate([unsel] * r, axis=0)], axis=1)
    kchunk = sa_ref.shape[1]
    n_chunks = ks_ref.shape[2] // kchunk
    cblocks = kchunk // SEL_BLOCK
    pairs = cblocks // 2

    def scores_to(dst, c):
        start = pl.multiple_of(jnp.minimum(c, n_chunks - 1) * kchunk, kchunk)
        sc = _dot_nt(q_aug, ks_ref[0, 0, pl.ds(start, kchunk), :])
        d0 = i0 - c * cblocks
        tiles = [[jnp.clip(d0 + u - 2 * pm, -1, nd) + 1 for pm in range(pairs)] for u in range(Q_PAIR)]
        dst[...] = sc + jnp.concatenate(
            [jnp.concatenate([bt_ref[0, t, h] for t in tiles[u]], axis=1)
             for h in range(r) for u in range(Q_PAIR)], axis=0)

    def absorb(src, c, m, acc):
        start = pl.multiple_of(c * kchunk, kchunk)
        sc = src[...]
        m_new = jnp.maximum(m, jnp.max(sc, axis=-1, keepdims=True))
        p = jnp.exp2((sc - m_new).astype(BF16))
        acc = jnp.exp2(m - m_new) * acc + _dot(p, vs_ref[0, 0, pl.ds(start, kchunk), :])
        return m_new, acc

    def two_chunks(j, carry):
        scores_to(sb_ref, 2 * j + 1)
        carry = absorb(sa_ref, 2 * j, *carry)
        scores_to(sa_ref, 2 * j + 2)
        return absorb(sb_ref, 2 * j + 1, *carry)

    scores_to(sa_ref, 0)
    init = (jnp.full((rq, 1), NEG_INF, F32), jnp.zeros((rq, LANES), F32))
    rem = (i0 + Q_PAIR) % (2 * cblocks)
    trips = (i0 + Q_PAIR) // (2 * cblocks) + (rem > cblocks).astype(jnp.int32)
    state = lax.fori_loop(0, trips, two_chunks, init)
    _, acc_s = lax.cond((rem > 0) & (rem <= cblocks),
                        lambda m, acc: absorb(sa_ref, 2 * trips, m, acc), lambda m, acc: (m, acc), *state)
    o_s = _normalize(acc_s)

    gates = gate_ref[0]
    glane = lax.broadcasted_iota(jnp.int32, gates.shape, 1)
    outs = []
    for h in range(r):
        head = g * r + h
        gs = [jnp.sum(jnp.where(glane == br * N_HEADS + head, gates, 0.0), axis=-1, keepdims=True)
              for br in range(N_BRANCH)]
        outs.append(gs[0] * o_c[heads[h]] + gs[1] * o_s[heads[h]] + gs[2] * o_w[heads[h]])
    o_ref[0] = jnp.concatenate(
        [outs[h] + pltpu.roll(outs[h + 1], HEAD_DIM, 1) for h in range(0, r, 2)], axis=1)


def _nsa_attention(q, gates, kc_pad, vc_pad, ks_aug, vs, kw_pad, vw_pad, tables):
    b, _, s, _ = q.shape
    r = N_HEADS // N_KV_GROUPS
    rc, bt, wb, ov = tables
    n_qb = s // Q_BLOCK
    per_bg = lambda a: pl.BlockSpec((1, 1) + a.shape[2:], lambda bi, gi, qi: (bi, gi, 0, 0))
    per_g = lambda a: pl.BlockSpec((1,) + a.shape[1:], lambda bi, gi, qi: (gi,) + (0,) * (a.ndim - 1))
    pq = Q_PAIR * Q_BLOCK
    return pl.pallas_call(
        _attn_body,
        grid=(b, N_KV_GROUPS, n_qb // Q_PAIR),
        in_specs=[pl.BlockSpec((1, r, pq, LANES), lambda bi, gi, qi: (bi, gi, qi, 0)),
                  pl.BlockSpec((1, pq, LANES), lambda bi, gi, qi: (bi, qi, 0)),
                  per_bg(kc_pad), per_bg(vc_pad), per_bg(ks_aug), per_bg(vs), per_bg(kw_pad), per_bg(vw_pad),
                  per_g(rc), per_g(bt), per_g(wb),
                  pl.BlockSpec(ov.shape, lambda bi, gi, qi: (0, 0))],
        out_specs=pl.BlockSpec((1, pq, r * HEAD_DIM), lambda bi, gi, qi: (bi, qi, gi)),
        out_shape=jax.ShapeDtypeStruct((b, s, N_HEADS * HEAD_DIM), F32),
        scratch_shapes=[pltpu.VMEM((r * pq, min(KEY_CHUNK, s // 2)), F32)] * 2,
        compiler_params=_cparams(("arbitrary", "arbitrary", "arbitrary")),
        name="nsa_attention",
    )(q, gates, kc_pad, vc_pad, ks_aug, vs, kw_pad, vw_pad, rc, bt, wb, ov)


def _outproj_body(a_ref, h_ref, w_ref, o_ref):
    o_ref[...] = h_ref[...] + _dot(a_ref[...].astype(BF16), w_ref[...])


def _out_project(attn, h, w_out):
    n, d = h.shape
    hd = attn.shape[1]
    tm = min(512, n)
    return pl.pallas_call(
        _outproj_body,
        grid=(n // tm,),
        in_specs=[pl.BlockSpec((tm, hd), lambda i: (i, 0)), pl.BlockSpec((tm, d), lambda i: (i, 0)),
                  pl.BlockSpec((hd, d), lambda i: (0, 0))],
        out_specs=pl.BlockSpec((tm, d), lambda i: (i, 0)),
        out_shape=jax.ShapeDtypeStruct((n, d), F32),
        compiler_params=_cparams(("arbitrary",)),
        name="nsa_outproj",
    )(attn, h, w_out.astype(BF16))


def kernel(x, p, rel_bias, norm_mix, norm_ffn, a_w_in, a_ln_g, a_ln_b, a_w_s, a_b_s, a_w_out, kv_norm, kv_w, cmp_pos, cmp_w1, cmp_b1, cmp_w2, k_norm, b_w_in, b_b_gate, q_norm, b_w_out, router_w, router_b, e_w_gu, e_b_gu, e_w_d, e_b_d, ple_w, ple_gate_w, ple_norm):
    b, s, d = x.shape
    n = b * s
    pf = p.reshape(p.shape[0], n, p.shape[-1])

    def moe_ple(h, i):
        return _moe_ple_layer(h, norm_ffn[i], router_w[i], router_b[i], e_w_gu, e_b_gu[i], e_w_d, e_b_d[i], i,
                              pf[i], ple_w[i], ple_gate_w[i], ple_norm[i])

    h = _gmlp_layer(x.reshape(n, d), norm_mix[0], a_w_in[0], a_ln_g[0], a_ln_b[0], a_w_s[0], a_b_s[0], a_w_out[0])
    h = moe_ple(h, 0)

    h3 = h.reshape(b, s, d)
    kc, vc, ks_aug, vs, kw, vw = _kv_project(h3, kv_norm, kv_w, k_norm)
    kc_pad, vc_pad = _compress(kc, vc, cmp_pos, cmp_w1, cmp_b1, cmp_w2, k_norm[0])
    front = jnp.zeros((b, N_KV_GROUPS, WINDOW, LANES), BF16).at[..., HEAD_DIM].set(1.0)
    kw_pad = jnp.concatenate([front, kw, front[:, :, :2 * Q_BLOCK]], axis=2)
    vw_pad = jnp.pad(vw, ((0, 0), (0, 0), (WINDOW, 2 * Q_BLOCK), (0, 0)))

    q, gates = _q_project(h3, norm_mix[1], b_w_in[0], b_b_gate[0], q_norm[0])
    attn = _nsa_attention(q, gates, kc_pad, vc_pad, ks_aug, vs, kw_pad, vw_pad, _attn_tables(rel_bias, s))
    h = _out_project(attn.reshape(n, -1), h, b_w_out[0])
    h = moe_ple(h, 1)
    return h.reshape(b, s, d)
```

```python
import functools
import math

import numpy as np
import jax
import jax.numpy as jnp
from jax import lax
from jax.experimental import pallas as pl
from jax.experimental.pallas import tpu as pltpu

GMLP_CHUNK = 128
GMLP_GROUPS = 8
N_HEADS = 16
HEAD_DIM = 64
N_KV_GROUPS = 4
N_BRANCH = 3
CMP_BLOCK = 32
CMP_STRIDE = 16
CMP_HIDDEN = 256
SEL_BLOCK = 64
SEL_TOP = 16
WINDOW = 512
Q_BLOCK = 64
N_KV_SLOTS = 6
REL_BUCKETS = 32
REL_MAX_DIST = 2048
N_EXPERTS = 32
TOP_K = 4
SWIGLU_LIMIT = 7.0
SWIGLU_ALPHA = 1.702
NORM_EPS = 1e-6
NEG_INF = -1e30
FORCE = 1e30
LOG2E = math.log2(math.e)

LANES = 128
MOE_ROWS = 512
KEY_CHUNK = 512
Q_PAIR = 4
VMEM_LIMIT = 56 * 1024 * 1024

F32 = jnp.float32
BF16 = jnp.bfloat16


def _cparams(sem):
    return pltpu.CompilerParams(dimension_semantics=sem, vmem_limit_bytes=VMEM_LIMIT)


def _rms(x, g):
    return x * lax.rsqrt(jnp.mean(x * x, axis=-1, keepdims=True) + NORM_EPS) * g


def _dot(a, b):
    return jnp.dot(a, b, preferred_element_type=F32)


def _dot_nt(a, b):
    return lax.dot_general(a, b, (((1,), (1,)), ((), ())), preferred_element_type=F32)


def _dot_split(a, b2):
    hi = a.astype(BF16)
    lo = (a - hi.astype(F32)).astype(BF16)
    return _dot(jnp.concatenate([hi, lo], axis=1), b2)


def _normalize(acc):
    lane = lax.broadcasted_iota(jnp.int32, acc.shape, 1)
    denom = jnp.sum(jnp.where(lane == HEAD_DIM, acc, 0.0), axis=-1, keepdims=True)
    return jnp.where(lane < HEAD_DIM, acc / denom, 0.0)


def _head_lanes(x, idx):
    base = idx * HEAD_DIM // LANES * LANES
    y = x[:, base:base + LANES]
    return y if idx * HEAD_DIM == base else pltpu.roll(y, LANES - HEAD_DIM, 1)


def _low_lanes(x, fill):
    lane = lax.broadcasted_iota(jnp.int32, x.shape, 1)
    return jnp.where(lane < HEAD_DIM, x, fill)


def _argmax_first(x, ids, n):
    mx = jnp.max(x, axis=-1, keepdims=True)
    return mx, jnp.min(jnp.where(x == mx, ids, float(n)), axis=-1, keepdims=True)


def _gmlp_body(x_ref, nm_ref, win_ref, lng_ref, lnb_ref, ws_ref, bs_ref, wout_ref, o_ref):
    tm = x_ref.shape[0]
    gd = win_ref.shape[1] // 2
    gdim = gd // GMLP_GROUPS
    x = x_ref[...]
    xn = _rms(x, nm_ref[...]).astype(BF16)
    z = jax.nn.gelu(_dot(xn, win_ref[...]))
    u = z[:, :gd]
    v = z[:, gd:]
    mu = jnp.mean(v, axis=-1, keepdims=True)
    vc = v - mu
    vln = vc * lax.rsqrt(jnp.mean(vc * vc, axis=-1, keepdims=True) + NORM_EPS) * lng_ref[...] + lnb_ref[...]
    vb = vln.astype(BF16)
    row = lax.broadcasted_iota(jnp.int32, (GMLP_CHUNK, GMLP_CHUNK), 0)
    col = lax.broadcasted_iota(jnp.int32, (GMLP_CHUNK, GMLP_CHUNK), 1)
    causal = row >= col
    chunks = []
    for c in range(tm // GMLP_CHUNK):
        cols = []
        for g in range(GMLP_GROUPS):
            wsg = jnp.where(causal, ws_ref[g], 0.0).astype(BF16)
            vg = vb[c * GMLP_CHUNK:(c + 1) * GMLP_CHUNK, g * gdim:(g + 1) * gdim]
            cols.append(_dot(wsg, vg) + bs_ref[g])
        chunks.append(jnp.concatenate(cols, axis=1))
    mixed = jnp.concatenate(chunks, axis=0)
    gated = (u * mixed).astype(BF16)
    o_ref[...] = x + _dot(gated, wout_ref[...])


def _gmlp_layer(h, norm_g, w_in, ln_g, ln_b, w_s, b_s, w_out):
    n, d = h.shape
    gd2 = w_in.shape[1]
    gd = gd2 // 2
    tm = min(512, n)
    full = lambda *shape: pl.BlockSpec(shape, lambda i: (0,) * len(shape))
    return pl.pallas_call(
        _gmlp_body,
        grid=(n // tm,),
        in_specs=[pl.BlockSpec((tm, d), lambda i: (i, 0)),
                  full(1, d), full(d, gd2), full(1, gd), full(1, gd),
                  full(GMLP_GROUPS, GMLP_CHUNK, GMLP_CHUNK), full(GMLP_GROUPS, GMLP_CHUNK, 1),
                  full(gd, d)],
        out_specs=pl.BlockSpec((tm, d), lambda i: (i, 0)),
        out_shape=jax.ShapeDtypeStruct((n, d), F32),
        compiler_params=_cparams(("arbitrary",)),
        name="gmlp_layer",
    )(h, norm_g.reshape(1, d), w_in.astype(BF16), ln_g.reshape(1, gd), ln_b.reshape(1, gd),
      w_s, b_s.reshape(GMLP_GROUPS, GMLP_CHUNK, 1), w_out.astype(BF16))


def _route_body(h_ref, ng_ref, rw_ref, rb_ref, o_ref, cnt_ref, run_ref):
    i = pl.program_id(0)
    tm = h_ref.shape[0]

    @pl.when(i == 0)
    def _():
        run_ref[...] = jnp.zeros_like(run_ref)

    xn = _rms(h_ref[...], ng_ref[...])
    x_hi = xn.astype(BF16)
    x_lo = (xn - x_hi.astype(F32)).astype(BF16)
    logits = _dot(jnp.concatenate([x_hi, x_hi, x_lo], axis=1), rw_ref[...]) + rb_ref[...]
    eid = lax.broadcasted_iota(jnp.int32, logits.shape, 1).astype(F32)
    lane = lax.broadcasted_iota(jnp.int32, (tm, LANES), 1)
    work = logits
    vals, idxs = [], []
    for _ in range(TOP_K):
        mx, ix = _argmax_first(work, eid, N_EXPERTS)
        vals.append(mx)
        idxs.append(ix)
        work = jnp.where(eid == ix, -jnp.inf, work)
    exps = [jnp.exp(v - vals[0]) for v in vals]
    den = exps[0]
    for e in exps[1:]:
        den = den + e
    onehot = jnp.zeros(logits.shape, F32)
    for ix in idxs:
        onehot = onehot + (eid == ix).astype(F32)
    r = lax.broadcasted_iota(jnp.int32, (tm, tm), 0)
    c = lax.broadcasted_iota(jnp.int32, (tm, tm), 1)
    before = (r > c).astype(BF16)
    prefix = _dot(before, onehot.astype(BF16)) + run_ref[...]
    out = jnp.zeros((tm, LANES), F32)
    for k in range(TOP_K):
        rank = jnp.sum(jnp.where(eid == idxs[k], prefix, 0.0), axis=-1, keepdims=True)
        out = jnp.where(lane == k, idxs[k], out)
        out = jnp.where(lane == TOP_K + k, exps[k] / den, out)
        out = jnp.where(lane == 2 * TOP_K + k, rank, out)
    o_ref[...] = out
    run_ref[...] = run_ref[...] + jnp.sum(onehot, axis=0, keepdims=True)
    cnt_ref[...] = run_ref[...]


def _moe_route(h, norm_g, router_w, router_b):
    n, d = h.shape
    tm = min(512, n)
    w_hi = router_w.astype(BF16)
    w_lo = (router_w - w_hi.astype(F32)).astype(BF16)
    full = lambda *shape: pl.BlockSpec(shape, lambda i: (0,) * len(shape))
    return pl.pallas_call(
        _route_body,
        grid=(n // tm,),
        in_specs=[pl.BlockSpec((tm, d), lambda i: (i, 0)), full(1, d), full(3 * d, N_EXPERTS), full(1, N_EXPERTS)],
        out_specs=[pl.BlockSpec((tm, LANES), lambda i: (i, 0)), full(1, N_EXPERTS)],
        out_shape=[jax.ShapeDtypeStruct((n, LANES), F32), jax.ShapeDtypeStruct((1, N_EXPERTS), F32)],
        scratch_shapes=[pltpu.VMEM((1, N_EXPERTS), F32)],
        compiler_params=_cparams(("arbitrary",)),
        name="moe_route",
    )(h, norm_g.reshape(1, d), jnp.concatenate([w_hi, w_lo, w_hi], axis=0), router_b.reshape(1, N_EXPERTS))


def _dispatch_body(pad_ref, dest_ref, h_ref, ng_ref, xs_out, buf, zbuf, sem, zsem):
    i = pl.program_id(0)
    tm = h_ref.shape[0]

    @pl.when(i == 0)
    def _():
        zbuf[...] = jnp.zeros_like(zbuf)
        for e in range(2 * N_EXPERTS):
            first = pl.multiple_of(pad_ref[e], MOE_ROWS)
            zero = pltpu.make_async_copy(zbuf, xs_out.at[pl.ds(first, MOE_ROWS), :], zsem)
            zero.start()
            zero.wait()

    def wait_tile(s):
        for _ in range(TOP_K):
            pltpu.make_async_copy(buf.at[s], xs_out.at[pl.ds(0, tm), :], sem.at[s]).wait()

    def step(s):
        buf[s] = _rms(h_ref[...], ng_ref[...])
        for j in range(tm):
            for k in range(TOP_K):
                r = dest_ref[0, 0, j * TOP_K + k]
                pltpu.make_async_copy(buf.at[s, pl.ds(j, 1), :], xs_out.at[pl.ds(r, 1), :],
                                      sem.at[s]).start(priority=k % 2)

        @pl.when(i >= 1)
        def _():
            wait_tile(1 - s)

        @pl.when(i == pl.num_programs(0) - 1)
        def _():
            wait_tile(s)

    for s in range(2):
        pl.when(i % 2 == s)(functools.partial(step, s))


def _moe_dispatch(h, norm_g, dest, pad_rows, n_rows):
    n, d = h.shape
    tm = min(256, n)
    return pl.pallas_call(
        _dispatch_body,
        grid_spec=pltpu.PrefetchScalarGridSpec(
            num_scalar_prefetch=1, grid=(n // tm,),
            in_specs=[pl.BlockSpec((1, 1, tm * TOP_K), lambda i, pr: (i, 0, 0), memory_space=pltpu.SMEM),
                      pl.BlockSpec((tm, d), lambda i, pr: (i, 0)),
                      pl.BlockSpec((1, d), lambda i, pr: (0, 0))],
            out_specs=pl.BlockSpec(memory_space=pl.ANY),
            scratch_shapes=[pltpu.VMEM((2, tm, d), F32), pltpu.VMEM((MOE_ROWS, d), F32),
                            pltpu.SemaphoreType.DMA((2,)), pltpu.SemaphoreType.DMA(())]),
        out_shape=jax.ShapeDtypeStruct((n_rows, d), F32),
        compiler_params=_cparams(("arbitrary",)),
        name="moe_dispatch",
    )(pad_rows, dest.reshape(n // tm, 1, tm * TOP_K), h, norm_g.reshape(1, d))


def _expert_body(be_ref, nu_ref, xs_ref, wgu_ref, bgu_ref, wd_ref, bd_ref, ys_ref, wgu_bf, wd_bf):
    i = pl.program_id(0)
    ed = wd_ref.shape[2]

    @pl.when((i == 0) | (be_ref[i] != be_ref[jnp.maximum(i - 1, 0)]))
    def _():
        wgu_bf[...] = wgu_ref[0, 0].astype(BF16)
        wd_bf[...] = wd_ref[0, 0].astype(BF16)

    @pl.when(i < nu_ref[0])
    def _():
        x = xs_ref[...].astype(BF16)
        gu = _dot(x, wgu_bf[...]) + bgu_ref[0]
        gate = jnp.minimum(gu[:, :ed], SWIGLU_LIMIT)
        up = jnp.clip(gu[:, ed:], -SWIGLU_LIMIT, SWIGLU_LIMIT)
        glu = gate * jax.nn.sigmoid(gate * SWIGLU_ALPHA)
        ys_ref[...] = _dot(((up + 1.0) * glu).astype(BF16), wd_bf[...]) + bd_ref[0]

    @pl.when(i >= nu_ref[0])
    def _():
        ys_ref[...] = jnp.zeros_like(ys_ref)


def _moe_experts(xs, blk_expert, n_used, w_gu, b_gu, w_d, b_d, layer):
    n_rows, d = xs.shape
    ed = w_d.shape[2]
    n_blk = n_rows // MOE_ROWS
    return pl.pallas_call(
        _expert_body,
        grid_spec=pltpu.PrefetchScalarGridSpec(
            num_scalar_prefetch=2, grid=(n_blk,),
            in_specs=[pl.BlockSpec((MOE_ROWS, d), lambda i, be, nu: (jnp.maximum(jnp.minimum(i, nu[0] - 1), 0), 0)),
                      pl.BlockSpec((1, 1, d, 2 * ed), lambda i, be, nu: (layer, be[i], 0, 0)),
                      pl.BlockSpec((1, 1, 2 * ed), lambda i, be, nu: (be[i], 0, 0)),
                      pl.BlockSpec((1, 1, ed, d), lambda i, be, nu: (layer, be[i], 0, 0)),
                      pl.BlockSpec((1, 1, d), lambda i, be, nu: (be[i], 0, 0))],
            out_specs=pl.BlockSpec((MOE_ROWS, d), lambda i, be, nu: (i, 0)),
            scratch_shapes=[pltpu.VMEM((d, 2 * ed), BF16), pltpu.VMEM((ed, d), BF16)]),
        out_shape=jax.ShapeDtypeStruct((n_rows, d), F32),
        compiler_params=_cparams(("arbitrary",)),
        name="moe_experts",
    )(blk_expert, n_used, xs, w_gu, b_gu.reshape(N_EXPERTS, 1, 2 * ed), w_d, b_d.reshape(N_EXPERTS, 1, d))


def _combine_body(dest_ref, next_ref, rt_ref, h_ref, p_ref, pw_ref, pg_ref, pn_ref, ys_hbm, o_ref, buf, sem):
    i = pl.program_id(0)
    n = pl.num_programs(0)
    tm = h_ref.shape[0]
    slot = i % 2

    def row_copy(idx_ref, j, k, s):
        r = idx_ref[0, 0, j * TOP_K + k]
        return pltpu.make_async_copy(ys_hbm.at[pl.ds(r, 1), :], buf.at[s, k, pl.ds(j, 1), :], sem.at[s])

    def wait_tile(s):
        for k in range(TOP_K):
            pltpu.make_async_copy(ys_hbm.at[pl.ds(0, tm), :], buf.at[s, k], sem.at[s]).wait()

    @pl.when(i == 0)
    def _():
        def first(j, carry):
            for k in range(TOP_K):
                row_copy(dest_ref, j, k, 0).start()
            return carry
        lax.fori_loop(0, tm, first, 0)

    def step(s):
        wait_tile(s)
        for j in range(tm):
            for k in range(TOP_K):
                row_copy(next_ref, j, k, 1 - s).start(priority=k % 2)
        rt = rt_ref[...]
        h = h_ref[...]
        for k in range(TOP_K):
            h = h + rt[:, TOP_K + k:TOP_K + k + 1] * buf[s, k]
        emb = _dot(p_ref[...].astype(BF16), pw_ref[...])
        gate = jax.nn.sigmoid(_dot(_rms(h, pn_ref[...]).astype(BF16), pg_ref[...]))
        o_ref[...] = h + emb * gate

        @pl.when(i == n - 1)
        def _():
            wait_tile(1 - s)

    for s in range(2):
        pl.when(slot == s)(functools.partial(step, s))


def _moe_combine_ple(h, route, dest, ys, p, ple_w, ple_gate_w, ple_norm):
    n, d = h.shape
    pd = p.shape[1]
    tm = min(256, n)
    nt = n // tm
    full = lambda *shape: pl.BlockSpec(shape, lambda i: (0,) * len(shape))
    dest3 = dest.reshape(nt, 1, tm * TOP_K)
    return pl.pallas_call(
        _combine_body,
        grid=(nt,),
        in_specs=[pl.BlockSpec((1, 1, tm * TOP_K), lambda i: (i, 0, 0), memory_space=pltpu.SMEM),
                  pl.BlockSpec((1, 1, tm * TOP_K), lambda i: (jnp.minimum(i + 1, nt - 1), 0, 0),
                               memory_space=pltpu.SMEM),
                  pl.BlockSpec((tm, LANES), lambda i: (i, 0)),
                  pl.BlockSpec((tm, d), lambda i: (i, 0)),
                  pl.BlockSpec((tm, pd), lambda i: (i, 0)),
                  full(pd, d), full(d, d), full(1, d),
                  pl.BlockSpec(memory_space=pl.ANY)],
        out_specs=pl.BlockSpec((tm, d), lambda i: (i, 0)),
        out_shape=jax.ShapeDtypeStruct((n, d), F32),
        scratch_shapes=[pltpu.VMEM((2, TOP_K, tm, d), F32), pltpu.SemaphoreType.DMA((2,))],
        compiler_params=_cparams(("arbitrary",)),
        name="moe_combine_ple",
    )(dest3, dest3, route, h, p, ple_w.astype(BF16), ple_gate_w.astype(BF16), ple_norm.reshape(1, d), ys)


def _moe_ple_layer(h, norm_g, router_w, router_b, w_gu, b_gu, w_d, b_d, layer, p, ple_w, ple_gate_w, ple_norm):
    n, _ = h.shape
    route, counts = _moe_route(h, norm_g, router_w, router_b)
    counts = counts[0].astype(jnp.int32)
    pad_counts = (counts + MOE_ROWS - 1) // MOE_ROWS * MOE_ROWS
    pad_ends = jnp.cumsum(pad_counts)
    pad_starts = pad_ends - pad_counts
    top_idx = route[:, :TOP_K].astype(jnp.int32)
    rank = route[:, 2 * TOP_K:3 * TOP_K].astype(jnp.int32)
    dest = (pad_starts[top_idx] + rank).reshape(-1)
    n_blk = -(-(n * TOP_K) // MOE_ROWS) + N_EXPERTS
    blk_start = jnp.arange(n_blk, dtype=jnp.int32) * MOE_ROWS
    blk_expert = jnp.minimum(jnp.sum((pad_ends[None, :] <= blk_start[:, None]).astype(jnp.int32), axis=1),
                             N_EXPERTS - 1)
    n_used = (pad_ends[-1:] // MOE_ROWS).astype(jnp.int32)
    tail = jnp.minimum(pad_ends[-1] + jnp.arange(N_EXPERTS, dtype=jnp.int32) * MOE_ROWS, (n_blk - 1) * MOE_ROWS)
    clear = jnp.concatenate([jnp.maximum(pad_ends - MOE_ROWS, 0), tail]).astype(jnp.int32)
    xs = _moe_dispatch(h, norm_g, dest, clear, n_blk * MOE_ROWS)
    ys = _moe_experts(xs, blk_expert, n_used, w_gu, b_gu, w_d, b_d, layer)
    return _moe_combine_ple(h, route, dest, ys, p, ple_w, ple_gate_w, ple_norm)


def _kv_body(h_ref, ng_ref, w_ref, seg_ref, kn_ref, kc_ref, vc_ref, ks_ref, vs_ref, kw_ref, vw_ref):
    ts = h_ref.shape[1]
    gw = N_KV_GROUPS * HEAD_DIM
    step = pl.program_id(1)
    last = pl.num_programs(1) - 1
    st = jnp.clip(step - 1, 0, last - 2)
    is_pad = (step == 0) | (step == last)
    hn = _rms(h_ref[0], ng_ref[...]).astype(BF16)
    kv = _dot(hn, w_ref[...])

    def knorm(x, j):
        ms = _dot_split(x * x, seg_ref[...])
        return x * lax.rsqrt(ms + NORM_EPS) * kn_ref[j]

    k_c, v_c = kv[:, 0:gw], kv[:, gw:2 * gw]
    k_s, v_s = knorm(kv[:, 2 * gw:3 * gw], 1), kv[:, 3 * gw:4 * gw]
    k_w, v_w = knorm(kv[:, 4 * gw:5 * gw], 2), kv[:, 5 * gw:6 * gw]
    tok = st * ts + lax.broadcasted_iota(jnp.int32, (ts, LANES), 0)
    blk = lax.broadcasted_iota(jnp.int32, (ts, LANES), 1)
    onehot = (tok // SEL_BLOCK == blk).astype(BF16)
    one_col = (blk == HEAD_DIM).astype(F32)
    for g in range(N_KV_GROUPS):
        kc_ref[0, g] = _low_lanes(_head_lanes(k_c, g), 0.0)
        vc_ref[0, g] = _low_lanes(_head_lanes(v_c, g), 0.0)
        ks_ref[0, g] = jnp.concatenate([_low_lanes(_head_lanes(k_s, g), 0.0).astype(BF16), onehot], axis=1)
        vs_ref[0, g] = _low_lanes(_head_lanes(v_s, g), one_col).astype(BF16)
        kw_ref[0, g] = jnp.where(is_pad, one_col, _low_lanes(_head_lanes(k_w, g), 0.0)).astype(BF16)
        vw_ref[0, g] = jnp.where(is_pad, 0.0, _low_lanes(_head_lanes(v_w, g), one_col)).astype(BF16)


def _kv_project(h3, kv_norm, kv_w, k_norm):
    b, s, d = h3.shape
    gw = N_KV_GROUPS * HEAD_DIM
    ts = min(512, s)
    seg = jnp.asarray(np.tile(np.kron(np.eye(N_KV_GROUPS), np.full((HEAD_DIM, HEAD_DIM), 1.0 / HEAD_DIM)),
                              (2, 1)), F32).astype(BF16)
    kn = jnp.tile(k_norm, (1, N_KV_GROUPS)).reshape(N_BRANCH, 1, gw)
    assert ts == WINDOW, "the window keys' front padding is one sequence tile"
    nt = s // ts
    full = lambda *shape: pl.BlockSpec(shape, lambda bi, si: (0,) * len(shape))
    tile = lambda si: jnp.clip(si - 1, 0, nt - 1)
    hd = lambda w: pl.BlockSpec((1, N_KV_GROUPS, ts, w), lambda bi, si: (bi, 0, tile(si), 0))
    padded = pl.BlockSpec((1, N_KV_GROUPS, ts, LANES), lambda bi, si: (bi, 0, si, 0))
    sds = lambda w, dt, rows=s: jax.ShapeDtypeStruct((b, N_KV_GROUPS, rows, w), dt)
    return pl.pallas_call(
        _kv_body,
        grid=(b, nt + 2),
        in_specs=[pl.BlockSpec((1, ts, d), lambda bi, si: (bi, tile(si), 0)), full(1, d),
                  full(d, N_KV_SLOTS * gw), full(2 * gw, gw), full(N_BRANCH, 1, gw)],
        out_specs=[hd(LANES), hd(LANES), hd(2 * LANES), hd(LANES), padded, padded],
        out_shape=[sds(LANES, F32), sds(LANES, F32), sds(2 * LANES, BF16), sds(LANES, BF16),
                   sds(LANES, BF16, s + 2 * ts), sds(LANES, BF16, s + 2 * ts)],
        compiler_params=_cparams(("arbitrary", "arbitrary")),
        name="kv_project",
    )(h3, kv_norm.reshape(1, d), kv_w.astype(BF16), seg, kn)


def _compress_body(kc_ref, vc_ref, pos_ref, w1_ref, b1_ref, w2_ref, kn_ref, ko_ref, vo_ref):
    nc = kc_ref.shape[2] // CMP_STRIDE
    row = lax.broadcasted_iota(jnp.int32, (nc, 1), 0)
    valid = row < nc - 1
    lane = lax.broadcasted_iota(jnp.int32, (nc, LANES), 1)

    def chunks(ref):
        parts = []
        for l in range(0, CMP_STRIDE, 2):
            even = ref[0, 0, pl.ds(l, nc, stride=CMP_STRIDE), :]
            odd = ref[0, 0, pl.ds(l + 1, nc, stride=CMP_STRIDE), :]
            parts.append(jnp.where(lane < HEAD_DIM, even, pltpu.roll(odd, HEAD_DIM, 1)))
        return jnp.concatenate(parts, axis=1)

    def compress(c, j):
        a = _dot((c + pos_ref[j, 0]).astype(BF16), w1_ref[j, 0])
        bm = _dot((c + pos_ref[j, 1]).astype(BF16), w1_ref[j, 1])
        nxt = pltpu.roll(bm, nc - 1, 0)
        hid = jax.nn.gelu(a + nxt + b1_ref[j])
        return _dot(hid.astype(BF16), w2_ref[j])

    kraw = compress(chunks(kc_ref), 0)
    ms = jnp.sum(kraw * kraw, axis=-1, keepdims=True) * (1.0 / HEAD_DIM)
    kcmp = kraw * lax.rsqrt(ms + NORM_EPS) * kn_ref[...]
    vcmp = jnp.where(valid, compress(chunks(vc_ref), 1), 0.0)
    flag = (lane == HEAD_DIM).astype(F32)
    ko_ref[0, 0, 0:nc, :] = flag
    ko_ref[0, 0, nc:2 * nc, :] = jnp.where(valid, kcmp, flag)
    vo_ref[0, 0, 0:nc, :] = flag
    vo_ref[0, 0, nc:2 * nc, :] = jnp.where(lane == HEAD_DIM, 1.0, vcmp)


def _compress(kc, vc, cmp_pos, cmp_w1, cmp_b1, cmp_w2, k_norm0):
    b, g, s, _ = kc.shape
    dh = HEAD_DIM
    nc = s // CMP_STRIDE
    half = CMP_STRIDE * dh
    pos = cmp_pos.reshape(2, 2, 1, half)
    w1 = cmp_w1.reshape(2, 2, half, CMP_HIDDEN).astype(BF16)
    w2 = jnp.pad(cmp_w2, ((0, 0), (0, 0), (0, LANES - dh))).astype(BF16)
    kn = jnp.pad(k_norm0, (0, LANES - dh)).reshape(1, LANES)
    full = lambda *shape: pl.BlockSpec(shape, lambda bi, gi: (0,) * len(shape))
    blk = pl.BlockSpec((1, 1, s, LANES), lambda bi, gi: (bi, gi, 0, 0))
    out = pl.BlockSpec((1, 1, 2 * nc, LANES), lambda bi, gi: (bi, gi, 0, 0))
    return pl.pallas_call(
        _compress_body,
        grid=(b, g),
        in_specs=[blk, blk, full(2, 2, 1, half), full(2, 2, half, CMP_HIDDEN), full(2, 1, CMP_HIDDEN),
                  full(2, CMP_HIDDEN, LANES), full(1, LANES)],
        out_specs=[out, out],
        out_shape=[jax.ShapeDtypeStruct((b, g, 2 * nc, LANES), F32)] * 2,
        compiler_params=_cparams(("arbitrary", "arbitrary")),
        name="kv_compress",
    )(kc, vc, pos, w1, cmp_b1.reshape(2, 1, CMP_HIDDEN), w2, kn)


def _qproj_body(h_ref, ng_ref, w_ref, bg_ref, ind_ref, indt_ref, qn_ref, q_ref, gate_ref):
    hd = N_HEADS * HEAD_DIM
    xn = _rms(h_ref[0], ng_ref[...]).astype(BF16)
    proj = _dot(xn, w_ref[...])
    q = proj[:, :hd]
    ms = _dot_split(q * q, ind_ref[...]) * (1.0 / HEAD_DIM)
    scale = _dot_split(lax.rsqrt(ms + NORM_EPS), indt_ref[...])
    qn = q * scale * qn_ref[...] * (HEAD_DIM ** -0.5 * LOG2E)
    lane = lax.broadcasted_iota(jnp.int32, (q.shape[0], LANES), 1)
    fill = jnp.where(lane == HEAD_DIM, NEG_INF, 0.0)
    for h in range(N_HEADS):
        q_ref[0, h] = _low_lanes(_head_lanes(qn, h), fill).astype(BF16)
    gate_ref[0] = jax.nn.sigmoid(proj[:, hd:] + bg_ref[...])


def _q_project(h3, norm_g, w_in, b_gate, q_norm):
    b, s, d = h3.shape
    hd = N_HEADS * HEAD_DIM
    ng = N_BRANCH * N_HEADS
    ts = min(512, s)
    w = jnp.pad(w_in, ((0, 0), (0, LANES - ng))).astype(BF16)
    bg = jnp.pad(b_gate, (0, LANES - ng)).reshape(1, LANES)
    ind = np.zeros((hd, LANES), np.float32)
    ind[np.arange(hd), np.arange(hd) // HEAD_DIM] = 1.0
    full = lambda *shape: pl.BlockSpec(shape, lambda bi, si: (0,) * len(shape))
    return pl.pallas_call(
        _qproj_body,
        grid=(b, s // ts),
        in_specs=[pl.BlockSpec((1, ts, d), lambda bi, si: (bi, si, 0)), full(1, d), full(d, hd + LANES),
                  full(1, LANES), full(2 * hd, LANES), full(2 * LANES, hd), full(1, hd)],
        out_specs=[pl.BlockSpec((1, N_HEADS, ts, LANES), lambda bi, si: (bi, 0, si, 0)),
                   pl.BlockSpec((1, ts, LANES), lambda bi, si: (bi, si, 0))],
        out_shape=[jax.ShapeDtypeStruct((b, N_HEADS, s, LANES), BF16),
                   jax.ShapeDtypeStruct((b, s, LANES), F32)],
        compiler_params=_cparams(("arbitrary", "arbitrary")),
        name="nsa_qproj",
    )(h3, norm_g.reshape(1, d), w, bg, jnp.asarray(np.tile(ind, (2, 1))).astype(BF16),
      jnp.asarray(np.tile(ind.T, (2, 1))).astype(BF16), jnp.tile(q_norm, N_HEADS).reshape(1, hd))


def _t5_bucket_np(dist):
    n = np.maximum(dist, 0)
    max_exact = REL_BUCKETS // 2
    nf = np.maximum(n, 1).astype(np.float64)
    large = max_exact + (np.log(nf / max_exact) / math.log(REL_MAX_DIST / max_exact)
                         * (REL_BUCKETS - max_exact)).astype(np.int64)
    return np.where(n < max_exact, n, np.minimum(large, REL_BUCKETS - 1))


def _bias_table(rel_bias, dist, valid):
    r = N_HEADS // N_KV_GROUPS
    tab = rel_bias.astype(F32).T.reshape(N_KV_GROUPS, r, REL_BUCKETS)
    onehot = jnp.asarray(_t5_bucket_np(dist)[..., None] == np.arange(REL_BUCKETS), F32)
    bias = jnp.einsum('xqln,grn->gxrql', onehot, tab, precision=lax.Precision.HIGHEST) * LOG2E
    return jnp.where(jnp.asarray(valid)[None, :, None, :, :], bias, NEG_INF)


def _n_delta(seq):
    d = np.arange(seq + SEL_BLOCK)
    bk = _t5_bucket_np(d)
    change = np.nonzero(bk[1:] != bk[:-1])[0]
    d_const = int(change[-1]) + 1 if change.size else 0
    return -(-(d_const + SEL_BLOCK - 1) // SEL_BLOCK) + 1


def _attn_tables(rel_bias, seq):
    r = N_HEADS // N_KV_GROUPS
    qi = np.arange(Q_BLOCK)[:, None]
    rows = lambda t: jnp.transpose(t, (0, 2, 1, 3, 4)).reshape(N_KV_GROUPS, r * Q_PAIR * Q_BLOCK, t.shape[-1])
    nc = seq // CMP_STRIDE
    j = np.arange(nc)[None, :]
    dist_c = np.stack([qi - (CMP_BLOCK - 1) - Q_BLOCK * (Q_PAIR - u) + CMP_STRIDE * (nc - j)
                       for u in range(Q_PAIR)])
    rc = rows(_bias_table(rel_bias, dist_c, dist_c >= 0))
    nd = _n_delta(seq)
    delta = np.arange(-1, nd + 1)[:, None, None]
    kj = np.arange(2 * SEL_BLOCK)[None, None, :]
    dist_s = SEL_BLOCK * (delta - kj // SEL_BLOCK) + qi[None] - kj % SEL_BLOCK
    bt = _bias_table(rel_bias, dist_s, dist_s >= 0)
    jw = np.arange(WINDOW + Q_PAIR * Q_BLOCK)[None, :]
    dist_w = np.stack([Q_BLOCK * u + qi - jw + WINDOW for u in range(Q_PAIR)])
    wb = rows(_bias_table(rel_bias, dist_w, (dist_w >= 0) & (dist_w < WINDOW)))
    n_sel = seq // SEL_BLOCK
    cs = np.arange(nc) * CMP_STRIDE
    ss = np.arange(n_sel) * SEL_BLOCK
    ov = np.clip(np.minimum(cs[:, None] + CMP_BLOCK, ss[None, :] + SEL_BLOCK)
                 - np.maximum(cs[:, None], ss[None, :]), 0, None) / CMP_BLOCK
    return rc, bt, wb, jnp.asarray(ov.T.astype(np.float32)).astype(BF16)


def _attn_body(q_ref, gate_ref, kc_ref, vc_ref, ks_ref, vs_ref, kw_ref, vw_ref, rc_ref, bt_ref, wb_ref, ovt_ref,
               o_ref, sa_ref, sb_ref):
    g = pl.program_id(1)
    i0 = Q_PAIR * pl.program_id(2)
    r = q_ref.shape[1]
    pq = Q_PAIR * Q_BLOCK
    rq = r * pq
    nc = rc_ref.shape[2]
    n_sel = ovt_ref.shape[0]
    nd = bt_ref.shape[1] - 2
    wl = wb_ref.shape[2]
    per = SEL_BLOCK // CMP_STRIDE

    heads = [slice(h * pq, (h + 1) * pq) for h in range(r)]
    q_pad = q_ref[0].reshape(rq, LANES)

    end = pl.multiple_of(per * (i0 + Q_PAIR), Q_PAIR * per)
    kcw = kc_ref[0, 0, pl.ds(end, nc), :].astype(BF16)
    vcw = vc_ref[0, 0, pl.ds(end, nc), :].astype(BF16)
    qpos = i0 * Q_BLOCK + lax.broadcasted_iota(jnp.int32, (rq, 1), 0) % pq
    sc = _dot_nt(q_pad, kcw) + rc_ref[0]
    p_c = jnp.exp2((sc - jnp.max(sc, axis=-1, keepdims=True)).astype(BF16))
    o_c = _normalize(_dot(p_c, vcw)) * (qpos >= CMP_BLOCK - 1).astype(F32)

    imp_r = _dot_nt(ovt_ref[...], p_c)
    inv_l = 1.0 / _dot_nt(jnp.ones((8, nc), BF16), p_c)[0:1]
    imp = imp_r[:, heads[0]] * inv_l[:, heads[0]]
    for h in range(1, r):
        imp = imp + imp_r[:, heads[h]] * inv_l[:, heads[h]]
    tpos = i0 * Q_BLOCK + lax.broadcasted_iota(jnp.int32, (1, pq), 1)
    imp = imp * (tpos >= CMP_BLOCK - 1).astype(F32)

    anchor = (jnp.max(imp, axis=(0, 1), keepdims=True) * 0.0).astype(BF16)
    ws = pl.multiple_of(i0 * Q_BLOCK, pq)
    sw = _dot_nt(q_pad + anchor, kw_ref[0, 0, pl.ds(ws, wl), :]) + wb_ref[0]
    p_w = jnp.exp2((sw - jnp.max(sw, axis=-1, keepdims=True)).astype(BF16))
    o_w = _normalize(_dot(p_w, vw_ref[0, 0, pl.ds(ws, wl), :]))

    shift = i0 + Q_PAIR
    blk_rel = lax.broadcasted_iota(jnp.int32, (n_sel, pq), 0)
    blk = blk_rel + shift - n_sel
    cur = i0 + lax.broadcasted_iota(jnp.int32, (n_sel, pq), 1) // Q_BLOCK
    forced = (blk == 0) | (blk == cur) | (blk == cur - 1)
    imp = jnp.where(forced, FORCE, jnp.where(blk > cur, NEG_INF, imp))
    imp = jnp.where(blk < 0, -jnp.inf, imp)
    ids = blk_rel.astype(F32)
    sel = jnp.zeros((n_sel, pq), F32)
    for _ in range(min(SEL_TOP, n_sel)):
        mx = jnp.max(imp, axis=0, keepdims=True)
        ix = jnp.min(jnp.where(imp == mx, ids, float(n_sel)), axis=0, keepdims=True)
        hit = ids == ix
        sel = jnp.where(hit, 1.0, sel)
        imp = jnp.where(hit, -jnp.inf, imp)
    unsel = jnp.where((sel > 0.0) & (blk >= 0), 0.0, NEG_INF)
    unsel = pltpu.roll(unsel.T, shift % n_sel, 1).astype(BF16)
    if n_sel < LANES:
        unsel = jnp.concatenate([unsel, jnp.zeros((pq, LANES - n_sel), BF16)], axis=1)

    q_aug = jnp.concatenate([q_pad, jnp.concatenate([unsel] * r, axis=0)], axis=1)
    kchunk = sa_ref.shape[1]
    n_chunks = ks_ref.shape[2] // kchunk
    cblocks = kchunk // SEL_BLOCK
    pairs = cblocks // 2

    def scores_to(dst, c):
        start = pl.multiple_of(jnp.minimum(c, n_chunks - 1) * kchunk, kchunk)
        sc = _dot_nt(q_aug, ks_ref[0, 0, pl.ds(start, kchunk), :])
        d0 = i0 - c * cblocks
        tiles = [[jnp.clip(d0 + u - 2 * pm, -1, nd) + 1 for pm in range(pairs)] for u in range(Q_PAIR)]
        dst[...] = sc + jnp.concatenate(
            [jnp.concatenate([bt_ref[0, t, h] for t in tiles[u]], axis=1)
             for h in range(r) for u in range(Q_PAIR)], axis=0)

    def absorb(src, c, m, acc):
        start = pl.multiple_of(c * kchunk, kchunk)
        sc = src[...]
        m_new = jnp.maximum(m, jnp.max(sc, axis=-1, keepdims=True))
        p = jnp.exp2((sc - m_new).astype(BF16))
        acc = jnp.exp2(m - m_new) * acc + _dot(p, vs_ref[0, 0, pl.ds(start, kchunk), :])
        return m_new, acc

    def two_chunks(j, carry):
        scores_to(sb_ref, 2 * j + 1)
        carry = absorb(sa_ref, 2 * j, *carry)
        scores_to(sa_ref, 2 * j + 2)
        return absorb(sb_ref, 2 * j + 1, *carry)

    scores_to(sa_ref, 0)
    init = (jnp.full((rq, 1), NEG_INF, F32), jnp.zeros((rq, LANES), F32))
    rem = (i0 + Q_PAIR) % (2 * cblocks)
    trips = (i0 + Q_PAIR) // (2 * cblocks) + (rem > cblocks).astype(jnp.int32)
    state = lax.fori_loop(0, trips, two_chunks, init)
    _, acc_s = lax.cond((rem > 0) & (rem <= cblocks),
                        lambda m, acc: absorb(sa_ref, 2 * trips, m, acc), lambda m, acc: (m, acc), *state)
    o_s = _normalize(acc_s)

    gates = gate_ref[0]
    glane = lax.broadcasted_iota(jnp.int32, gates.shape, 1)
    outs = []
    for h in range(r):
        head = g * r + h
        gs = [jnp.sum(jnp.where(glane == br * N_HEADS + head, gates, 0.0), axis=-1, keepdims=True)
              for br in range(N_BRANCH)]
        outs.append(gs[0] * o_c[heads[h]] + gs[1] * o_s[heads[h]] + gs[2] * o_w[heads[h]])
    o_ref[0] = jnp.concatenate(
        [outs[h] + pltpu.roll(outs[h + 1], HEAD_DIM, 1) for h in range(0, r, 2)], axis=1)


def _nsa_attention(q, gates, kc_pad, vc_pad, ks_aug, vs, kw_pad, vw_pad, tables):
    b, _, s, _ = q.shape
    r = N_HEADS // N_KV_GROUPS
    rc, bt, wb, ov = tables
    n_qb = s // Q_BLOCK
    per_bg = lambda a: pl.BlockSpec((1, 1) + a.shape[2:], lambda bi, gi, qi: (bi, gi, 0, 0))
    per_g = lambda a: pl.BlockSpec((1,) + a.shape[1:], lambda bi, gi, qi: (gi,) + (0,) * (a.ndim - 1))
    pq = Q_PAIR * Q_BLOCK
    return pl.pallas_call(
        _attn_body,
        grid=(b, N_KV_GROUPS, n_qb // Q_PAIR),
        in_specs=[pl.BlockSpec((1, r, pq, LANES), lambda bi, gi, qi: (bi, gi, qi, 0)),
                  pl.BlockSpec((1, pq, LANES), lambda bi, gi, qi: (bi, qi, 0)),
                  per_bg(kc_pad), per_bg(vc_pad), per_bg(ks_aug), per_bg(vs), per_bg(kw_pad), per_bg(vw_pad),
                  per_g(rc), per_g(bt), per_g(wb),
                  pl.BlockSpec(ov.shape, lambda bi, gi, qi: (0, 0))],
        out_specs=pl.BlockSpec((1, pq, r * HEAD_DIM), lambda bi, gi, qi: (bi, qi, gi)),
        out_shape=jax.ShapeDtypeStruct((b, s, N_HEADS * HEAD_DIM), F32),
        scratch_shapes=[pltpu.VMEM((r * pq, min(KEY_CHUNK, s // 2)), F32)] * 2,
        compiler_params=_cparams(("arbitrary", "arbitrary", "arbitrary")),
        name="nsa_attention",
    )(q, gates, kc_pad, vc_pad, ks_aug, vs, kw_pad, vw_pad, rc, bt, wb, ov)


def _outproj_body(a_ref, h_ref, w_ref, o_ref):
    o_ref[...] = h_ref[...] + _dot(a_ref[...].astype(BF16), w_ref[...])


def _out_project(attn, h, w_out):
    n, d = h.shape
    hd = attn.shape[1]
    tm = min(512, n)
    return pl.pallas_call(
        _outproj_body,
        grid=(n // tm,),
        in_specs=[pl.BlockSpec((tm, hd), lambda i: (i, 0)), pl.BlockSpec((tm, d), lambda i: (i, 0)),
                  pl.BlockSpec((hd, d), lambda i: (0, 0))],
        out_specs=pl.BlockSpec((tm, d), lambda i: (i, 0)),
        out_shape=jax.ShapeDtypeStruct((n, d), F32),
        compiler_params=_cparams(("arbitrary",)),
        name="nsa_outproj",
    )(attn, h, w_out.astype(BF16))


def kernel(x, p, rel_bias, norm_mix, norm_ffn, a_w_in, a_ln_g, a_ln_b, a_w_s, a_b_s, a_w_out, kv_norm, kv_w, cmp_pos, cmp_w1, cmp_b1, cmp_w2, k_norm, b_w_in, b_b_gate, q_norm, b_w_out, router_w, router_b, e_w_gu, e_b_gu, e_w_d, e_b_d, ple_w, ple_gate_w, ple_norm):
    b, s, d = x.shape
    n = b * s
    pf = p.reshape(p.shape[0], n, p.shape[-1])

    def moe_ple(h, i):
        return _moe_ple_layer(h, norm_ffn[i], router_w[i], router_b[i], e_w_gu, e_b_gu[i], e_w_d, e_b_d[i], i,
                              pf[i], ple_w[i], ple_gate_w[i], ple_norm[i])

    h = _gmlp_layer(x.reshape(n, d), norm_mix[0], a_w_in[0], a_ln_g[0], a_ln_b[0], a_w_s[0], a_b_s[0], a_w_out[0])
    h = moe_ple(h, 0)

    h3 = h.reshape(b, s, d)
    kc, vc, ks_aug, vs, kw_pad, vw_pad = _kv_project(h3, kv_norm, kv_w, k_norm)
    kc_pad, vc_pad = _compress(kc, vc, cmp_pos, cmp_w1, cmp_b1, cmp_w2, k_norm[0])

    q, gates = _q_project(h3, norm_mix[1], b_w_in[0], b_b_gate[0], q_norm[0])
    attn = _nsa_attention(q, gates, kc_pad, vc_pad, ks_aug, vs, kw_pad, vw_pad, _attn_tables(rel_bias, s))
    h = _out_project(attn.reshape(n, -1), h, b_w_out[0])
    h = moe_ple(h, 1)
    return h.reshape(b, s, d)
```

```python
import functools
import math

import numpy as np
import jax
import jax.numpy as jnp
from jax import lax
from jax.experimental import pallas as pl
from jax.experimental.pallas import tpu as pltpu

GMLP_CHUNK = 128
GMLP_GROUPS = 8
N_HEADS = 16
HEAD_DIM = 64
N_KV_GROUPS = 4
N_BRANCH = 3
CMP_BLOCK = 32
CMP_STRIDE = 16
CMP_HIDDEN = 256
SEL_BLOCK = 64
SEL_TOP = 16
WINDOW = 512
Q_BLOCK = 64
N_KV_SLOTS = 6
REL_BUCKETS = 32
REL_MAX_DIST = 2048
N_EXPERTS = 32
TOP_K = 4
SWIGLU_LIMIT = 7.0
SWIGLU_ALPHA = 1.702
NORM_EPS = 1e-6
NEG_INF = -1e30
FORCE = 1e30
LOG2E = math.log2(math.e)

LANES = 128
MOE_ROWS = 512
KEY_CHUNK = 512
Q_PAIR = 4
VMEM_LIMIT = 56 * 1024 * 1024

F32 = jnp.float32
BF16 = jnp.bfloat16


def _cparams(sem):
    return pltpu.CompilerParams(dimension_semantics=sem, vmem_limit_bytes=VMEM_LIMIT)


def _rms(x, g):
    return x * lax.rsqrt(jnp.mean(x * x, axis=-1, keepdims=True) + NORM_EPS) * g


def _dot(a, b):
    return jnp.dot(a, b, preferred_element_type=F32)


def _dot_nt(a, b):
    return lax.dot_general(a, b, (((1,), (1,)), ((), ())), preferred_element_type=F32)


def _dot_split(a, b2):
    hi = a.astype(BF16)
    lo = (a - hi.astype(F32)).astype(BF16)
    return _dot(jnp.concatenate([hi, lo], axis=1), b2)


def _normalize(acc):
    lane = lax.broadcasted_iota(jnp.int32, acc.shape, 1)
    denom = jnp.sum(jnp.where(lane == HEAD_DIM, acc, 0.0), axis=-1, keepdims=True)
    return jnp.where(lane < HEAD_DIM, acc / denom, 0.0)


def _head_lanes(x, idx):
    base = idx * HEAD_DIM // LANES * LANES
    y = x[:, base:base + LANES]
    return y if idx * HEAD_DIM == base else pltpu.roll(y, LANES - HEAD_DIM, 1)


def _low_lanes(x, fill):
    lane = lax.broadcasted_iota(jnp.int32, x.shape, 1)
    return jnp.where(lane < HEAD_DIM, x, fill)


def _argmax_first(x, ids, n):
    mx = jnp.max(x, axis=-1, keepdims=True)
    return mx, jnp.min(jnp.where(x == mx, ids, float(n)), axis=-1, keepdims=True)


def _gmlp_body(x_ref, nm_ref, win_ref, lng_ref, lnb_ref, ws_ref, bs_ref, wout_ref, o_ref):
    tm = x_ref.shape[0]
    gd = win_ref.shape[1] // 2
    gdim = gd // GMLP_GROUPS
    x = x_ref[...]
    xn = _rms(x, nm_ref[...]).astype(BF16)
    z = jax.nn.gelu(_dot(xn, win_ref[...]))
    u = z[:, :gd]
    v = z[:, gd:]
    mu = jnp.mean(v, axis=-1, keepdims=True)
    vc = v - mu
    vln = vc * lax.rsqrt(jnp.mean(vc * vc, axis=-1, keepdims=True) + NORM_EPS) * lng_ref[...] + lnb_ref[...]
    vb = vln.astype(BF16)
    row = lax.broadcasted_iota(jnp.int32, (GMLP_CHUNK, GMLP_CHUNK), 0)
    col = lax.broadcasted_iota(jnp.int32, (GMLP_CHUNK, GMLP_CHUNK), 1)
    causal = row >= col
    chunks = []
    for c in range(tm // GMLP_CHUNK):
        cols = []
        for g in range(GMLP_GROUPS):
            wsg = jnp.where(causal, ws_ref[g], 0.0).astype(BF16)
            vg = vb[c * GMLP_CHUNK:(c + 1) * GMLP_CHUNK, g * gdim:(g + 1) * gdim]
            cols.append(_dot(wsg, vg) + bs_ref[g])
        chunks.append(jnp.concatenate(cols, axis=1))
    mixed = jnp.concatenate(chunks, axis=0)
    gated = (u * mixed).astype(BF16)
    o_ref[...] = x + _dot(gated, wout_ref[...])


def _gmlp_layer(h, norm_g, w_in, ln_g, ln_b, w_s, b_s, w_out):
    n, d = h.shape
    gd2 = w_in.shape[1]
    gd = gd2 // 2
    tm = min(512, n)
    full = lambda *shape: pl.BlockSpec(shape, lambda i: (0,) * len(shape))
    return pl.pallas_call(
        _gmlp_body,
        grid=(n // tm,),
        in_specs=[pl.BlockSpec((tm, d), lambda i: (i, 0)),
                  full(1, d), full(d, gd2), full(1, gd), full(1, gd),
                  full(GMLP_GROUPS, GMLP_CHUNK, GMLP_CHUNK), full(GMLP_GROUPS, GMLP_CHUNK, 1),
                  full(gd, d)],
        out_specs=pl.BlockSpec((tm, d), lambda i: (i, 0)),
        out_shape=jax.ShapeDtypeStruct((n, d), F32),
        compiler_params=_cparams(("arbitrary",)),
        name="gmlp_layer",
    )(h, norm_g.reshape(1, d), w_in.astype(BF16), ln_g.reshape(1, gd), ln_b.reshape(1, gd),
      w_s, b_s.reshape(GMLP_GROUPS, GMLP_CHUNK, 1), w_out.astype(BF16))


def _route_body(h_ref, ng_ref, rw_ref, rb_ref, o_ref, cnt_ref, run_ref):
    i = pl.program_id(0)
    tm = h_ref.shape[0]

    @pl.when(i == 0)
    def _():
        run_ref[...] = jnp.zeros_like(run_ref)

    xn = _rms(h_ref[...], ng_ref[...])
    x_hi = xn.astype(BF16)
    x_lo = (xn - x_hi.astype(F32)).astype(BF16)
    logits = _dot(jnp.concatenate([x_hi, x_hi, x_lo], axis=1), rw_ref[...]) + rb_ref[...]
    eid = lax.broadcasted_iota(jnp.int32, logits.shape, 1).astype(F32)
    lane = lax.broadcasted_iota(jnp.int32, (tm, LANES), 1)
    work = logits
    vals, idxs = [], []
    for _ in range(TOP_K):
        mx, ix = _argmax_first(work, eid, N_EXPERTS)
        vals.append(mx)
        idxs.append(ix)
        work = jnp.where(eid == ix, -jnp.inf, work)
    exps = [jnp.exp(v - vals[0]) for v in vals]
    den = exps[0]
    for e in exps[1:]:
        den = den + e
    onehot = jnp.zeros(logits.shape, F32)
    for ix in idxs:
        onehot = onehot + (eid == ix).astype(F32)
    r = lax.broadcasted_iota(jnp.int32, (tm, tm), 0)
    c = lax.broadcasted_iota(jnp.int32, (tm, tm), 1)
    before = (r > c).astype(BF16)
    prefix = _dot(before, onehot.astype(BF16)) + run_ref[...]
    out = jnp.zeros((tm, LANES), F32)
    for k in range(TOP_K):
        rank = jnp.sum(jnp.where(eid == idxs[k], prefix, 0.0), axis=-1, keepdims=True)
        out = jnp.where(lane == k, idxs[k], out)
        out = jnp.where(lane == TOP_K + k, exps[k] / den, out)
        out = jnp.where(lane == 2 * TOP_K + k, rank, out)
    o_ref[...] = out
    run_ref[...] = run_ref[...] + jnp.sum(onehot, axis=0, keepdims=True)
    cnt_ref[...] = run_ref[...]


def _moe_route(h, norm_g, router_w, router_b):
    n, d = h.shape
    tm = min(512, n)
    w_hi = router_w.astype(BF16)
    w_lo = (router_w - w_hi.astype(F32)).astype(BF16)
    full = lambda *shape: pl.BlockSpec(shape, lambda i: (0,) * len(shape))
    return pl.pallas_call(
        _route_body,
        grid=(n // tm,),
        in_specs=[pl.BlockSpec((tm, d), lambda i: (i, 0)), full(1, d), full(3 * d, N_EXPERTS), full(1, N_EXPERTS)],
        out_specs=[pl.BlockSpec((tm, LANES), lambda i: (i, 0)), full(1, N_EXPERTS)],
        out_shape=[jax.ShapeDtypeStruct((n, LANES), F32), jax.ShapeDtypeStruct((1, N_EXPERTS), F32)],
        scratch_shapes=[pltpu.VMEM((1, N_EXPERTS), F32)],
        compiler_params=_cparams(("arbitrary",)),
        name="moe_route",
    )(h, norm_g.reshape(1, d), jnp.concatenate([w_hi, w_lo, w_hi], axis=0), router_b.reshape(1, N_EXPERTS))


def _dispatch_body(pad_ref, dest_ref, h_ref, ng_ref, xs_out, buf, zbuf, sem, zsem):
    i = pl.program_id(0)
    tm = h_ref.shape[0]

    @pl.when(i == 0)
    def _():
        zbuf[...] = jnp.zeros_like(zbuf)
        for e in range(2 * N_EXPERTS):
            first = pl.multiple_of(pad_ref[e], MOE_ROWS)
            zero = pltpu.make_async_copy(zbuf, xs_out.at[pl.ds(first, MOE_ROWS), :], zsem)
            zero.start()
            zero.wait()

    def wait_tile(s):
        for _ in range(TOP_K):
            pltpu.make_async_copy(buf.at[s], xs_out.at[pl.ds(0, tm), :], sem.at[s]).wait()

    def step(s):
        buf[s] = _rms(h_ref[...], ng_ref[...])
        for j in range(tm):
            for k in range(TOP_K):
                r = dest_ref[0, 0, j * TOP_K + k]
                pltpu.make_async_copy(buf.at[s, pl.ds(j, 1), :], xs_out.at[pl.ds(r, 1), :],
                                      sem.at[s]).start(priority=k % 2)

        @pl.when(i >= 1)
        def _():
            wait_tile(1 - s)

        @pl.when(i == pl.num_programs(0) - 1)
        def _():
            wait_tile(s)

    for s in range(2):
        pl.when(i % 2 == s)(functools.partial(step, s))


def _moe_dispatch(h, norm_g, dest, pad_rows, n_rows):
    n, d = h.shape
    tm = min(256, n)
    return pl.pallas_call(
        _dispatch_body,
        grid_spec=pltpu.PrefetchScalarGridSpec(
            num_scalar_prefetch=1, grid=(n // tm,),
            in_specs=[pl.BlockSpec((1, 1, tm * TOP_K), lambda i, pr: (i, 0, 0), memory_space=pltpu.SMEM),
                      pl.BlockSpec((tm, d), lambda i, pr: (i, 0)),
                      pl.BlockSpec((1, d), lambda i, pr: (0, 0))],
            out_specs=pl.BlockSpec(memory_space=pl.ANY),
            scratch_shapes=[pltpu.VMEM((2, tm, d), F32), pltpu.VMEM((MOE_ROWS, d), F32),
                            pltpu.SemaphoreType.DMA((2,)), pltpu.SemaphoreType.DMA(())]),
        out_shape=jax.ShapeDtypeStruct((n_rows, d), F32),
        compiler_params=_cparams(("arbitrary",)),
        name="moe_dispatch",
    )(pad_rows, dest.reshape(n // tm, 1, tm * TOP_K), h, norm_g.reshape(1, d))


def _expert_body(be_ref, nu_ref, xs_ref, wgu_ref, bgu_ref, wd_ref, bd_ref, ys_ref, wgu_bf, wd_bf):
    i = pl.program_id(0)
    ed = wd_ref.shape[2]

    @pl.when((i == 0) | (be_ref[i] != be_ref[jnp.maximum(i - 1, 0)]))
    def _():
        wgu_bf[...] = wgu_ref[0, 0].astype(BF16)
        wd_bf[...] = wd_ref[0, 0].astype(BF16)

    @pl.when(i < nu_ref[0])
    def _():
        x = xs_ref[...].astype(BF16)
        gu = _dot(x, wgu_bf[...]) + bgu_ref[0]
        gate = jnp.minimum(gu[:, :ed], SWIGLU_LIMIT)
        up = jnp.clip(gu[:, ed:], -SWIGLU_LIMIT, SWIGLU_LIMIT)
        glu = gate * jax.nn.sigmoid(gate * SWIGLU_ALPHA)
        ys_ref[...] = _dot(((up + 1.0) * glu).astype(BF16), wd_bf[...]) + bd_ref[0]

    @pl.when(i >= nu_ref[0])
    def _():
        ys_ref[...] = jnp.zeros_like(ys_ref)


def _moe_experts(xs, blk_expert, n_used, w_gu, b_gu, w_d, b_d, layer):
    n_rows, d = xs.shape
    ed = w_d.shape[2]
    n_blk = n_rows // MOE_ROWS
    return pl.pallas_call(
        _expert_body,
        grid_spec=pltpu.PrefetchScalarGridSpec(
            num_scalar_prefetch=2, grid=(n_blk,),
            in_specs=[pl.BlockSpec((MOE_ROWS, d), lambda i, be, nu: (jnp.maximum(jnp.minimum(i, nu[0] - 1), 0), 0)),
                      pl.BlockSpec((1, 1, d, 2 * ed), lambda i, be, nu: (layer, be[i], 0, 0)),
                      pl.BlockSpec((1, 1, 2 * ed), lambda i, be, nu: (be[i], 0, 0)),
                      pl.BlockSpec((1, 1, ed, d), lambda i, be, nu: (layer, be[i], 0, 0)),
                      pl.BlockSpec((1, 1, d), lambda i, be, nu: (be[i], 0, 0))],
            out_specs=pl.BlockSpec((MOE_ROWS, d), lambda i, be, nu: (i, 0)),
            scratch_shapes=[pltpu.VMEM((d, 2 * ed), BF16), pltpu.VMEM((ed, d), BF16)]),
        out_shape=jax.ShapeDtypeStruct((n_rows, d), F32),
        compiler_params=_cparams(("arbitrary",)),
        name="moe_experts",
    )(blk_expert, n_used, xs, w_gu, b_gu.reshape(N_EXPERTS, 1, 2 * ed), w_d, b_d.reshape(N_EXPERTS, 1, d))


def _combine_body(dest_ref, next_ref, rt_ref, h_ref, p_ref, pw_ref, pg_ref, pn_ref, ys_hbm, o_ref, buf, sem):
    i = pl.program_id(0)
    n = pl.num_programs(0)
    tm = h_ref.shape[0]
    slot = i % 2

    def row_copy(idx_ref, j, k, s):
        r = idx_ref[0, 0, j * TOP_K + k]
        return pltpu.make_async_copy(ys_hbm.at[pl.ds(r, 1), :], buf.at[s, k, pl.ds(j, 1), :], sem.at[s])

    def wait_tile(s):
        for k in range(TOP_K):
            pltpu.make_async_copy(ys_hbm.at[pl.ds(0, tm), :], buf.at[s, k], sem.at[s]).wait()

    @pl.when(i == 0)
    def _():
        def first(j, carry):
            for k in range(TOP_K):
                row_copy(dest_ref, j, k, 0).start()
            return carry
        lax.fori_loop(0, tm, first, 0)

    def step(s):
        wait_tile(s)
        for j in range(tm):
            for k in range(TOP_K):
                row_copy(next_ref, j, k, 1 - s).start(priority=k % 2)
        rt = rt_ref[...]
        h = h_ref[...]
        for k in range(TOP_K):
            h = h + rt[:, TOP_K + k:TOP_K + k + 1] * buf[s, k]
        emb = _dot(p_ref[...].astype(BF16), pw_ref[...])
        gate = jax.nn.sigmoid(_dot(_rms(h, pn_ref[...]).astype(BF16), pg_ref[...]))
        o_ref[...] = h + emb * gate

        @pl.when(i == n - 1)
        def _():
            wait_tile(1 - s)

    for s in range(2):
        pl.when(slot == s)(functools.partial(step, s))


def _moe_combine_ple(h, route, dest, ys, p, ple_w, ple_gate_w, ple_norm):
    n, d = h.shape
    pd = p.shape[1]
    tm = min(256, n)
    nt = n // tm
    full = lambda *shape: pl.BlockSpec(shape, lambda i: (0,) * len(shape))
    dest3 = dest.reshape(nt, 1, tm * TOP_K)
    return pl.pallas_call(
        _combine_body,
        grid=(nt,),
        in_specs=[pl.BlockSpec((1, 1, tm * TOP_K), lambda i: (i, 0, 0), memory_space=pltpu.SMEM),
                  pl.BlockSpec((1, 1, tm * TOP_K), lambda i: (jnp.minimum(i + 1, nt - 1), 0, 0),
                               memory_space=pltpu.SMEM),
                  pl.BlockSpec((tm, LANES), lambda i: (i, 0)),
                  pl.BlockSpec((tm, d), lambda i: (i, 0)),
                  pl.BlockSpec((tm, pd), lambda i: (i, 0)),
                  full(pd, d), full(d, d), full(1, d),
                  pl.BlockSpec(memory_space=pl.ANY)],
        out_specs=pl.BlockSpec((tm, d), lambda i: (i, 0)),
        out_shape=jax.ShapeDtypeStruct((n, d), F32),
        scratch_shapes=[pltpu.VMEM((2, TOP_K, tm, d), F32), pltpu.SemaphoreType.DMA((2,))],
        compiler_params=_cparams(("arbitrary",)),
        name="moe_combine_ple",
    )(dest3, dest3, route, h, p, ple_w.astype(BF16), ple_gate_w.astype(BF16), ple_norm.reshape(1, d), ys)


def _moe_ple_layer(h, norm_g, router_w, router_b, w_gu, b_gu, w_d, b_d, layer, p, ple_w, ple_gate_w, ple_norm):
    n, _ = h.shape
    route, counts = _moe_route(h, norm_g, router_w, router_b)
    counts = counts[0].astype(jnp.int32)
    pad_counts = (counts + MOE_ROWS - 1) // MOE_ROWS * MOE_ROWS
    pad_ends = jnp.cumsum(pad_counts)
    pad_starts = pad_ends - pad_counts
    route_t = route[:, :3 * TOP_K].T
    top_idx = route_t[:TOP_K].astype(jnp.int32)
    rank = route_t[2 * TOP_K:].astype(jnp.int32)
    dest = (pad_starts[top_idx] + rank).T.reshape(-1)
    n_blk = -(-(n * TOP_K) // MOE_ROWS) + N_EXPERTS
    blk_start = jnp.arange(n_blk, dtype=jnp.int32) * MOE_ROWS
    blk_expert = jnp.minimum(jnp.sum((pad_ends[None, :] <= blk_start[:, None]).astype(jnp.int32), axis=1),
                             N_EXPERTS - 1)
    n_used = (pad_ends[-1:] // MOE_ROWS).astype(jnp.int32)
    tail = jnp.minimum(pad_ends[-1] + jnp.arange(N_EXPERTS, dtype=jnp.int32) * MOE_ROWS, (n_blk - 1) * MOE_ROWS)
    clear = jnp.concatenate([jnp.maximum(pad_ends - MOE_ROWS, 0), tail]).astype(jnp.int32)
    xs = _moe_dispatch(h, norm_g, dest, clear, n_blk * MOE_ROWS)
    ys = _moe_experts(xs, blk_expert, n_used, w_gu, b_gu, w_d, b_d, layer)
    return _moe_combine_ple(h, route, dest, ys, p, ple_w, ple_gate_w, ple_norm)


def _kv_body(h_ref, ng_ref, w_ref, seg_ref, kn_ref, kc_ref, vc_ref, ks_ref, vs_ref, kw_ref, vw_ref):
    ts = h_ref.shape[1]
    gw = N_KV_GROUPS * HEAD_DIM
    step = pl.program_id(1)
    last = pl.num_programs(1) - 1
    st = jnp.clip(step - 1, 0, last - 2)
    is_pad = (step == 0) | (step == last)
    hn = _rms(h_ref[0], ng_ref[...]).astype(BF16)
    kv = _dot(hn, w_ref[...])

    def knorm(x, j):
        ms = _dot_split(x * x, seg_ref[...])
        return x * lax.rsqrt(ms + NORM_EPS) * kn_ref[j]

    k_c, v_c = kv[:, 0:gw], kv[:, gw:2 * gw]
    k_s, v_s = knorm(kv[:, 2 * gw:3 * gw], 1), kv[:, 3 * gw:4 * gw]
    k_w, v_w = knorm(kv[:, 4 * gw:5 * gw], 2), kv[:, 5 * gw:6 * gw]
    tok = st * ts + lax.broadcasted_iota(jnp.int32, (ts, LANES), 0)
    blk = lax.broadcasted_iota(jnp.int32, (ts, LANES), 1)
    onehot = (tok // SEL_BLOCK == blk).astype(BF16)
    one_col = (blk == HEAD_DIM).astype(F32)
    for g in range(N_KV_GROUPS):
        kc_ref[0, g] = _low_lanes(_head_lanes(k_c, g), 0.0)
        vc_ref[0, g] = _low_lanes(_head_lanes(v_c, g), 0.0)
        ks_ref[0, g] = jnp.concatenate([_low_lanes(_head_lanes(k_s, g), 0.0).astype(BF16), onehot], axis=1)
        vs_ref[0, g] = _low_lanes(_head_lanes(v_s, g), one_col).astype(BF16)
        kw_ref[0, g] = jnp.where(is_pad, one_col, _low_lanes(_head_lanes(k_w, g), 0.0)).astype(BF16)
        vw_ref[0, g] = jnp.where(is_pad, 0.0, _low_lanes(_head_lanes(v_w, g), one_col)).astype(BF16)


def _kv_project(h3, kv_norm, kv_w, k_norm):
    b, s, d = h3.shape
    gw = N_KV_GROUPS * HEAD_DIM
    ts = min(512, s)
    seg = jnp.asarray(np.tile(np.kron(np.eye(N_KV_GROUPS), np.full((HEAD_DIM, HEAD_DIM), 1.0 / HEAD_DIM)),
                              (2, 1)), F32).astype(BF16)
    kn = jnp.tile(k_norm, (1, N_KV_GROUPS)).reshape(N_BRANCH, 1, gw)
    assert ts == WINDOW, "the window keys' front padding is one sequence tile"
    nt = s // ts
    full = lambda *shape: pl.BlockSpec(shape, lambda bi, si: (0,) * len(shape))
    tile = lambda si: jnp.clip(si - 1, 0, nt - 1)
    hd = lambda w: pl.BlockSpec((1, N_KV_GROUPS, ts, w), lambda bi, si: (bi, 0, tile(si), 0))
    padded = pl.BlockSpec((1, N_KV_GROUPS, ts, LANES), lambda bi, si: (bi, 0, si, 0))
    sds = lambda w, dt, rows=s: jax.ShapeDtypeStruct((b, N_KV_GROUPS, rows, w), dt)
    return pl.pallas_call(
        _kv_body,
        grid=(b, nt + 2),
        in_specs=[pl.BlockSpec((1, ts, d), lambda bi, si: (bi, tile(si), 0)), full(1, d),
                  full(d, N_KV_SLOTS * gw), full(2 * gw, gw), full(N_BRANCH, 1, gw)],
        out_specs=[hd(LANES), hd(LANES), hd(2 * LANES), hd(LANES), padded, padded],
        out_shape=[sds(LANES, F32), sds(LANES, F32), sds(2 * LANES, BF16), sds(LANES, BF16),
                   sds(LANES, BF16, s + 2 * ts), sds(LANES, BF16, s + 2 * ts)],
        compiler_params=_cparams(("arbitrary", "arbitrary")),
        name="kv_project",
    )(h3, kv_norm.reshape(1, d), kv_w.astype(BF16), seg, kn)


def _compress_body(kc_ref, vc_ref, pos_ref, w1_ref, b1_ref, w2_ref, kn_ref, ko_ref, vo_ref):
    nc = kc_ref.shape[2] // CMP_STRIDE
    row = lax.broadcasted_iota(jnp.int32, (nc, 1), 0)
    valid = row < nc - 1
    lane = lax.broadcasted_iota(jnp.int32, (nc, LANES), 1)

    def chunks(ref):
        parts = []
        for l in range(0, CMP_STRIDE, 2):
            even = ref[0, 0, pl.ds(l, nc, stride=CMP_STRIDE), :]
            odd = ref[0, 0, pl.ds(l + 1, nc, stride=CMP_STRIDE), :]
            parts.append(jnp.where(lane < HEAD_DIM, even, pltpu.roll(odd, HEAD_DIM, 1)))
        return jnp.concatenate(parts, axis=1)

    def compress(c, j):
        a = _dot((c + pos_ref[j, 0]).astype(BF16), w1_ref[j, 0])
        bm = _dot((c + pos_ref[j, 1]).astype(BF16), w1_ref[j, 1])
        nxt = pltpu.roll(bm, nc - 1, 0)
        hid = jax.nn.gelu(a + nxt + b1_ref[j])
        return _dot(hid.astype(BF16), w2_ref[j])

    kraw = compress(chunks(kc_ref), 0)
    ms = jnp.sum(kraw * kraw, axis=-1, keepdims=True) * (1.0 / HEAD_DIM)
    kcmp = kraw * lax.rsqrt(ms + NORM_EPS) * kn_ref[...]
    vcmp = jnp.where(valid, compress(chunks(vc_ref), 1), 0.0)
    flag = (lane == HEAD_DIM).astype(F32)
    ko_ref[0, 0, 0:nc, :] = flag
    ko_ref[0, 0, nc:2 * nc, :] = jnp.where(valid, kcmp, flag)
    vo_ref[0, 0, 0:nc, :] = flag
    vo_ref[0, 0, nc:2 * nc, :] = jnp.where(lane == HEAD_DIM, 1.0, vcmp)


def _compress(kc, vc, cmp_pos, cmp_w1, cmp_b1, cmp_w2, k_norm0):
    b, g, s, _ = kc.shape
    dh = HEAD_DIM
    nc = s // CMP_STRIDE
    half = CMP_STRIDE * dh
    pos = cmp_pos.reshape(2, 2, 1, half)
    w1 = cmp_w1.reshape(2, 2, half, CMP_HIDDEN).astype(BF16)
    w2 = jnp.pad(cmp_w2, ((0, 0), (0, 0), (0, LANES - dh))).astype(BF16)
    kn = jnp.pad(k_norm0, (0, LANES - dh)).reshape(1, LANES)
    full = lambda *shape: pl.BlockSpec(shape, lambda bi, gi: (0,) * len(shape))
    blk = pl.BlockSpec((1, 1, s, LANES), lambda bi, gi: (bi, gi, 0, 0))
    out = pl.BlockSpec((1, 1, 2 * nc, LANES), lambda bi, gi: (bi, gi, 0, 0))
    return pl.pallas_call(
        _compress_body,
        grid=(b, g),
        in_specs=[blk, blk, full(2, 2, 1, half), full(2, 2, half, CMP_HIDDEN), full(2, 1, CMP_HIDDEN),
                  full(2, CMP_HIDDEN, LANES), full(1, LANES)],
        out_specs=[out, out],
        out_shape=[jax.ShapeDtypeStruct((b, g, 2 * nc, LANES), F32)] * 2,
        compiler_params=_cparams(("arbitrary", "arbitrary")),
        name="kv_compress",
    )(kc, vc, pos, w1, cmp_b1.reshape(2, 1, CMP_HIDDEN), w2, kn)


def _qproj_body(h_ref, ng_ref, w_ref, bg_ref, ind_ref, indt_ref, qn_ref, q_ref, gate_ref):
    hd = N_HEADS * HEAD_DIM
    xn = _rms(h_ref[0], ng_ref[...]).astype(BF16)
    proj = _dot(xn, w_ref[...])
    q = proj[:, :hd]
    ms = _dot_split(q * q, ind_ref[...]) * (1.0 / HEAD_DIM)
    scale = _dot_split(lax.rsqrt(ms + NORM_EPS), indt_ref[...])
    qn = q * scale * qn_ref[...] * (HEAD_DIM ** -0.5 * LOG2E)
    lane = lax.broadcasted_iota(jnp.int32, (q.shape[0], LANES), 1)
    fill = jnp.where(lane == HEAD_DIM, NEG_INF, 0.0)
    for h in range(N_HEADS):
        q_ref[0, h] = _low_lanes(_head_lanes(qn, h), fill).astype(BF16)
    gate_ref[0] = jax.nn.sigmoid(proj[:, hd:] + bg_ref[...])


def _q_project(h3, norm_g, w_in, b_gate, q_norm):
    b, s, d = h3.shape
    hd = N_HEADS * HEAD_DIM
    ng = N_BRANCH * N_HEADS
    ts = min(512, s)
    w = jnp.pad(w_in, ((0, 0), (0, LANES - ng))).astype(BF16)
    bg = jnp.pad(b_gate, (0, LANES - ng)).reshape(1, LANES)
    ind = np.zeros((hd, LANES), np.float32)
    ind[np.arange(hd), np.arange(hd) // HEAD_DIM] = 1.0
    full = lambda *shape: pl.BlockSpec(shape, lambda bi, si: (0,) * len(shape))
    return pl.pallas_call(
        _qproj_body,
        grid=(b, s // ts),
        in_specs=[pl.BlockSpec((1, ts, d), lambda bi, si: (bi, si, 0)), full(1, d), full(d, hd + LANES),
                  full(1, LANES), full(2 * hd, LANES), full(2 * LANES, hd), full(1, hd)],
        out_specs=[pl.BlockSpec((1, N_HEADS, ts, LANES), lambda bi, si: (bi, 0, si, 0)),
                   pl.BlockSpec((1, ts, LANES), lambda bi, si: (bi, si, 0))],
        out_shape=[jax.ShapeDtypeStruct((b, N_HEADS, s, LANES), BF16),
                   jax.ShapeDtypeStruct((b, s, LANES), F32)],
        compiler_params=_cparams(("arbitrary", "arbitrary")),
        name="nsa_qproj",
    )(h3, norm_g.reshape(1, d), w, bg, jnp.asarray(np.tile(ind, (2, 1))).astype(BF16),
      jnp.asarray(np.tile(ind.T, (2, 1))).astype(BF16), jnp.tile(q_norm, N_HEADS).reshape(1, hd))


def _t5_bucket_np(dist):
    n = np.maximum(dist, 0)
    max_exact = REL_BUCKETS // 2
    nf = np.maximum(n, 1).astype(np.float64)
    large = max_exact + (np.log(nf / max_exact) / math.log(REL_MAX_DIST / max_exact)
                         * (REL_BUCKETS - max_exact)).astype(np.int64)
    return np.where(n < max_exact, n, np.minimum(large, REL_BUCKETS - 1))


def _bias_table(rel_bias, dist, valid):
    r = N_HEADS // N_KV_GROUPS
    tab = rel_bias.astype(F32).T.reshape(N_KV_GROUPS, r, REL_BUCKETS)
    onehot = jnp.asarray(_t5_bucket_np(dist)[..., None] == np.arange(REL_BUCKETS), F32)
    bias = jnp.einsum('xqln,grn->gxrql', onehot, tab, precision=lax.Precision.HIGHEST) * LOG2E
    return jnp.where(jnp.asarray(valid)[None, :, None, :, :], bias, NEG_INF)


def _n_delta(seq):
    d = np.arange(seq + SEL_BLOCK)
    bk = _t5_bucket_np(d)
    change = np.nonzero(bk[1:] != bk[:-1])[0]
    d_const = int(change[-1]) + 1 if change.size else 0
    return -(-(d_const + SEL_BLOCK - 1) // SEL_BLOCK) + 1


def _attn_tables(rel_bias, seq):
    r = N_HEADS // N_KV_GROUPS
    qi = np.arange(Q_BLOCK)[:, None]
    rows = lambda t: jnp.transpose(t, (0, 2, 1, 3, 4)).reshape(N_KV_GROUPS, r * Q_PAIR * Q_BLOCK, t.shape[-1])
    nc = seq // CMP_STRIDE
    j = np.arange(nc)[None, :]
    dist_c = np.stack([qi - (CMP_BLOCK - 1) - Q_BLOCK * (Q_PAIR - u) + CMP_STRIDE * (nc - j)
                       for u in range(Q_PAIR)])
    rc = rows(_bias_table(rel_bias, dist_c, dist_c >= 0))
    nd = _n_delta(seq)
    delta = np.arange(-1, nd + 1)[:, None, None]
    kj = np.arange(2 * SEL_BLOCK)[None, None, :]
    dist_s = SEL_BLOCK * (delta - kj // SEL_BLOCK) + qi[None] - kj % SEL_BLOCK
    bt = _bias_table(rel_bias, dist_s, dist_s >= 0)
    jw = np.arange(WINDOW + Q_PAIR * Q_BLOCK)[None, :]
    dist_w = np.stack([Q_BLOCK * u + qi - jw + WINDOW for u in range(Q_PAIR)])
    wb = rows(_bias_table(rel_bias, dist_w, (dist_w >= 0) & (dist_w < WINDOW)))
    n_sel = seq // SEL_BLOCK
    cs = np.arange(nc) * CMP_STRIDE
    ss = np.arange(n_sel) * SEL_BLOCK
    ov = np.clip(np.minimum(cs[:, None] + CMP_BLOCK, ss[None, :] + SEL_BLOCK)
                 - np.maximum(cs[:, None], ss[None, :]), 0, None) / CMP_BLOCK
    return rc, bt, wb, jnp.asarray(ov.T.astype(np.float32)).astype(BF16)


def _attn_body(q_ref, gate_ref, kc_ref, vc_ref, ks_ref, vs_ref, kw_ref, vw_ref, rc_ref, bt_ref, wb_ref, ovt_ref,
               o_ref, sa_ref, sb_ref):
    g = pl.program_id(1)
    i0 = Q_PAIR * pl.program_id(2)
    r = q_ref.shape[1]
    pq = Q_PAIR * Q_BLOCK
    rq = r * pq
    nc = rc_ref.shape[2]
    n_sel = ovt_ref.shape[0]
    nd = bt_ref.shape[1] - 2
    wl = wb_ref.shape[2]
    per = SEL_BLOCK // CMP_STRIDE

    heads = [slice(h * pq, (h + 1) * pq) for h in range(r)]
    q_pad = q_ref[0].reshape(rq, LANES)

    end = pl.multiple_of(per * (i0 + Q_PAIR), Q_PAIR * per)
    kcw = kc_ref[0, 0, pl.ds(end, nc), :].astype(BF16)
    vcw = vc_ref[0, 0, pl.ds(end, nc), :].astype(BF16)
    qpos = i0 * Q_BLOCK + lax.broadcasted_iota(jnp.int32, (rq, 1), 0) % pq
    sc = _dot_nt(q_pad, kcw) + rc_ref[0]
    p_c = jnp.exp2((sc - jnp.max(sc, axis=-1, keepdims=True)).astype(BF16))
    o_c = _normalize(_dot(p_c, vcw)) * (qpos >= CMP_BLOCK - 1).astype(F32)

    imp_r = _dot_nt(ovt_ref[...], p_c)
    inv_l = 1.0 / _dot_nt(jnp.ones((8, nc), BF16), p_c)[0:1]
    imp = imp_r[:, heads[0]] * inv_l[:, heads[0]]
    for h in range(1, r):
        imp = imp + imp_r[:, heads[h]] * inv_l[:, heads[h]]
    tpos = i0 * Q_BLOCK + lax.broadcasted_iota(jnp.int32, (1, pq), 1)
    imp = imp * (tpos >= CMP_BLOCK - 1).astype(F32)

    anchor = (jnp.max(imp, axis=(0, 1), keepdims=True) * 0.0).astype(BF16)
    ws = pl.multiple_of(i0 * Q_BLOCK, pq)
    sw = _dot_nt(q_pad + anchor, kw_ref[0, 0, pl.ds(ws, wl), :]) + wb_ref[0]
    p_w = jnp.exp2((sw - jnp.max(sw, axis=-1, keepdims=True)).astype(BF16))
    o_w = _normalize(_dot(p_w, vw_ref[0, 0, pl.ds(ws, wl), :]))

    shift = i0 + Q_PAIR
    blk_rel = lax.broadcasted_iota(jnp.int32, (n_sel, pq), 0)
    blk = blk_rel + shift - n_sel
    cur = i0 + lax.broadcasted_iota(jnp.int32, (n_sel, pq), 1) // Q_BLOCK
    forced = (blk == 0) | (blk == cur) | (blk == cur - 1)
    imp = jnp.where(forced, FORCE, jnp.where(blk > cur, NEG_INF, imp))
    imp = jnp.where(blk < 0, -jnp.inf, imp)
    ids = blk_rel.astype(F32)
    sel = jnp.zeros((n_sel, pq), F32)
    for _ in range(min(SEL_TOP, n_sel)):
        mx = jnp.max(imp, axis=0, keepdims=True)
        ix = jnp.min(jnp.where(imp == mx, ids, float(n_sel)), axis=0, keepdims=True)
        hit = ids == ix
        sel = jnp.where(hit, 1.0, sel)
        imp = jnp.where(hit, -jnp.inf, imp)
    unsel = jnp.where((sel > 0.0) & (blk >= 0), 0.0, NEG_INF)
    unsel = pltpu.roll(unsel.T, shift % n_sel, 1).astype(BF16)
    if n_sel < LANES:
        unsel = jnp.concatenate([unsel, jnp.zeros((pq, LANES - n_sel), BF16)], axis=1)

    q_aug = jnp.concatenate([q_pad, jnp.concatenate([unsel] * r, axis=0)], axis=1)
    kchunk = sa_ref.shape[1]
    n_chunks = ks_ref.shape[2] // kchunk
    cblocks = kchunk // SEL_BLOCK
    pairs = cblocks // 2

    def scores_to(dst, c):
        start = pl.multiple_of(jnp.minimum(c, n_chunks - 1) * kchunk, kchunk)
        sc = _dot_nt(q_aug, ks_ref[0, 0, pl.ds(start, kchunk), :])
        d0 = i0 - c * cblocks
        tiles = [[jnp.clip(d0 + u - 2 * pm, -1, nd) + 1 for pm in range(pairs)] for u in range(Q_PAIR)]
        dst[...] = sc + jnp.concatenate(
            [jnp.concatenate([bt_ref[0, t, h] for t in tiles[u]], axis=1)
             for h in range(r) for u in range(Q_PAIR)], axis=0)

    def absorb(src, c, m, acc):
        start = pl.multiple_of(c * kchunk, kchunk)
        sc = src[...]
        m_new = jnp.maximum(m, jnp.max(sc, axis=-1, keepdims=True))
        p = jnp.exp2((sc - m_new).astype(BF16))
        acc = jnp.exp2(m - m_new) * acc + _dot(p, vs_ref[0, 0, pl.ds(start, kchunk), :])
        return m_new, acc

    def two_chunks(j, carry):
        scores_to(sb_ref, 2 * j + 1)
        carry = absorb(sa_ref, 2 * j, *carry)
        scores_to(sa_ref, 2 * j + 2)
        return absorb(sb_ref, 2 * j + 1, *carry)

    scores_to(sa_ref, 0)
    init = (jnp.full((rq, 1), NEG_INF, F32), jnp.zeros((rq, LANES), F32))
    rem = (i0 + Q_PAIR) % (2 * cblocks)
    trips = (i0 + Q_PAIR) // (2 * cblocks) + (rem > cblocks).astype(jnp.int32)
    state = lax.fori_loop(0, trips, two_chunks, init)
    _, acc_s = lax.cond((rem > 0) & (rem <= cblocks),
                        lambda m, acc: absorb(sa_ref, 2 * trips, m, acc), lambda m, acc: (m, acc), *state)
    o_s = _normalize(acc_s)

    gates = gate_ref[0]
    glane = lax.broadcasted_iota(jnp.int32, gates.shape, 1)
    outs = []
    for h in range(r):
        head = g * r + h
        gs = [jnp.sum(jnp.where(glane == br * N_HEADS + head, gates, 0.0), axis=-1, keepdims=True)
              for br in range(N_BRANCH)]
        outs.append(gs[0] * o_c[heads[h]] + gs[1] * o_s[heads[h]] + gs[2] * o_w[heads[h]])
    o_ref[0] = jnp.concatenate(
        [outs[h] + pltpu.roll(outs[h + 1], HEAD_DIM, 1) for h in range(0, r, 2)], axis=1)


def _nsa_attention(q, gates, kc_pad, vc_pad, ks_aug, vs, kw_pad, vw_pad, tables):
    b, _, s, _ = q.shape
    r = N_HEADS // N_KV_GROUPS
    rc, bt, wb, ov = tables
    n_qb = s // Q_BLOCK
    per_bg = lambda a: pl.BlockSpec((1, 1) + a.shape[2:], lambda bi, gi, qi: (bi, gi, 0, 0))
    per_g = lambda a: pl.BlockSpec((1,) + a.shape[1:], lambda bi, gi, qi: (gi,) + (0,) * (a.ndim - 1))
    pq = Q_PAIR * Q_BLOCK
    return pl.pallas_call(
        _attn_body,
        grid=(b, N_KV_GROUPS, n_qb // Q_PAIR),
        in_specs=[pl.BlockSpec((1, r, pq, LANES), lambda bi, gi, qi: (bi, gi, qi, 0)),
                  pl.BlockSpec((1, pq, LANES), lambda bi, gi, qi: (bi, qi, 0)),
                  per_bg(kc_pad), per_bg(vc_pad), per_bg(ks_aug), per_bg(vs), per_bg(kw_pad), per_bg(vw_pad),
                  per_g(rc), per_g(bt), per_g(wb),
                  pl.BlockSpec(ov.shape, lambda bi, gi, qi: (0, 0))],
        out_specs=pl.BlockSpec((1, pq, r * HEAD_DIM), lambda bi, gi, qi: (bi, qi, gi)),
        out_shape=jax.ShapeDtypeStruct((b, s, N_HEADS * HEAD_DIM), F32),
        scratch_shapes=[pltpu.VMEM((r * pq, min(KEY_CHUNK, s // 2)), F32)] * 2,
        compiler_params=_cparams(("arbitrary", "arbitrary", "arbitrary")),
        name="nsa_attention",
    )(q, gates, kc_pad, vc_pad, ks_aug, vs, kw_pad, vw_pad, rc, bt, wb, ov)


def _outproj_body(a_ref, h_ref, w_ref, o_ref):
    o_ref[...] = h_ref[...] + _dot(a_ref[...].astype(BF16), w_ref[...])


def _out_project(attn, h, w_out):
    n, d = h.shape
    hd = attn.shape[1]
    tm = min(512, n)
    return pl.pallas_call(
        _outproj_body,
        grid=(n // tm,),
        in_specs=[pl.BlockSpec((tm, hd), lambda i: (i, 0)), pl.BlockSpec((tm, d), lambda i: (i, 0)),
                  pl.BlockSpec((hd, d), lambda i: (0, 0))],
        out_specs=pl.BlockSpec((tm, d), lambda i: (i, 0)),
        out_shape=jax.ShapeDtypeStruct((n, d), F32),
        compiler_params=_cparams(("arbitrary",)),
        name="nsa_outproj",
    )(attn, h, w_out.astype(BF16))


def kernel(x, p, rel_bias, norm_mix, norm_ffn, a_w_in, a_ln_g, a_ln_b, a_w_s, a_b_s, a_w_out, kv_norm, kv_w, cmp_pos, cmp_w1, cmp_b1, cmp_w2, k_norm, b_w_in, b_b_gate, q_norm, b_w_out, router_w, router_b, e_w_gu, e_b_gu, e_w_d, e_b_d, ple_w, ple_gate_w, ple_norm):
    b, s, d = x.shape
    n = b * s
    pf = p.reshape(p.shape[0], n, p.shape[-1])

    def moe_ple(h, i):
        return _moe_ple_layer(h, norm_ffn[i], router_w[i], router_b[i], e_w_gu, e_b_gu[i], e_w_d, e_b_d[i], i,
                              pf[i], ple_w[i], ple_gate_w[i], ple_norm[i])

    h = _gmlp_layer(x.reshape(n, d), norm_mix[0], a_w_in[0], a_ln_g[0], a_ln_b[0], a_w_s[0], a_b_s[0], a_w_out[0])
    h = moe_ple(h, 0)

    h3 = h.reshape(b, s, d)
    kc, vc, ks_aug, vs, kw_pad, vw_pad = _kv_project(h3, kv_norm, kv_w, k_norm)
    kc_pad, vc_pad = _compress(kc, vc, cmp_pos, cmp_w1, cmp_b1, cmp_w2, k_norm[0])

    q, gates = _q_project(h3, norm_mix[1], b_w_in[0], b_b_gate[0], q_norm[0])
    attn = _nsa_attention(q, gates, kc_pad, vc_pad, ks_aug, vs, kw_pad, vw_pad, _attn_tables(rel_bias, s))
    h = _out_project(attn.reshape(n, -1), h, b_w_out[0])
    h = moe_ple(h, 1)
    return h.reshape(b, s, d)
```
